```python
import jax
import jax.numpy as jnp
from jax import lax
import numpy as np

D_MODEL = 4096
BATCH = 4
SEQ = 2048
DEPTH = 2

HEAD_DIM = 128
D_MIX = D_MODEL
D_RG = D_MIX // 4
D_NSA = D_MIX // 2
D_HG = D_MIX - D_RG - D_NSA

RG_BLOCKS = D_RG // HEAD_DIM
RG_CONV = 4
RG_C = 8.0

NSA_HEADS = D_NSA // HEAD_DIM
NSA_KV = 2
NSA_GROUP = NSA_HEADS // NSA_KV
CMP_LEN = 32
CMP_STRIDE = 16
CMP_HIDDEN = 256
SEL_LEN = 64
SEL_TOPN = 16
WINDOW = 512
Q_BLOCK = 64

HG_HEADS = D_HG // HEAD_DIM
HG_CHUNK = 64

ALPHA = (2.0 * DEPTH) ** 0.25
BETA = (8.0 * DEPTH) ** -0.25
ADA_SCALE = 0.1
LN_EPS = 1e-5
RMS_EPS = 1e-6
NEG_INF = -1e30
FORCE_SCORE = 1e9

KV_COLS = 2 * NSA_KV * HEAD_DIM
IN_SIZES = (D_RG, D_RG,
            D_NSA, KV_COLS, KV_COLS, KV_COLS, 3 * NSA_HEADS, D_NSA,
            D_HG, D_HG, D_HG, D_HG)
N_IN = sum(IN_SIZES)

kernel_name = "hymba_rglru_nsa_hgrn2_deepnorm"


def _layer_norm(x, g, b):
    xf = x.astype(jnp.float32)
    mu = jnp.mean(xf, axis=-1, keepdims=True)
    var = jnp.mean(jnp.square(xf - mu), axis=-1, keepdims=True)
    return ((xf - mu) * lax.rsqrt(var + LN_EPS)).astype(x.dtype) * g + b


def _masked_softmax(logits, valid):
    p = jax.nn.softmax(jnp.where(valid, logits, NEG_INF), axis=-1)
    return p * valid


def _alibi_slopes():
    h = np.arange(1, NSA_HEADS + 1, dtype=np.float32)
    s = np.power(np.float32(2.0), -8.0 * h / NSA_HEADS).astype(np.float32)
    return jnp.asarray(s.reshape(NSA_KV, NSA_GROUP))


def _causal_dwconv(x, w, b):
    T = x.shape[1]
    xp = jnp.pad(x, ((0, 0), (RG_CONV - 1, 0), (0, 0)))
    y = b
    for j in range(RG_CONV):
        y = y + xp[:, j:j + T] * w[j]
    return y


def _rg_lru(x, w_a, b_a, w_x, b_x, lam):
    B, T, _ = x.shape
    xb = x.reshape(B, T, RG_BLOCKS, HEAD_DIM)
    r = jax.nn.sigmoid(jnp.einsum('btnd,nde->btne', xb, w_a).reshape(B, T, D_RG) + b_a)
    i = jax.nn.sigmoid(jnp.einsum('btnd,nde->btne', xb, w_x).reshape(B, T, D_RG) + b_x)
    log_a = (-RG_C * jax.nn.softplus(-lam) * r).astype(jnp.float32)
    a = jnp.exp(log_a)
    mult = jnp.sqrt(-jnp.expm1(2.0 * log_a))
    mult = mult.at[:, 0].set(1.0)
    bx = mult * (i * x).astype(jnp.float32)

    def combine(left, right):
        a_l, b_l = left
        a_r, b_r = right
        return a_l * a_r, a_r * b_l + b_r

    _, h = lax.associative_scan(combine, (a, bx), axis=1)
    return h.astype(x.dtype)


def _nsa(q, kv_c, kv_s, kv_w, gate_logits, pe_k, pe_v, w1_k, w2_k, w1_v, w2_v):
    B, T, _ = q.shape
    dt = q.dtype
    f32 = jnp.float32
    G, R, D = NSA_KV, NSA_GROUP, HEAD_DIM
    scale = D ** -0.5
    slopes = _alibi_slopes()
    qh = q.reshape(B, T, G, R, D)

    def kv_split(kv):
        k, v = jnp.split(kv, 2, axis=-1)
        return k.reshape(B, T, G, D), v.reshape(B, T, G, D)

    kc, vc = kv_split(kv_c)
    ks, vs = kv_split(kv_s)
    kw, vw = kv_split(kv_w)
    t_pos = jnp.arange(T)

    n_cmp = (T - CMP_LEN) // CMP_STRIDE + 1
    cmp_idx = np.arange(n_cmp)[:, None] * CMP_STRIDE + np.arange(CMP_LEN)[None, :]

    def compress(k, pe, w1, w2):
        blk = k[:, cmp_idx] + pe[:, None, :]
        blk = blk.transpose(0, 1, 3, 2, 4).reshape(B, n_cmp, G, CMP_LEN * D)
        return jax.nn.silu(blk @ w1) @ w2

    k_cmp = compress(kc, pe_k, w1_k, w2_k)
    v_cmp = compress(vc, pe_v, w1_v, w2_v)
    cmp_end = jnp.asarray(cmp_idx[:, -1])
    d_cmp = (t_pos[:, None] - cmp_end[None, :]).astype(f32)
    lg_cmp = (jnp.einsum('btgrd,bjgd->bgrtj', qh, k_cmp).astype(f32) * scale
              - slopes[None, :, :, None, None] * d_cmp)
    p_cmp = _masked_softmax(lg_cmp, d_cmp >= 0)
    o_cmp = jnp.einsum('bgrtj,bjgd->btgrd', p_cmp.astype(dt), v_cmp)

    n_sel = T // SEL_LEN
    top_n = min(SEL_TOPN, n_sel)
    s_c = np.arange(n_cmp) * CMP_STRIDE
    s_s = np.arange(n_sel) * SEL_LEN
    overlap = np.clip(np.minimum(s_c[:, None] + CMP_LEN, s_s[None, :] + SEL_LEN)
                      - np.maximum(s_c[:, None], s_s[None, :]), 0, None).astype(np.float32) / CMP_LEN
    imp = jnp.einsum('bgrtj,jn->bgtn', p_cmp, jnp.asarray(overlap))
    n_ids = jnp.arange(n_sel)
    cur = t_pos // SEL_LEN
    future = jnp.asarray(s_s)[None, :] > t_pos[:, None]
    forced = (n_ids[None, :] == 0) | (n_ids[None, :] == cur[:, None]) | (n_ids[None, :] == cur[:, None] - 1)
    imp = jnp.where(future, NEG_INF, jnp.where(forced, FORCE_SCORE, imp))
    _, sel_idx = lax.top_k(imp, top_n)

    n_qb = T // Q_BLOCK
    q_blocks = qh.reshape(B, n_qb, Q_BLOCK, G, R, D).transpose(1, 0, 2, 3, 4, 5)
    idx_blocks = sel_idx.reshape(B, G, n_qb, Q_BLOCK, top_n).transpose(2, 0, 1, 3, 4)
    ks_blk = ks.reshape(B, n_sel, SEL_LEN, G, D).transpose(0, 3, 1, 2, 4)
    vs_blk = vs.reshape(B, n_sel, SEL_LEN, G, D).transpose(0, 3, 1, 2, 4)
    kw_pad = jnp.pad(kw, ((0, 0), (WINDOW, 0), (0, 0), (0, 0)))
    vw_pad = jnp.pad(vw, ((0, 0), (WINDOW, 0), (0, 0), (0, 0)))
    b_ix = jnp.arange(B)[:, None, None, None]
    g_ix = jnp.arange(G)[None, :, None, None]
    n_keys = top_n * SEL_LEN

    def block_fn(args):
        qb, idx, qb_i = args
        q0 = qb_i * Q_BLOCK
        tq = q0 + jnp.arange(Q_BLOCK)
        kg = ks_blk[b_ix, g_ix, idx]
        vg = vs_blk[b_ix, g_ix, idx]
        pos = idx[..., None] * SEL_LEN + jnp.arange(SEL_LEN)
        d_sel = (tq[None, None, :, None, None] - pos).reshape(B, G, 1, Q_BLOCK, n_keys).astype(f32)
        lg = jnp.einsum('bqgrd,bgqnld->bgrqnl', qb, kg).reshape(B, G, R, Q_BLOCK, n_keys).astype(f32) * scale
        p_sel = _masked_softmax(lg - slopes[None, :, :, None, None] * d_sel, d_sel >= 0)
        o_sel = jnp.einsum('bgrqk,bgqkd->bqgrd', p_sel.astype(dt), vg.reshape(B, G, Q_BLOCK, n_keys, D))
        kwin = lax.dynamic_slice_in_dim(kw_pad, q0, WINDOW + Q_BLOCK, axis=1)
        vwin = lax.dynamic_slice_in_dim(vw_pad, q0, WINDOW + Q_BLOCK, axis=1)
        kpos = q0 - WINDOW + jnp.arange(WINDOW + Q_BLOCK)
        d_win = tq[:, None] - kpos[None, :]
        valid = (d_win >= 0) & (d_win < WINDOW) & (kpos[None, :] >= 0)
        lw = (jnp.einsum('bqgrd,bkgd->bgrqk', qb, kwin).astype(f32) * scale
              - slopes[None, :, :, None, None] * d_win.astype(f32))
        p_win = _masked_softmax(lw, valid)
        o_win = jnp.einsum('bgrqk,bkgd->bqgrd', p_win.astype(dt), vwin)
        return o_sel, o_win

    o_sel, o_win = lax.map(block_fn, (q_blocks, idx_blocks, jnp.arange(n_qb)))
    o_sel = o_sel.transpose(1, 0, 2, 3, 4, 5).reshape(B, T, G, R, D)
    o_win = o_win.transpose(1, 0, 2, 3, 4, 5).reshape(B, T, G, R, D)

    g = jax.nn.sigmoid(gate_logits.reshape(B, T, G, R, 3))
    o = g[..., 0:1] * o_cmp + g[..., 1:2] * o_sel + g[..., 2:3] * o_win
    return o.reshape(B, T, D_NSA)


def _hgrn2(q, f_logit, v, out_gate, lb, norm_g):
    B, T, _ = q.shape
    dt = q.dtype
    f32 = jnp.float32
    n_ch = T // HG_CHUNK
    z = f_logit.astype(f32)
    lb = lb.astype(f32)
    log_f = jnp.logaddexp(jnp.log(lb), jnp.log1p(-lb) + jax.nn.log_sigmoid(z))
    k = (1.0 - lb) * jax.nn.sigmoid(-z)
    qs = jax.nn.silu(q.astype(f32))

    def chunks(a):
        return a.reshape(B, n_ch, HG_CHUNK, HG_HEADS, HEAD_DIM).transpose(1, 0, 2, 3, 4)

    qc, kc, vc, lfc = chunks(qs), chunks(k), chunks(v.astype(f32)), chunks(log_f)
    causal = jnp.tril(jnp.ones((HG_CHUNK, HG_CHUNK), dtype=bool))

    def step(S, inp):
        q_, k_, v_, lf = inp
        b = jnp.cumsum(lf, axis=1)
        b_last = b[:, -1]
        o_inter = jnp.einsum('bthk,bhkv->bthv', q_ * jnp.exp(b), S)
        diff = b[:, :, None] - b[:, None, :]
        decay = jnp.exp(jnp.where(causal[None, :, :, None, None], diff, -jnp.inf))
        A = jnp.einsum('bthk,btshk->bhts', q_, decay * k_[:, None])
        o_intra = jnp.einsum('bhts,bshv->bthv', A, v_)
        S = S * jnp.exp(b_last)[..., None] + jnp.einsum('bshk,bshv->bhkv', k_ * jnp.exp(b_last[:, None] - b), v_)
        return S, o_inter + o_intra

    S0 = jnp.zeros((B, HG_HEADS, HEAD_DIM, HEAD_DIM), f32)
    _, o = lax.scan(step, S0, (qc, kc, vc, lfc))
    o = o.transpose(1, 0, 2, 3, 4).reshape(B, T, HG_HEADS, HEAD_DIM)
    o = o * lax.rsqrt(jnp.mean(o * o, axis=-1, keepdims=True) + RMS_EPS) * norm_g.astype(f32)
    return (o.reshape(B, T, D_HG) * jax.nn.silu(out_gate.astype(f32))).astype(dt)


def setup_inputs(seed: int = 0) -> dict:
    key = jax.random.key(seed)
    ks = jax.random.split(key, 24)
    f32 = jnp.float32

    def nrm(k, shape, s):
        return jax.random.normal(k, shape, f32) * s

    a0 = jax.random.uniform(ks[11], (DEPTH, D_RG), f32, 0.9, 0.999) ** (1.0 / RG_C)
    return {
        "x": nrm(ks[0], (BATCH, SEQ, D_MODEL), 1.0),
        "c": nrm(ks[1], (BATCH, D_MODEL), 1.0),
        "w_ada": nrm(ks[2], (DEPTH, D_MODEL, 3 * D_MODEL), ADA_SCALE * D_MODEL ** -0.5),
        "b_ada": nrm(ks[3], (DEPTH, 3 * D_MODEL), 0.02),
        "w_in": nrm(ks[4], (DEPTH, D_MODEL, N_IN), D_MODEL ** -0.5),
        "rg_conv_w": nrm(ks[5], (DEPTH, RG_CONV, D_RG), RG_CONV ** -0.5),
        "rg_conv_b": nrm(ks[6], (DEPTH, D_RG), 0.02),
        "rg_w_a": nrm(ks[7], (DEPTH, RG_BLOCKS, HEAD_DIM, HEAD_DIM), HEAD_DIM ** -0.5),
        "rg_b_a": nrm(ks[8], (DEPTH, D_RG), 0.02),
        "rg_w_x": nrm(ks[9], (DEPTH, RG_BLOCKS, HEAD_DIM, HEAD_DIM), HEAD_DIM ** -0.5),
        "rg_b_x": nrm(ks[10], (DEPTH, D_RG), 0.02),
        "rg_lambda": jnp.log(a0) - jnp.log1p(-a0),
        "nsa_pe_k": nrm(ks[12], (DEPTH, CMP_LEN, HEAD_DIM), 0.1),
        "nsa_pe_v": nrm(ks[13], (DEPTH, CMP_LEN, HEAD_DIM), 0.1),
        "nsa_cmp_w1_k": nrm(ks[14], (DEPTH, CMP_LEN * HEAD_DIM, CMP_HIDDEN), (CMP_LEN * HEAD_DIM) ** -0.5),
        "nsa_cmp_w2_k": nrm(ks[15], (DEPTH, CMP_HIDDEN, HEAD_DIM), CMP_HIDDEN ** -0.5),
        "nsa_cmp_w1_v": nrm(ks[16], (DEPTH, CMP_LEN * HEAD_DIM, CMP_HIDDEN), (CMP_LEN * HEAD_DIM) ** -0.5),
        "nsa_cmp_w2_v": nrm(ks[17], (DEPTH, CMP_HIDDEN, HEAD_DIM), CMP_HIDDEN ** -0.5),
        "hg_lower_bounds": nrm(ks[18], (DEPTH, D_HG), 1.0),
        "hg_norm_g": 1.0 + nrm(ks[19], (DEPTH, HEAD_DIM), 0.02),
        "w_out": nrm(ks[20], (DEPTH, D_MIX, D_MODEL), BETA * D_MIX ** -0.5),
        "ln_g": 1.0 + nrm(ks[21], (DEPTH, D_MODEL), 0.02),
        "ln_b": nrm(ks[22], (DEPTH, D_MODEL), 0.02),
    }


def reference(x, c, w_ada, b_ada, w_in, rg_conv_w, rg_conv_b, rg_w_a, rg_b_a, rg_w_x, rg_b_x,
              rg_lambda, nsa_pe_k, nsa_pe_v, nsa_cmp_w1_k, nsa_cmp_w2_k, nsa_cmp_w1_v, nsa_cmp_w2_v,
              hg_lower_bounds, hg_norm_g, w_out, ln_g, ln_b):
    lbs = jnp.cumsum(jax.nn.softmax(hg_lower_bounds.astype(jnp.float32), axis=0), axis=0)
    lbs = lbs - lbs[0]
    splits = [int(s) for s in np.cumsum(IN_SIZES)[:-1]]
    for l in range(DEPTH):
        mod = c @ w_ada[l] + b_ada[l]
        shift, scale, gate = jnp.split(mod, 3, axis=-1)
        u = x * (1.0 + scale[:, None]) + shift[:, None]
        proj = u @ w_in[l]
        (rg_x, rg_g, nsa_q, kv_c, kv_s, kv_w, nsa_gl, nsa_g,
         hg_q, hg_f, hg_i, hg_g) = jnp.split(proj, splits, axis=-1)
        y_rg = _rg_lru(_causal_dwconv(rg_x, rg_conv_w[l], rg_conv_b[l]),
                       rg_w_a[l], rg_b_a[l], rg_w_x[l], rg_b_x[l], rg_lambda[l]) * jax.nn.silu(rg_g)
        y_nsa = _nsa(nsa_q, kv_c, kv_s, kv_w, nsa_gl, nsa_pe_k[l], nsa_pe_v[l],
                     nsa_cmp_w1_k[l], nsa_cmp_w2_k[l], nsa_cmp_w1_v[l], nsa_cmp_w2_v[l]) * jax.nn.silu(nsa_g)
        y_hg = _hgrn2(hg_q, hg_f, hg_i, hg_g, lbs[l], hg_norm_g[l])
        y = jnp.concatenate([y_rg, y_nsa, y_hg], axis=-1) @ w_out[l]
        x = _layer_norm(ALPHA * x + (1.0 + gate[:, None]) * y, ln_g[l], ln_b[l])
    return x
```

```python
import functools

import numpy as np
import jax
import jax.numpy as jnp
from jax import lax
from jax.experimental import pallas as pl
from jax.experimental.pallas import tpu as pltpu

F32 = jnp.float32
MXU_DTYPE = jnp.bfloat16

HEAD_DIM = 128
RG_CONV = 4
RG_C = 8.0
NSA_KV = 2
NSA_GROUP = 8
CMP_LEN = 32
CMP_STRIDE = 16
SEL_LEN = 64
SEL_TOPN = 16
WINDOW = 512
Q_BLOCK = 64
HG_CHUNK = 64
HG_SUB = 8
LN_EPS = 1e-5
RMS_EPS = 1e-6
NEG_INF = -1e30
FORCE_SCORE = 1e9

LANES = 128
SUBLANES = 8
VMEM_LIMIT = 56 * 1024 * 1024


def _mm(a, b):
    return jnp.dot(a.astype(MXU_DTYPE), b.astype(MXU_DTYPE), preferred_element_type=F32)


def _mm_nt(a, b):
    return lax.dot_general(a.astype(MXU_DTYPE), b.astype(MXU_DTYPE),
                           (((1,), (1,)), ((), ())), preferred_element_type=F32)


def _mm_tn(a, b):
    return lax.dot_general(a.astype(MXU_DTYPE), b.astype(MXU_DTYPE),
                           (((0,), (0,)), ((), ())), preferred_element_type=F32)


def _sigmoid(v):
    return jax.nn.sigmoid(v)


def _silu(v):
    return v * jax.nn.sigmoid(v)


def _params(*semantics):
    return pltpu.CompilerParams(dimension_semantics=semantics, vmem_limit_bytes=VMEM_LIMIT)


def _ada_kernel(c_ref, w_ref, b_ref, o_ref):
    o_ref[0] = _mm(c_ref[...], w_ref[0]) + b_ref[0]


def _ada(c_pad, w_ada, b_ada):
    depth, d, n3 = w_ada.shape
    rows = c_pad.shape[0]
    tn = 512
    return pl.pallas_call(
        _ada_kernel,
        grid=(depth, n3 // tn),
        in_specs=[pl.BlockSpec((rows, d), lambda l, j: (0, 0)),
                  pl.BlockSpec((1, d, tn), lambda l, j: (l, 0, j)),
                  pl.BlockSpec((1, 1, tn), lambda l, j: (l, 0, j))],
        out_specs=pl.BlockSpec((1, rows, tn), lambda l, j: (l, 0, j)),
        out_shape=jax.ShapeDtypeStruct((depth, rows, n3), F32),
        compiler_params=_params("parallel", "parallel"),
        name="ada",
    )(c_pad, w_ada, b_ada.reshape(depth, 1, n3))


def _modulate_kernel(x_ref, shift_ref, scale_ref, o_ref):
    o_ref[0] = (x_ref[0] * (1.0 + scale_ref[0, 0]) + shift_ref[0, 0]).astype(o_ref.dtype)


def _modulate(x, mod4, layer):
    b, t, d = x.shape
    tt = 256
    return pl.pallas_call(
        _modulate_kernel,
        grid=(b, t // tt),
        in_specs=[pl.BlockSpec((1, tt, d), lambda i, j: (i, j, 0)),
                  pl.BlockSpec((1, 1, 1, d), lambda i, j: (layer, i, 0, 0)),
                  pl.BlockSpec((1, 1, 1, d), lambda i, j: (layer, i, 0, 1))],
        out_specs=pl.BlockSpec((1, tt, d), lambda i, j: (i, j, 0)),
        out_shape=jax.ShapeDtypeStruct((b, t, d), MXU_DTYPE),
        compiler_params=_params("parallel", "parallel"),
        name="modulate",
    )(x, mod4, mod4)


def _in_proj_kernel(a_ref, w_ref, o_ref):
    o_ref[...] = jnp.dot(a_ref[...], w_ref[0], preferred_element_type=F32)


def _in_proj(u, w_in_p, layer):
    m, d = u.shape
    n = w_in_p.shape[2]
    tm, tn = 1024, 512
    return pl.pallas_call(
        _in_proj_kernel,
        grid=(m // tm, n // tn),
        in_specs=[pl.BlockSpec((tm, d), lambda i, j: (i, 0)),
                  pl.BlockSpec((1, d, tn), lambda i, j: (layer, 0, j))],
        out_specs=pl.BlockSpec((tm, tn), lambda i, j: (i, j)),
        out_shape=jax.ShapeDtypeStruct((m, n), F32),
        compiler_params=_params("parallel", "parallel"),
        name="in_proj",
    )(u, w_in_p)


def _rglru_kernel(x_ref, g_ref, cw_ref, cb_ref, wa_ref, ba_ref, wx_ref, bx_ref, lam_ref, o_ref,
                  xp_ref, a_ref, b_ref, h_ref):
    t, cb = x_ref.shape[1], x_ref.shape[2]
    pad = SUBLANES
    xp_ref[0:pad, :] = jnp.zeros((pad, cb), F32)
    xp_ref[pad:, :] = x_ref[0]
    xc = cb_ref[0]
    for j in range(RG_CONV):
        xc = xc + xp_ref[pl.ds(pad - (RG_CONV - 1) + j, t), :] * cw_ref[0, j:j + 1, :]

    row = lax.broadcasted_iota(jnp.int32, (t, HEAD_DIM), 0)
    sub = jnp.bitwise_and(row, SUBLANES - 1)
    for n in range(cb // HEAD_DIM):
        sl = slice(n * HEAD_DIM, (n + 1) * HEAD_DIM)
        xb = xc[:, sl]
        r = _sigmoid(_mm(xb, wa_ref[0, n]) + ba_ref[0, :, sl])
        i = _sigmoid(_mm(xb, wx_ref[0, n]) + bx_ref[0, :, sl])
        neg_lam = -lam_ref[0, :, sl]
        softplus = jnp.maximum(neg_lam, 0.0) + jnp.log1p(jnp.exp(-jnp.abs(neg_lam)))
        log_a = (-RG_C * softplus) * r
        a = jnp.exp(log_a)
        mult = jnp.sqrt(-jnp.tanh(log_a) * (a * a + 1.0))
        mult = jnp.where(row == 0, 1.0, mult)
        bx = mult * (i * xb)
        s = 1
        while s < SUBLANES:
            a_sh = pltpu.roll(a, s, axis=0)
            b_sh = pltpu.roll(bx, s, axis=0)
            inside = sub >= s
            bx = jnp.where(inside, a * b_sh + bx, bx)
            a = jnp.where(inside, a * a_sh, a)
            s *= 2
        a_ref[:, sl] = a
        b_ref[:, sl] = bx

    def carry_rows(v, h):
        r0 = pl.multiple_of(v * SUBLANES, SUBLANES)
        hh = a_ref[pl.ds(r0, SUBLANES), :] * h + b_ref[pl.ds(r0, SUBLANES), :]
        h_ref[pl.ds(r0, SUBLANES), :] = hh
        return jnp.broadcast_to(hh[SUBLANES - 1:SUBLANES, :], hh.shape)

    lax.fori_loop(0, t // SUBLANES, carry_rows, jnp.zeros((SUBLANES, cb), F32))
    o_ref[0] = (h_ref[...] * _silu(g_ref[0])).astype(o_ref.dtype)


def _rglru(proj3, conv_w, conv_b, w_a, b_a, w_x, b_x, lam, layer, d_rg):
    b, t, _ = proj3.shape
    depth = conv_w.shape[0]
    cb = 256
    nblk = cb // HEAD_DIM
    ncb = d_rg // cb
    vec = lambda v: v.reshape(depth, 1, d_rg)
    vspec = pl.BlockSpec((1, 1, cb), lambda i, j: (layer, 0, j))
    wspec = pl.BlockSpec((1, nblk, HEAD_DIM, HEAD_DIM), lambda i, j: (layer, j, 0, 0))
    return pl.pallas_call(
        _rglru_kernel,
        grid=(b, ncb),
        in_specs=[pl.BlockSpec((1, t, cb), lambda i, j: (i, 0, j)),
                  pl.BlockSpec((1, t, cb), lambda i, j: (i, 0, ncb + j)),
                  pl.BlockSpec((1, RG_CONV, cb), lambda i, j: (layer, 0, j)),
                  vspec, wspec, vspec, wspec, vspec, vspec],
        out_specs=pl.BlockSpec((1, t, cb), lambda i, j: (i, 0, j)),
        out_shape=jax.ShapeDtypeStruct((b, t, d_rg), MXU_DTYPE),
        scratch_shapes=[pltpu.VMEM((t + SUBLANES, cb), F32), pltpu.VMEM((t, cb), F32),
                        pltpu.VMEM((t, cb), F32), pltpu.VMEM((t, cb), F32)],
        compiler_params=_params("parallel", "parallel"),
        name="rglru",
    )(proj3, proj3, conv_w, vec(conv_b), w_a, vec(b_a), w_x, vec(b_x), vec(lam))


def _hgrn2_kernel(q_ref, f_ref, v_ref, g_ref, lbr_ref, ng_ref, o_ref,
                  qs_ref, kk_ref, b_ref, oo_ref, *, layer):
    t = q_ref.shape[1]
    ch, sb = HG_CHUNK, HG_SUB
    nsb = ch // sb
    z = f_ref[0]
    log_sig = jnp.minimum(z, 0.0) - jnp.log1p(jnp.exp(-jnp.abs(z)))
    if layer == 0:
        log_f = log_sig
        kk = _sigmoid(-z)
    else:
        raw = lbr_ref[...]
        e = jnp.exp(raw - jnp.max(raw, axis=0, keepdims=True))
        p = e / jnp.sum(e, axis=0, keepdims=True)
        lb = p[1:2]
        for j in range(2, layer + 1):
            lb = lb + p[j:j + 1]
        log_lb = jnp.log(lb)
        other = jnp.log1p(-lb) + log_sig
        log_f = jnp.maximum(log_lb, other) + jnp.log1p(jnp.exp(-jnp.abs(log_lb - other)))
        kk = (1.0 - lb) * _sigmoid(-z)
    row = lax.broadcasted_iota(jnp.int32, (t, HEAD_DIM), 0)
    in_chunk = jnp.bitwise_and(row, ch - 1)
    bcum = log_f
    s = 1
    while s < ch:
        bcum = bcum + jnp.where(in_chunk >= s, pltpu.roll(bcum, s, axis=0), 0.0)
        s *= 2
    b_ref[...] = bcum
    kk_ref[...] = kk
    qs_ref[...] = _silu(q_ref[0])

    ones = jnp.ones((HEAD_DIM, ch), MXU_DTYPE)
    lane = lax.broadcasted_iota(jnp.int32, (sb, ch), 1)
    subrow = lax.broadcasted_iota(jnp.int32, (sb, HEAD_DIM), 0)

    def chunk(c, st):
        r0 = pl.multiple_of(c * ch, ch)
        bq = b_ref[pl.ds(r0, ch), :]
        qc = qs_ref[pl.ds(r0, ch), :]
        kc = kk_ref[pl.ds(r0, ch), :]
        vc = v_ref[0, pl.ds(r0, ch), :]
        blast = bq[ch - 1:ch, :]
        o = _mm_nt(qc * jnp.exp(bq), st)
        a_rows = []
        diag = []
        for blk in range(nsb):
            lo = blk * sb
            b_i, q_i, k_i = bq[lo:lo + sb], qc[lo:lo + sb], kc[lo:lo + sb]
            for s_ in range(sb):
                dec = jnp.where(subrow >= s_, jnp.exp(b_i - b_i[s_:s_ + 1]), 0.0)
                diag.append(q_i * (k_i[s_:s_ + 1] * dec))
            if blk == 0:
                a_rows.append(jnp.zeros((sb, ch), F32))
            else:
                m_i = bq[lo - 1:lo]
                qd = q_i * jnp.exp(b_i - m_i)
                kd = kc[0:lo] * jnp.exp(m_i - bq[0:lo])
                kd = jnp.concatenate([kd, jnp.zeros((ch - lo, HEAD_DIM), F32)], axis=0)
                a_rows.append(_mm_nt(qd, kd))
        dsum = _mm(jnp.concatenate(diag, axis=0), ones)
        for blk in range(nsb):
            acc = a_rows[blk]
            for s_ in range(sb):
                idx = blk * sb + s_
                acc = acc + jnp.where(lane == idx, dsum[idx * sb:(idx + 1) * sb], 0.0)
            a_rows[blk] = acc
        a_mat = jnp.concatenate(a_rows, axis=0)
        oo_ref[pl.ds(r0, ch), :] = o + _mm(a_mat, vc)
        kdec = kc * jnp.exp(blast - bq)
        return st * jnp.exp(blast) + _mm_tn(vc, kdec)

    lax.fori_loop(0, t // ch, chunk, jnp.zeros((HEAD_DIM, HEAD_DIM), F32))
    o = oo_ref[...]
    o = o * lax.rsqrt(jnp.mean(o * o, axis=-1, keepdims=True) + RMS_EPS) * ng_ref[0]
    o_ref[0] = (o * _silu(g_ref[0])).astype(o_ref.dtype)


def _hgrn2(proj3, col0, lower_bounds, norm_g, layer, d_hg):
    b, t, _ = proj3.shape
    depth = lower_bounds.shape[0]
    heads = d_hg // HEAD_DIM
    c0 = col0 // HEAD_DIM

    def colspec(k):
        return pl.BlockSpec((1, t, HEAD_DIM), lambda i, h: (i, 0, c0 + k * heads + h))

    return pl.pallas_call(
        functools.partial(_hgrn2_kernel, layer=layer),
        grid=(b, heads),
        in_specs=[colspec(0), colspec(1), colspec(2), colspec(3),
                  pl.BlockSpec((depth, HEAD_DIM), lambda i, h: (0, h)),
                  pl.BlockSpec((1, 1, HEAD_DIM), lambda i, h: (layer, 0, 0))],
        out_specs=pl.BlockSpec((1, t, HEAD_DIM), lambda i, h: (i, 0, h)),
        out_shape=jax.ShapeDtypeStruct((b, t, d_hg), MXU_DTYPE),
        scratch_shapes=[pltpu.VMEM((t, HEAD_DIM), F32)] * 4,
        compiler_params=_params("parallel", "parallel"),
        name="hgrn2",
    )(proj3, proj3, proj3, proj3, lower_bounds, norm_g.reshape(depth, 1, HEAD_DIM))


def _compress_kernel(x_ref, pek_ref, pev_ref, w1k_ref, w2k_ref, w1v_ref, w2v_ref, ko_ref, vo_ref):
    nrow = x_ref.shape[1]
    width = 2 * NSA_KV * HEAD_DIM
    row = lax.broadcasted_iota(jnp.int32, (nrow, HEAD_DIM), 0)
    branches = ((pek_ref, w1k_ref, w2k_ref, ko_ref), (pev_ref, w1v_ref, w2v_ref, vo_ref))
    for kv, (pe_ref, w1_ref, w2_ref, out_ref) in enumerate(branches):
        for g in range(NSA_KV):
            c0 = (kv * NSA_KV + g) * HEAD_DIM
            lo = jnp.concatenate(
                [x_ref[0, :, i * width + c0:i * width + c0 + HEAD_DIM] for i in range(CMP_STRIDE)], axis=1)
            hi = pltpu.roll(lo, nrow - 1, axis=0)
            blk = jnp.concatenate([lo, hi], axis=1) + pe_ref[0]
            hid = _silu(_mm(blk, w1_ref[0]))
            out = _mm(hid, w2_ref[0])
            out_ref[0, g] = jnp.where(row < nrow - 1, out, 0.0)


def _compress(kvc, pe_k, pe_v, w1_k, w2_k, w1_v, w2_v, layer):
    b, nrow, wide = kvc.shape
    depth = pe_k.shape[0]
    hidden = w1_k.shape[2]
    flat = CMP_LEN * HEAD_DIM
    pespec = pl.BlockSpec((1, 1, flat), lambda i: (layer, 0, 0))
    w1spec = pl.BlockSpec((1, flat, hidden), lambda i: (layer, 0, 0))
    w2spec = pl.BlockSpec((1, hidden, HEAD_DIM), lambda i: (layer, 0, 0))
    ospec = pl.BlockSpec((1, NSA_KV, nrow, HEAD_DIM), lambda i: (i, 0, 0, 0))
    oshape = jax.ShapeDtypeStruct((b, NSA_KV, nrow, HEAD_DIM), F32)
    return pl.pallas_call(
        _compress_kernel,
        grid=(b,),
        in_specs=[pl.BlockSpec((1, nrow, wide), lambda i: (i, 0, 0)),
                  pespec, pespec, w1spec, w2spec, w1spec, w2spec],
        out_specs=[ospec, ospec],
        out_shape=[oshape, oshape],
        compiler_params=_params("parallel"),
        name="nsa_compress",
    )(kvc, pe_k.reshape(depth, 1, flat), pe_v.reshape(depth, 1, flat), w1_k, w2_k, w1_v, w2_v)


def _nsa_kernel(slope_ref, q_ref, gl_ref, sg_ref, ks_ref, vs_ref, kw_ref, vw_ref, kc_ref, vc_ref,
                ov_ref, ge_ref, o_ref):
    qb = pl.program_id(2)
    q0 = qb * Q_BLOCK
    nq = NSA_GROUP * Q_BLOCK
    scale = HEAD_DIM ** -0.5
    q = q_ref[0]
    qs = jnp.concatenate([q[:, r * HEAD_DIM:(r + 1) * HEAD_DIM] for r in range(NSA_GROUP)],
                         axis=0).astype(MXU_DTYPE)
    slope = slope_ref[0]
    tq = q0 + lax.broadcasted_iota(jnp.int32, (Q_BLOCK, 1), 0)

    def stack(v):
        return jnp.concatenate([v] * NSA_GROUP, axis=0)

    def scores(k, pos, valid_fn):
        dist = tq - pos
        ok = stack(jnp.where(valid_fn(dist, pos), 1.0, 0.0)) > 0.5
        s = _mm_nt(qs, k) * scale - slope * stack(dist.astype(F32))
        return jnp.where(ok, s, NEG_INF), ok

    def online(carry, k, v, pos, valid_fn):
        m, l, acc = carry
        s, ok = scores(k, pos, valid_fn)
        m_new = jnp.maximum(m, jnp.max(s, axis=1, keepdims=True))
        alpha = jnp.exp(m - m_new)
        p = jnp.where(ok, jnp.exp(s - m_new), 0.0)
        l = alpha * l + jnp.sum(p, axis=1, keepdims=True)
        acc = alpha * acc + _mm(p, v)
        return m_new, l, acc

    def finish(carry):
        _, l, acc = carry
        return acc * jnp.where(l > 0.0, 1.0 / l, 0.0)

    init = (jnp.full((nq, 1), NEG_INF, F32), jnp.zeros((nq, 1), F32), jnp.zeros((nq, HEAD_DIM), F32))

    ncmp = kc_ref.shape[2]
    pos_c = (CMP_LEN - 1) + CMP_STRIDE * lax.broadcasted_iota(jnp.int32, (1, ncmp), 1)
    s_c, ok_c = scores(kc_ref[0, 0], pos_c, lambda d, p_: d >= 0)
    p_c = jnp.where(ok_c, jnp.exp(s_c - jnp.max(s_c, axis=1, keepdims=True)), 0.0)
    l_c = jnp.sum(p_c, axis=1, keepdims=True)
    p_c = p_c * jnp.where(l_c > 0.0, 1.0 / l_c, 0.0)
    o_cmp = _mm(p_c, vc_ref[0, 0])

    p_sum = p_c[0:Q_BLOCK]
    for r in range(1, NSA_GROUP):
        p_sum = p_sum + p_c[r * Q_BLOCK:(r + 1) * Q_BLOCK]
    imp = lax.dot_general(ov_ref[...], p_sum, (((1,), (1,)), ((), ())),
                          precision=lax.Precision.HIGHEST, preferred_element_type=F32)
    nblk = lax.broadcasted_iota(jnp.int32, imp.shape, 0)
    forced = (nblk == 0) | (nblk == qb) | (nblk == qb - 1)
    imp = jnp.where(nblk > qb, NEG_INF, jnp.where(forced, FORCE_SCORE, imp))
    n_sel = ks_ref.shape[1] // SEL_LEN
    rank = jnp.zeros(imp.shape, F32)
    for m_ in range(n_sel):
        other = imp[m_:m_ + 1, :]
        rank = rank + jnp.where(nblk > m_, jnp.where(other >= imp, 1.0, 0.0), jnp.where(other > imp, 1.0, 0.0))
    sel = jnp.where(rank < float(min(SEL_TOPN, n_sel)), 1.0, 0.0).T
    sel = sel.astype(MXU_DTYPE)

    kc_sel = 512

    def sel_chunk(ci, carry):
        k0 = pl.multiple_of(ci * kc_sel, kc_sel)
        pos = k0 + lax.broadcasted_iota(jnp.int32, (1, kc_sel), 1)
        blk_of_key = jnp.right_shift(k0 + lax.broadcasted_iota(jnp.int32, (LANES, kc_sel), 1), 6)
        expand = jnp.where(blk_of_key == lax.broadcasted_iota(jnp.int32, (LANES, kc_sel), 0), 1.0, 0.0)
        chosen = _mm(sel, expand)
        return online(carry, ks_ref[0, pl.ds(k0, kc_sel), :], vs_ref[0, pl.ds(k0, kc_sel), :], pos,
                      lambda d, p_: jnp.where(d >= 0, chosen, 0.0) > 0.5)

    o_sel = finish(lax.fori_loop(0, (q0 + Q_BLOCK + kc_sel - 1) // kc_sel, sel_chunk, init))

    w0 = pl.multiple_of(jnp.maximum(q0 - WINDOW, 0), Q_BLOCK)
    pos_a = w0 + lax.broadcasted_iota(jnp.int32, (1, WINDOW), 1)
    carry = online(init, kw_ref[0, pl.ds(w0, WINDOW), :], vw_ref[0, pl.ds(w0, WINDOW), :], pos_a,
                   lambda d, p_: (d < WINDOW) & (p_ < q0))
    q0a = pl.multiple_of(q0, Q_BLOCK)
    pos_b = q0 + lax.broadcasted_iota(jnp.int32, (1, Q_BLOCK), 1)
    carry = online(carry, kw_ref[0, pl.ds(q0a, Q_BLOCK), :], vw_ref[0, pl.ds(q0a, Q_BLOCK), :], pos_b,
                   lambda d, p_: d >= 0)
    o_win = finish(carry)

    def unstack(o):
        return jnp.concatenate([o[r * Q_BLOCK:(r + 1) * Q_BLOCK] for r in range(NSA_GROUP)], axis=1)

    sig = _sigmoid(gl_ref[0])
    sig_hi = sig.astype(MXU_DTYPE)
    sig_lo = (sig - sig_hi.astype(F32)).astype(MXU_DTYPE)
    out = None
    for i, branch in enumerate((o_cmp, o_sel, o_win)):
        gate = (jnp.dot(sig_hi, ge_ref[0, i], preferred_element_type=F32)
                + jnp.dot(sig_lo, ge_ref[0, i], preferred_element_type=F32))
        term = gate * unstack(branch)
        out = term if out is None else out + term
    o_ref[0] = (out * _silu(sg_ref[0])).astype(o_ref.dtype)


def _alibi_slope_rows():
    heads = NSA_KV * NSA_GROUP
    h = np.arange(1, heads + 1, dtype=np.float32)
    s = np.power(np.float32(2.0), -8.0 * h / heads).astype(np.float32).reshape(NSA_KV, NSA_GROUP)
    return jnp.asarray(np.repeat(s, Q_BLOCK, axis=1).reshape(NSA_KV, NSA_GROUP * Q_BLOCK, 1))


def _overlap_t(t, ncmp_rows):
    n_cmp = (t - CMP_LEN) // CMP_STRIDE + 1
    n_sel = t // SEL_LEN
    s_c = np.arange(n_cmp) * CMP_STRIDE
    s_s = np.arange(n_sel) * SEL_LEN
    ov = np.clip(np.minimum(s_c[:, None] + CMP_LEN, s_s[None, :] + SEL_LEN)
                 - np.maximum(s_c[:, None], s_s[None, :]), 0, None).astype(np.float32) / CMP_LEN
    out = np.zeros((LANES, ncmp_rows), np.float32)
    out[:n_sel, :n_cmp] = ov.T
    return jnp.asarray(out)


def _gate_expand():
    width = NSA_GROUP * HEAD_DIM
    e = np.zeros((NSA_KV, 3, LANES, width), np.float32)
    for g in range(NSA_KV):
        for r in range(NSA_GROUP):
            for i in range(3):
                e[g, i, (g * NSA_GROUP + r) * 3 + i, r * HEAD_DIM:(r + 1) * HEAD_DIM] = 1.0
    return jnp.asarray(e, dtype=MXU_DTYPE)


def _nsa(proj3, cols, k_cmp, v_cmp):
    b, t, _ = proj3.shape
    assert t // SEL_LEN <= LANES and t % 512 == 0 and Q_BLOCK == SEL_LEN
    gw = NSA_GROUP * HEAD_DIM
    ncmp_rows = k_cmp.shape[2]
    q_c, ks_c, kw_c, gl_c, sg_c = cols

    def kvspec(c, is_v):
        blk = c // HEAD_DIM + (NSA_KV if is_v else 0)
        return pl.BlockSpec((1, t, HEAD_DIM), lambda i, g, j: (i, 0, blk + g))

    cmpspec = pl.BlockSpec((1, 1, ncmp_rows, HEAD_DIM), lambda i, g, j: (i, g, 0, 0))
    return pl.pallas_call(
        _nsa_kernel,
        grid=(b, NSA_KV, t // Q_BLOCK),
        in_specs=[pl.BlockSpec((1, NSA_GROUP * Q_BLOCK, 1), lambda i, g, j: (g, 0, 0)),
                  pl.BlockSpec((1, Q_BLOCK, gw), lambda i, g, j: (i, j, q_c // gw + g)),
                  pl.BlockSpec((1, Q_BLOCK, LANES), lambda i, g, j: (i, j, gl_c // LANES)),
                  pl.BlockSpec((1, Q_BLOCK, gw), lambda i, g, j: (i, j, sg_c // gw + g)),
                  kvspec(ks_c, False), kvspec(ks_c, True), kvspec(kw_c, False), kvspec(kw_c, True),
                  cmpspec, cmpspec,
                  pl.BlockSpec((LANES, ncmp_rows), lambda i, g, j: (0, 0)),
                  pl.BlockSpec((1, 3, LANES, gw), lambda i, g, j: (g, 0, 0, 0))],
        out_specs=pl.BlockSpec((1, Q_BLOCK, gw), lambda i, g, j: (i, j, g)),
        out_shape=jax.ShapeDtypeStruct((b, t, NSA_KV * gw), MXU_DTYPE),
        compiler_params=_params("parallel", "parallel", "arbitrary"),
        name="nsa_attn",
    )(_alibi_slope_rows(), proj3, proj3, proj3, proj3, proj3, proj3, proj3, k_cmp, v_cmp,
      _overlap_t(t, ncmp_rows), _gate_expand())


def _out_ln_kernel(a0_ref, a1_ref, a2_ref, a3_ref, w0_ref, w1_ref, w2_ref, w3_ref,
                   x_ref, gate_ref, lng_ref, lnb_ref, o_ref, y_ref, *, alpha):
    j = pl.program_id(1)
    y_ref[j] = (jnp.dot(a0_ref[...], w0_ref[0], preferred_element_type=F32)
                + jnp.dot(a1_ref[...], w1_ref[0], preferred_element_type=F32)
                + jnp.dot(a2_ref[...], w2_ref[0], preferred_element_type=F32)
                + jnp.dot(a3_ref[...], w3_ref[0], preferred_element_type=F32))

    @pl.when(j == pl.num_programs(1) - 1)
    def _():
        nj, _, tn = y_ref.shape
        d = nj * tn
        pre = [alpha * x_ref[:, c * tn:(c + 1) * tn]
               + (1.0 + gate_ref[0, 0, :, c * tn:(c + 1) * tn]) * y_ref[c] for c in range(nj)]
        total = pre[0].sum(axis=-1, keepdims=True)
        for c in range(1, nj):
            total = total + pre[c].sum(axis=-1, keepdims=True)
        mu = total / d
        sq = jnp.square(pre[0] - mu).sum(axis=-1, keepdims=True)
        for c in range(1, nj):
            sq = sq + jnp.square(pre[c] - mu).sum(axis=-1, keepdims=True)
        inv = lax.rsqrt(sq / d + LN_EPS)
        for c in range(nj):
            cs = slice(c * tn, (c + 1) * tn)
            o_ref[:, cs] = (pre[c] - mu) * inv * lng_ref[0, :, cs] + lnb_ref[0, :, cs]


def _out_ln(y_rg, y_nsa, y_hg, w_out_b, x2, mod4, ln_g, ln_b, layer, alpha, t):
    m, d = x2.shape
    depth = w_out_b.shape[0]
    kb = y_rg.shape[1]
    assert y_nsa.shape[1] == 2 * kb and y_hg.shape[1] == kb and d == 4 * kb
    tm, tn = 256, 512
    aspec = lambda c: pl.BlockSpec((tm, kb), lambda i, j: (i, c))
    wspec = lambda r: pl.BlockSpec((1, kb, tn), lambda i, j: (layer, r, j))
    return pl.pallas_call(
        functools.partial(_out_ln_kernel, alpha=alpha),
        grid=(m // tm, d // tn),
        in_specs=[aspec(0), aspec(0), aspec(1), aspec(0), wspec(0), wspec(1), wspec(2), wspec(3),
                  pl.BlockSpec((tm, d), lambda i, j: (i, 0)),
                  pl.BlockSpec((1, 1, 1, d), lambda i, j: (layer, (i * tm) // t, 0, 2)),
                  pl.BlockSpec((1, 1, d), lambda i, j: (layer, 0, 0)),
                  pl.BlockSpec((1, 1, d), lambda i, j: (layer, 0, 0))],
        out_specs=pl.BlockSpec((tm, d), lambda i, j: (i, 0)),
        out_shape=jax.ShapeDtypeStruct((m, d), F32),
        scratch_shapes=[pltpu.VMEM((d // tn, tm, tn), F32)],
        compiler_params=_params("parallel", "arbitrary"),
        name="out_ln",
    )(y_rg, y_nsa, y_nsa, y_hg, w_out_b, w_out_b, w_out_b, w_out_b, x2, mod4,
      ln_g.reshape(depth, 1, d), ln_b.reshape(depth, 1, d))


def kernel(x, c, w_ada, b_ada, w_in, rg_conv_w, rg_conv_b, rg_w_a, rg_b_a, rg_w_x, rg_b_x, rg_lambda,
           nsa_pe_k, nsa_pe_v, nsa_cmp_w1_k, nsa_cmp_w2_k, nsa_cmp_w1_v, nsa_cmp_w2_v,
           hg_lower_bounds, hg_norm_g, w_out, ln_g, ln_b):
    b, t, d = x.shape
    depth = w_ada.shape[0]
    m = b * t
    d_rg, d_nsa, d_hg = d // 4, d // 2, d // 4
    kv_cols = 2 * NSA_KV * HEAD_DIM
    n_gl = 3 * NSA_KV * NSA_GROUP
    alpha = (2.0 * depth) ** 0.25
    assert d_nsa == NSA_KV * NSA_GROUP * HEAD_DIM and b <= SUBLANES

    gl_pad = 512
    c_q = 2 * d_rg
    c_kvc = c_q + d_nsa
    c_kvs = c_kvc + kv_cols
    c_kvw = c_kvs + kv_cols
    c_gl = c_kvw + kv_cols
    c_sg = c_gl + gl_pad
    c_hg = c_sg + d_nsa
    w_in_p = jnp.concatenate(
        [w_in[:, :, :c_gl], jnp.pad(w_in[:, :, c_gl:c_gl + n_gl], ((0, 0), (0, 0), (0, gl_pad - n_gl))),
         w_in[:, :, c_gl + n_gl:]], axis=2).astype(MXU_DTYPE)
    w_out_b = w_out.astype(MXU_DTYPE)
    w1_k, w1_v = nsa_cmp_w1_k.astype(MXU_DTYPE), nsa_cmp_w1_v.astype(MXU_DTYPE)

    c_pad = jnp.pad(c, ((0, SUBLANES - b), (0, 0)))
    mod4 = _ada(c_pad, w_ada, b_ada).reshape(depth, SUBLANES, 1, 3 * d)

    for layer in range(depth):
        u = _modulate(x, mod4, layer)
        proj3 = _in_proj(u.reshape(m, d), w_in_p, layer).reshape(b, t, -1)
        y_rg = _rglru(proj3, rg_conv_w, rg_conv_b, rg_w_a, rg_b_a, rg_w_x, rg_b_x, rg_lambda, layer, d_rg)
        kvc = proj3[:, :, c_kvc:c_kvc + kv_cols].reshape(b, t // CMP_STRIDE, CMP_STRIDE * kv_cols)
        k_cmp, v_cmp = _compress(kvc, nsa_pe_k, nsa_pe_v, w1_k, nsa_cmp_w2_k, w1_v, nsa_cmp_w2_v, layer)
        y_nsa = _nsa(proj3, (c_q, c_kvs, c_kvw, c_gl, c_sg), k_cmp, v_cmp)
        y_hg = _hgrn2(proj3, c_hg, hg_lower_bounds, hg_norm_g, layer, d_hg)
        x = _out_ln(y_rg.reshape(m, d_rg), y_nsa.reshape(m, d_nsa), y_hg.reshape(m, d_hg), w_out_b,
                    x.reshape(m, d), mod4, ln_g, ln_b, layer, alpha, t).reshape(b, t, d)
    return x
```

```python
import functools

import numpy as np
import jax
import jax.numpy as jnp
from jax import lax
from jax.experimental import pallas as pl
from jax.experimental.pallas import tpu as pltpu

F32 = jnp.float32
MXU_DTYPE = jnp.bfloat16

HEAD_DIM = 128
RG_CONV = 4
RG_C = 8.0
NSA_KV = 2
NSA_GROUP = 8
CMP_LEN = 32
CMP_STRIDE = 16
SEL_LEN = 64
SEL_TOPN = 16
WINDOW = 512
Q_BLOCK = 64
HG_CHUNK = 64
HG_SUB = 8
LN_EPS = 1e-5
RMS_EPS = 1e-6
NEG_INF = -1e30
FORCE_SCORE = 1e9
MASK_VAL = -(2.0 ** 100)
LOG2E = 1.4426950408889634
POS_LANE = 32

LANES = 128
SUBLANES = 8
VMEM_LIMIT = 56 * 1024 * 1024


def _mm(a, b):
    return jnp.dot(a.astype(MXU_DTYPE), b.astype(MXU_DTYPE), preferred_element_type=F32)


def _mm_nt(a, b):
    return lax.dot_general(a.astype(MXU_DTYPE), b.astype(MXU_DTYPE),
                           (((1,), (1,)), ((), ())), preferred_element_type=F32)


def _mm_tn(a, b):
    return lax.dot_general(a.astype(MXU_DTYPE), b.astype(MXU_DTYPE),
                           (((0,), (0,)), ((), ())), preferred_element_type=F32)


def _sigmoid(v):
    return jax.nn.sigmoid(v)


def _silu(v):
    return v * jax.nn.sigmoid(v)


def _params(*semantics):
    return pltpu.CompilerParams(dimension_semantics=semantics, vmem_limit_bytes=VMEM_LIMIT)


def _ada_kernel(c_ref, w_ref, b_ref, o_ref):
    o_ref[0] = _mm(c_ref[...], w_ref[0]) + b_ref[0]


def _ada(c_pad, w_ada, b_ada):
    depth, d, n3 = w_ada.shape
    rows = c_pad.shape[0]
    tn = 512
    return pl.pallas_call(
        _ada_kernel,
        grid=(depth, n3 // tn),
        in_specs=[pl.BlockSpec((rows, d), lambda l, j: (0, 0)),
                  pl.BlockSpec((1, d, tn), lambda l, j: (l, 0, j)),
                  pl.BlockSpec((1, 1, tn), lambda l, j: (l, 0, j))],
        out_specs=pl.BlockSpec((1, rows, tn), lambda l, j: (l, 0, j)),
        out_shape=jax.ShapeDtypeStruct((depth, rows, n3), F32),
        compiler_params=_params("parallel", "parallel"),
        name="ada",
    )(c_pad, w_ada, b_ada.reshape(depth, 1, n3))


def _modulate_kernel(x_ref, shift_ref, scale_ref, o_ref):
    o_ref[0] = (x_ref[0] * (1.0 + scale_ref[0, 0]) + shift_ref[0, 0]).astype(o_ref.dtype)


def _modulate(x, mod4, layer):
    b, t, d = x.shape
    tt = 256
    return pl.pallas_call(
        _modulate_kernel,
        grid=(b, t // tt),
        in_specs=[pl.BlockSpec((1, tt, d), lambda i, j: (i, j, 0)),
                  pl.BlockSpec((1, 1, 1, d), lambda i, j: (layer, i, 0, 0)),
                  pl.BlockSpec((1, 1, 1, d), lambda i, j: (layer, i, 0, 1))],
        out_specs=pl.BlockSpec((1, tt, d), lambda i, j: (i, j, 0)),
        out_shape=jax.ShapeDtypeStruct((b, t, d), MXU_DTYPE),
        compiler_params=_params("parallel", "parallel"),
        name="modulate",
    )(x, mod4, mod4)


def _in_proj_kernel(a_ref, w_ref, o_ref):
    o_ref[...] = jnp.dot(a_ref[...], w_ref[0], preferred_element_type=F32)


def _in_proj(u, w_in_p, layer):
    m, d = u.shape
    n = w_in_p.shape[2]
    tm, tn = 1024, 512
    return pl.pallas_call(
        _in_proj_kernel,
        grid=(m // tm, n // tn),
        in_specs=[pl.BlockSpec((tm, d), lambda i, j: (i, 0)),
                  pl.BlockSpec((1, d, tn), lambda i, j: (layer, 0, j))],
        out_specs=pl.BlockSpec((tm, tn), lambda i, j: (i, j)),
        out_shape=jax.ShapeDtypeStruct((m, n), F32),
        compiler_params=_params("parallel", "parallel"),
        name="in_proj",
    )(u, w_in_p)


def _rglru_kernel(x_ref, g_ref, cw_ref, cb_ref, wa_ref, ba_ref, wx_ref, bx_ref, lam_ref, o_ref,
                  xp_ref, a_ref, b_ref, h_ref):
    t, cb = x_ref.shape[1], x_ref.shape[2]
    pad = SUBLANES
    xp_ref[0:pad, :] = jnp.zeros((pad, cb), F32)
    xp_ref[pad:, :] = x_ref[0]
    xc = cb_ref[0]
    for j in range(RG_CONV):
        xc = xc + xp_ref[pl.ds(pad - (RG_CONV - 1) + j, t), :] * cw_ref[0, j:j + 1, :]

    row = lax.broadcasted_iota(jnp.int32, (t, HEAD_DIM), 0)
    sub = jnp.bitwise_and(row, SUBLANES - 1)
    for n in range(cb // HEAD_DIM):
        sl = slice(n * HEAD_DIM, (n + 1) * HEAD_DIM)
        xb = xc[:, sl]
        r = _sigmoid(_mm(xb, wa_ref[0, n]) + ba_ref[0, :, sl])
        i = _sigmoid(_mm(xb, wx_ref[0, n]) + bx_ref[0, :, sl])
        neg_lam = -lam_ref[0, :, sl]
        softplus = jnp.maximum(neg_lam, 0.0) + jnp.log1p(jnp.exp(-jnp.abs(neg_lam)))
        log_a = (-RG_C * softplus) * r
        a = jnp.exp(log_a)
        mult = jnp.sqrt(-jnp.tanh(log_a) * (a * a + 1.0))
        mult = jnp.where(row == 0, 1.0, mult)
        bx = mult * (i * xb)
        s = 1
        while s < SUBLANES:
            a_sh = pltpu.roll(a, s, axis=0)
            b_sh = pltpu.roll(bx, s, axis=0)
            inside = sub >= s
            bx = jnp.where(inside, a * b_sh + bx, bx)
            a = jnp.where(inside, a * a_sh, a)
            s *= 2
        a_ref[:, sl] = a
        b_ref[:, sl] = bx

    def carry_rows(v, h):
        r0 = pl.multiple_of(v * SUBLANES, SUBLANES)
        hh = a_ref[pl.ds(r0, SUBLANES), :] * h + b_ref[pl.ds(r0, SUBLANES), :]
        h_ref[pl.ds(r0, SUBLANES), :] = hh
        return jnp.broadcast_to(hh[SUBLANES - 1:SUBLANES, :], hh.shape)

    lax.fori_loop(0, t // SUBLANES, carry_rows, jnp.zeros((SUBLANES, cb), F32))
    o_ref[0] = (h_ref[...] * _silu(g_ref[0])).astype(o_ref.dtype)


def _rglru(proj3, conv_w, conv_b, w_a, b_a, w_x, b_x, lam, layer, d_rg):
    b, t, _ = proj3.shape
    depth = conv_w.shape[0]
    cb = 256
    nblk = cb // HEAD_DIM
    ncb = d_rg // cb
    vec = lambda v: v.reshape(depth, 1, d_rg)
    vspec = pl.BlockSpec((1, 1, cb), lambda i, j: (layer, 0, j))
    wspec = pl.BlockSpec((1, nblk, HEAD_DIM, HEAD_DIM), lambda i, j: (layer, j, 0, 0))
    return pl.pallas_call(
        _rglru_kernel,
        grid=(b, ncb),
        in_specs=[pl.BlockSpec((1, t, cb), lambda i, j: (i, 0, j)),
                  pl.BlockSpec((1, t, cb), lambda i, j: (i, 0, ncb + j)),
                  pl.BlockSpec((1, RG_CONV, cb), lambda i, j: (layer, 0, j)),
                  vspec, wspec, vspec, wspec, vspec, vspec],
        out_specs=pl.BlockSpec((1, t, cb), lambda i, j: (i, 0, j)),
        out_shape=jax.ShapeDtypeStruct((b, t, d_rg), MXU_DTYPE),
        scratch_shapes=[pltpu.VMEM((t + SUBLANES, cb), F32), pltpu.VMEM((t, cb), F32),
                        pltpu.VMEM((t, cb), F32), pltpu.VMEM((t, cb), F32)],
        compiler_params=_params("parallel", "parallel"),
        name="rglru",
    )(proj3, proj3, conv_w, vec(conv_b), w_a, vec(b_a), w_x, vec(b_x), vec(lam))


def _hgrn2_kernel(q_ref, f_ref, v_ref, g_ref, lbr_ref, ng_ref, o_ref,
                  qs_ref, kk_ref, b_ref, oo_ref, *, layer):
    t = q_ref.shape[1]
    ch, sb = HG_CHUNK, HG_SUB
    nsb = ch // sb
    z = f_ref[0]
    ez = jnp.exp(-jnp.abs(z))
    log_sig = jnp.minimum(z, 0.0) - jnp.log(1.0 + ez)
    inv = 1.0 / (1.0 + ez)
    sig_neg = jnp.where(z >= 0.0, ez * inv, inv)
    if layer == 0:
        log_f = log_sig
        kk = sig_neg
    else:
        raw = lbr_ref[...]
        e = jnp.exp(raw - jnp.max(raw, axis=0, keepdims=True))
        p = e / jnp.sum(e, axis=0, keepdims=True)
        lb = p[1:2]
        for j in range(2, layer + 1):
            lb = lb + p[j:j + 1]
        log_lb = jnp.log(lb)
        other = jnp.log1p(-lb) + log_sig
        log_f = jnp.maximum(log_lb, other) + jnp.log(1.0 + jnp.exp(-jnp.abs(log_lb - other)))
        kk = (1.0 - lb) * sig_neg
    row = lax.broadcasted_iota(jnp.int32, (t, HEAD_DIM), 0)
    in_chunk = jnp.bitwise_and(row, ch - 1)
    bcum = log_f
    s = 1
    while s < ch:
        bcum = bcum + jnp.where(in_chunk >= s, pltpu.roll(bcum, s, axis=0), 0.0)
        s *= 2
    b_ref[...] = bcum
    kk_ref[...] = kk
    qs_ref[...] = _silu(q_ref[0])

    ones = jnp.ones((HEAD_DIM, ch), MXU_DTYPE)
    lane = lax.broadcasted_iota(jnp.int32, (sb, ch), 1)
    subrow = lax.broadcasted_iota(jnp.int32, (sb, HEAD_DIM), 0)

    def chunk(c, st):
        r0 = pl.multiple_of(c * ch, ch)
        bq = b_ref[pl.ds(r0, ch), :]
        qc = qs_ref[pl.ds(r0, ch), :]
        kc = kk_ref[pl.ds(r0, ch), :]
        vc = v_ref[0, pl.ds(r0, ch), :]
        blast = bq[ch - 1:ch, :]
        o = _mm_nt(qc * jnp.exp(bq), st)
        a_rows = []
        diag = []
        for blk in range(nsb):
            lo = blk * sb
            b_i, q_i, k_i = bq[lo:lo + sb], qc[lo:lo + sb], kc[lo:lo + sb]
            for s_ in range(sb):
                dec = jnp.where(subrow >= s_, jnp.exp(b_i - b_i[s_:s_ + 1]), 0.0)
                diag.append(q_i * (k_i[s_:s_ + 1] * dec))
            if blk == 0:
                a_rows.append(jnp.zeros((sb, ch), F32))
            else:
                m_i = bq[lo - 1:lo]
                qd = q_i * jnp.exp(b_i - m_i)
                kd = kc[0:lo] * jnp.exp(m_i - bq[0:lo])
                kd = jnp.concatenate([kd, jnp.zeros((ch - lo, HEAD_DIM), F32)], axis=0)
                a_rows.append(_mm_nt(qd, kd))
        dsum = _mm(jnp.concatenate(diag, axis=0), ones)
        for blk in range(nsb):
            acc = a_rows[blk]
            for s_ in range(sb):
                idx = blk * sb + s_
                acc = acc + jnp.where(lane == idx, dsum[idx * sb:(idx + 1) * sb], 0.0)
            a_rows[blk] = acc
        a_mat = jnp.concatenate(a_rows, axis=0)
        oo_ref[pl.ds(r0, ch), :] = o + _mm(a_mat, vc)
        kdec = kc * jnp.exp(blast - bq)
        return st * jnp.exp(blast) + _mm_tn(vc, kdec)

    lax.fori_loop(0, t // ch, chunk, jnp.zeros((HEAD_DIM, HEAD_DIM), F32), unroll=8)
    o = oo_ref[...]
    o = o * lax.rsqrt(jnp.mean(o * o, axis=-1, keepdims=True) + RMS_EPS) * ng_ref[0]
    o_ref[0] = (o * _silu(g_ref[0])).astype(o_ref.dtype)


def _hgrn2(proj3, col0, lower_bounds, norm_g, layer, d_hg):
    b, t, _ = proj3.shape
    depth = lower_bounds.shape[0]
    heads = d_hg // HEAD_DIM
    c0 = col0 // HEAD_DIM

    def colspec(k):
        return pl.BlockSpec((1, t, HEAD_DIM), lambda i, h: (i, 0, c0 + k * heads + h))

    return pl.pallas_call(
        functools.partial(_hgrn2_kernel, layer=layer),
        grid=(b, heads),
        in_specs=[colspec(0), colspec(1), colspec(2), colspec(3),
                  pl.BlockSpec((depth, HEAD_DIM), lambda i, h: (0, h)),
                  pl.BlockSpec((1, 1, HEAD_DIM), lambda i, h: (layer, 0, 0))],
        out_specs=pl.BlockSpec((1, t, HEAD_DIM), lambda i, h: (i, 0, h)),
        out_shape=jax.ShapeDtypeStruct((b, t, d_hg), MXU_DTYPE),
        scratch_shapes=[pltpu.VMEM((t, HEAD_DIM), F32)] * 4,
        compiler_params=_params("parallel", "parallel"),
        name="hgrn2",
    )(proj3, proj3, proj3, proj3, lower_bounds, norm_g.reshape(depth, 1, HEAD_DIM))


def _compress_kernel(x_ref, pek_ref, pev_ref, w1k_ref, w2k_ref, w1v_ref, w2v_ref, ko_ref, vo_ref):
    nrow = x_ref.shape[1]
    width = 2 * NSA_KV * HEAD_DIM
    row = lax.broadcasted_iota(jnp.int32, (nrow, HEAD_DIM), 0)
    branches = ((pek_ref, w1k_ref, w2k_ref, ko_ref), (pev_ref, w1v_ref, w2v_ref, vo_ref))
    for kv, (pe_ref, w1_ref, w2_ref, out_ref) in enumerate(branches):
        for g in range(NSA_KV):
            c0 = (kv * NSA_KV + g) * HEAD_DIM
            lo = jnp.concatenate(
                [x_ref[0, :, i * width + c0:i * width + c0 + HEAD_DIM] for i in range(CMP_STRIDE)], axis=1)
            hi = pltpu.roll(lo, nrow - 1, axis=0)
            blk = jnp.concatenate([lo, hi], axis=1) + pe_ref[0]
            hid = _silu(_mm(blk, w1_ref[0]))
            out = _mm(hid, w2_ref[0])
            out_ref[0, g] = jnp.where(row < nrow - 1, out, 0.0)


def _compress(kvc, pe_k, pe_v, w1_k, w2_k, w1_v, w2_v, layer):
    b, nrow, wide = kvc.shape
    depth = pe_k.shape[0]
    hidden = w1_k.shape[2]
    flat = CMP_LEN * HEAD_DIM
    pespec = pl.BlockSpec((1, 1, flat), lambda i: (layer, 0, 0))
    w1spec = pl.BlockSpec((1, flat, hidden), lambda i: (layer, 0, 0))
    w2spec = pl.BlockSpec((1, hidden, HEAD_DIM), lambda i: (layer, 0, 0))
    ospec = pl.BlockSpec((1, NSA_KV, nrow, HEAD_DIM), lambda i: (i, 0, 0, 0))
    oshape = jax.ShapeDtypeStruct((b, NSA_KV, nrow, HEAD_DIM), F32)
    return pl.pallas_call(
        _compress_kernel,
        grid=(b,),
        in_specs=[pl.BlockSpec((1, nrow, wide), lambda i: (i, 0, 0)),
                  pespec, pespec, w1spec, w2spec, w1spec, w2spec],
        out_specs=[ospec, ospec],
        out_shape=[oshape, oshape],
        compiler_params=_params("parallel"),
        name="nsa_compress",
    )(kvc, pe_k.reshape(depth, 1, flat), pe_v.reshape(depth, 1, flat), w1_k, w2_k, w1_v, w2_v)


def _nsa_kernel(slt_ref, kaux_ref, kauxc_ref, q_ref, gl_ref, sg_ref, ks_ref, vs_ref, kw_ref, vw_ref,
                kc_ref, vc_ref, ov_ref, ge_ref, o_ref,
                ksx_ref, kwx_ref, vsb_ref, vwb_ref, kcx_ref, vcb_ref):
    qb = pl.program_id(2)
    q0 = qb * Q_BLOCK
    q0a = pl.multiple_of(q0, Q_BLOCK)
    nq = NSA_GROUP * Q_BLOCK
    n_sel = ks_ref.shape[1] // SEL_LEN
    kchunk = 512

    @pl.when(qb == 0)
    def _():
        ksx_ref[:, 0:HEAD_DIM] = ks_ref[0].astype(MXU_DTYPE)
        ksx_ref[:, HEAD_DIM:] = kaux_ref[...]
        kwx_ref[:, 0:HEAD_DIM] = kw_ref[0].astype(MXU_DTYPE)
        kwx_ref[:, HEAD_DIM:] = kaux_ref[...]
        kcx_ref[:, 0:HEAD_DIM] = kc_ref[0, 0].astype(MXU_DTYPE)
        kcx_ref[:, HEAD_DIM:] = kauxc_ref[...]
        vsb_ref[...] = vs_ref[0].astype(MXU_DTYPE)
        vwb_ref[...] = vw_ref[0].astype(MXU_DTYPE)
        vcb_ref[...] = vc_ref[0, 0].astype(MXU_DTYPE)

    def stack(v):
        return jnp.concatenate([v] * NSA_GROUP, axis=0)

    q = q_ref[0] * (HEAD_DIM ** -0.5 * LOG2E)
    q_main = jnp.concatenate([q[:, r * HEAD_DIM:(r + 1) * HEAD_DIM] for r in range(NSA_GROUP)],
                             axis=0).astype(MXU_DTYPE)
    slope_lanes = slt_ref[0]

    def with_aux(block_bias):
        return jnp.concatenate([q_main, (stack(block_bias) + slope_lanes).astype(MXU_DTYPE)], axis=1)

    def online(carry, s, v):
        m, l, acc = carry
        m_new = jnp.maximum(m, jnp.max(s, axis=1, keepdims=True))
        alpha = jnp.exp2(m - m_new)
        p = jnp.exp2(s - m_new)
        l = alpha * l + jnp.sum(p, axis=1, keepdims=True)
        acc = alpha * acc + jnp.dot(p.astype(MXU_DTYPE), v, preferred_element_type=F32)
        return m_new, l, acc

    def finish(carry):
        _, l, acc = carry
        return acc * jnp.where(l > 0.0, 1.0 / l, 0.0)

    init = (jnp.full((nq, 1), NEG_INF, F32), jnp.zeros((nq, 1), F32), jnp.zeros((nq, HEAD_DIM), F32))
    t_loc = lax.broadcasted_iota(jnp.int32, (Q_BLOCK, LANES), 0)
    lane = lax.broadcasted_iota(jnp.int32, (Q_BLOCK, LANES), 1)
    q_plain = with_aux(jnp.zeros((Q_BLOCK, LANES), F32))

    ncmp = kcx_ref.shape[0]
    pos_c = (CMP_LEN - 1) + CMP_STRIDE * lax.broadcasted_iota(jnp.int32, (Q_BLOCK, ncmp), 1)
    ok_c = stack(jnp.where(q0 + lax.broadcasted_iota(jnp.int32, (Q_BLOCK, ncmp), 0) >= pos_c, 1.0, 0.0)) > 0.5
    s_c = jnp.where(ok_c, _mm_nt(q_plain, kcx_ref[...]), MASK_VAL)
    p_c = jnp.exp2(s_c - jnp.maximum(jnp.max(s_c, axis=1, keepdims=True), NEG_INF))
    l_c = jnp.sum(p_c, axis=1, keepdims=True)
    p_c = p_c * jnp.where(l_c > 0.0, 1.0 / l_c, 0.0)
    o_cmp = jnp.dot(p_c.astype(MXU_DTYPE), vcb_ref[...], preferred_element_type=F32)

    p_sum = p_c[0:Q_BLOCK]
    for r in range(1, NSA_GROUP):
        p_sum = p_sum + p_c[r * Q_BLOCK:(r + 1) * Q_BLOCK]
    imp = lax.dot_general(ov_ref[...], p_sum, (((1,), (1,)), ((), ())),
                          precision=lax.Precision.HIGHEST, preferred_element_type=F32)[0:n_sel]
    nblk = lax.broadcasted_iota(jnp.int32, imp.shape, 0)
    forced = (nblk == 0) | (nblk == qb) | (nblk == qb - 1)
    imp = jnp.where(nblk > qb, NEG_INF, jnp.where(forced, FORCE_SCORE, imp))
    rank = jnp.zeros(imp.shape, F32)
    for m_ in range(n_sel):
        other = imp[m_:m_ + 1, :]
        rank = rank + jnp.where(nblk > m_, jnp.where(other >= imp, 1.0, 0.0), jnp.where(other > imp, 1.0, 0.0))
    sel_bias = jnp.where(rank < float(min(SEL_TOPN, n_sel)), jnp.where(nblk < qb, 0.0, MASK_VAL), MASK_VAL)
    sel_bias = jnp.concatenate([sel_bias, jnp.zeros((LANES - n_sel, Q_BLOCK), F32)], axis=0).T

    q_sel = with_aux(sel_bias)

    def sel_chunk(ci, carry):
        k0 = pl.multiple_of(ci * kchunk, kchunk)
        return online(carry, _mm_nt(q_sel, ksx_ref[pl.ds(k0, kchunk), :]), vsb_ref[pl.ds(k0, kchunk), :])

    carry = lax.fori_loop(0, (q0 + kchunk - 1) // kchunk, sel_chunk, init)
    tri_low = stack(jnp.where(lax.broadcasted_iota(jnp.int32, (Q_BLOCK, Q_BLOCK), 1)
                              <= lax.broadcasted_iota(jnp.int32, (Q_BLOCK, Q_BLOCK), 0), 1.0, 0.0)) > 0.5
    s_own = jnp.where(tri_low, _mm_nt(q_plain, ksx_ref[pl.ds(q0a, Q_BLOCK), :]), MASK_VAL)
    o_sel = finish(online(carry, s_own, vsb_ref[pl.ds(q0a, Q_BLOCK), :]))

    n_full = WINDOW // SEL_LEN - 1
    win_bias = jnp.where(lane < n_sel,
                         jnp.where(lane >= qb - n_full, jnp.where(lane < qb, 0.0, MASK_VAL), MASK_VAL), 0.0)
    w0 = pl.multiple_of(jnp.maximum(q0 - WINDOW, 0), Q_BLOCK)
    carry = online(init, _mm_nt(with_aux(win_bias), kwx_ref[pl.ds(w0, WINDOW), :]), vwb_ref[pl.ds(w0, WINDOW), :])
    e0 = pl.multiple_of(jnp.maximum(q0 - WINDOW, 0), Q_BLOCK)
    k_edge = jnp.concatenate([kwx_ref[pl.ds(e0, Q_BLOCK), :], kwx_ref[pl.ds(q0a, Q_BLOCK), :]], axis=0)
    v_edge = jnp.concatenate([vwb_ref[pl.ds(e0, Q_BLOCK), :], vwb_ref[pl.ds(q0a, Q_BLOCK), :]], axis=0)
    has_far = jnp.where(q0 >= WINDOW, 1.0, 0.0)
    ok_edge = jnp.where(lane < Q_BLOCK,
                        jnp.where(lane > t_loc, has_far, 0.0),
                        jnp.where(lane - Q_BLOCK <= t_loc, 1.0, 0.0))
    s_edge = jnp.where(stack(ok_edge) > 0.5, _mm_nt(q_plain, k_edge), MASK_VAL)
    o_win = finish(online(carry, s_edge, v_edge))

    def unstack(o):
        return jnp.concatenate([o[r * Q_BLOCK:(r + 1) * Q_BLOCK] for r in range(NSA_GROUP)], axis=1)

    sig = _sigmoid(gl_ref[0])
    sig_hi = sig.astype(MXU_DTYPE)
    sig_lo = (sig - sig_hi.astype(F32)).astype(MXU_DTYPE)
    out = None
    for i, branch in enumerate((o_cmp, o_sel, o_win)):
        gate = (jnp.dot(sig_hi, ge_ref[0, i], preferred_element_type=F32)
                + jnp.dot(sig_lo, ge_ref[0, i], preferred_element_type=F32))
        term = gate * unstack(branch)
        out = term if out is None else out + term
    o_ref[0] = (out * _silu(sg_ref[0])).astype(o_ref.dtype)


def _split3(v):
    bf = jnp.bfloat16
    hi = v.astype(bf).astype(np.float64)
    mid = (v - hi).astype(bf).astype(np.float64)
    lo = (v - hi - mid).astype(bf).astype(np.float64)
    return hi, mid, lo


def _slope_lanes():
    heads = NSA_KV * NSA_GROUP
    h = np.arange(1, heads + 1, dtype=np.float32)
    s = np.power(np.float32(2.0), -8.0 * h / heads).astype(np.float32).astype(np.float64) * LOG2E
    parts = _split3(s)
    out = np.zeros((heads, LANES), np.float32)
    for rep in range(2):
        for i, part in enumerate(parts):
            out[:, POS_LANE + 3 * rep + i] = part
    out = np.repeat(out.reshape(NSA_KV, NSA_GROUP, 1, LANES), Q_BLOCK, axis=2)
    return jnp.asarray(out.reshape(NSA_KV, NSA_GROUP * Q_BLOCK, LANES))


def _key_aux(pos, blocks):
    out = np.zeros((pos.shape[0], LANES), np.float32)
    if blocks:
        out[np.arange(pos.shape[0]), pos // SEL_LEN] = 1.0
    out[:, POS_LANE:POS_LANE + 3] = (SEL_LEN * (pos // SEL_LEN))[:, None]
    out[:, POS_LANE + 3:POS_LANE + 6] = (pos % SEL_LEN)[:, None]
    return jnp.asarray(out, dtype=MXU_DTYPE)


def _overlap_t(t, ncmp_rows):
    n_cmp = (t - CMP_LEN) // CMP_STRIDE + 1
    n_sel = t // SEL_LEN
    s_c = np.arange(n_cmp) * CMP_STRIDE
    s_s = np.arange(n_sel) * SEL_LEN
    ov = np.clip(np.minimum(s_c[:, None] + CMP_LEN, s_s[None, :] + SEL_LEN)
                 - np.maximum(s_c[:, None], s_s[None, :]), 0, None).astype(np.float32) / CMP_LEN
    out = np.zeros((LANES, ncmp_rows), np.float32)
    out[:n_sel, :n_cmp] = ov.T
    return jnp.asarray(out)


def _gate_expand():
    width = NSA_GROUP * HEAD_DIM
    e = np.zeros((NSA_KV, 3, LANES, width), np.float32)
    for g in range(NSA_KV):
        for r in range(NSA_GROUP):
            for i in range(3):
                e[g, i, (g * NSA_GROUP + r) * 3 + i, r * HEAD_DIM:(r + 1) * HEAD_DIM] = 1.0
    return jnp.asarray(e, dtype=MXU_DTYPE)


def _nsa(proj3, cols, k_cmp, v_cmp):
    b, t, _ = proj3.shape
    assert t // SEL_LEN <= LANES and t % 512 == 0 and Q_BLOCK == SEL_LEN
    gw = NSA_GROUP * HEAD_DIM
    ncmp_rows = k_cmp.shape[2]
    q_c, ks_c, kw_c, gl_c, sg_c = cols

    def kvspec(c, is_v):
        blk = c // HEAD_DIM + (NSA_KV if is_v else 0)
        return pl.BlockSpec((1, t, HEAD_DIM), lambda i, g, j: (i, 0, blk + g))

    cmpspec = pl.BlockSpec((1, 1, ncmp_rows, HEAD_DIM), lambda i, g, j: (i, g, 0, 0))
    cmp_pos = (CMP_LEN - 1) + CMP_STRIDE * np.arange(ncmp_rows)
    return pl.pallas_call(
        _nsa_kernel,
        grid=(b, NSA_KV, t // Q_BLOCK),
        in_specs=[pl.BlockSpec((1, NSA_GROUP * Q_BLOCK, LANES), lambda i, g, j: (g, 0, 0)),
                  pl.BlockSpec((t, LANES), lambda i, g, j: (0, 0)),
                  pl.BlockSpec((ncmp_rows, LANES), lambda i, g, j: (0, 0)),
                  pl.BlockSpec((1, Q_BLOCK, gw), lambda i, g, j: (i, j, q_c // gw + g)),
                  pl.BlockSpec((1, Q_BLOCK, LANES), lambda i, g, j: (i, j, gl_c // LANES)),
                  pl.BlockSpec((1, Q_BLOCK, gw), lambda i, g, j: (i, j, sg_c // gw + g)),
                  kvspec(ks_c, False), kvspec(ks_c, True), kvspec(kw_c, False), kvspec(kw_c, True),
                  cmpspec, cmpspec,
                  pl.BlockSpec((LANES, ncmp_rows), lambda i, g, j: (0, 0)),
                  pl.BlockSpec((1, 3, LANES, gw), lambda i, g, j: (g, 0, 0, 0))],
        out_specs=pl.BlockSpec((1, Q_BLOCK, gw), lambda i, g, j: (i, j, g)),
        out_shape=jax.ShapeDtypeStruct((b, t, NSA_KV * gw), MXU_DTYPE),
        scratch_shapes=[pltpu.VMEM((t, 2 * HEAD_DIM), MXU_DTYPE), pltpu.VMEM((t, 2 * HEAD_DIM), MXU_DTYPE),
                        pltpu.VMEM((t, HEAD_DIM), MXU_DTYPE), pltpu.VMEM((t, HEAD_DIM), MXU_DTYPE),
                        pltpu.VMEM((ncmp_rows, 2 * HEAD_DIM), MXU_DTYPE), pltpu.VMEM((ncmp_rows, HEAD_DIM), MXU_DTYPE)],
        compiler_params=_params("parallel", "parallel", "arbitrary"),
        name="nsa_attn",
    )(_slope_lanes(), _key_aux(np.arange(t), True), _key_aux(cmp_pos, False),
      proj3, proj3, proj3, proj3, proj3, proj3, proj3, k_cmp, v_cmp, _overlap_t(t, ncmp_rows), _gate_expand())


def _out_ln_kernel(a0_ref, a1_ref, a2_ref, a3_ref, w0_ref, w1_ref, w2_ref, w3_ref,
                   x_ref, gate_ref, lng_ref, lnb_ref, o_ref, pre_ref, mu_ref, inv_ref, *, alpha):
    j = pl.program_id(1)
    nj, _, tn = pre_ref.shape

    @pl.when(j < nj)
    def _():
        y = (jnp.dot(a0_ref[...], w0_ref[0], preferred_element_type=F32)
             + jnp.dot(a1_ref[...], w1_ref[0], preferred_element_type=F32)
             + jnp.dot(a2_ref[...], w2_ref[0], preferred_element_type=F32)
             + jnp.dot(a3_ref[...], w3_ref[0], preferred_element_type=F32))
        pre_ref[j] = alpha * x_ref[...] + (1.0 + gate_ref[0, 0]) * y

    @pl.when(j == nj - 1)
    def _():
        d = nj * tn
        total = pre_ref[0].sum(axis=-1, keepdims=True)
        for c in range(1, nj):
            total = total + pre_ref[c].sum(axis=-1, keepdims=True)
        mu = total / d
        sq = jnp.square(pre_ref[0] - mu).sum(axis=-1, keepdims=True)
        for c in range(1, nj):
            sq = sq + jnp.square(pre_ref[c] - mu).sum(axis=-1, keepdims=True)
        mu_ref[...] = mu
        inv_ref[...] = lax.rsqrt(sq / d + LN_EPS)

    @pl.when(j >= nj)
    def _():
        o_ref[...] = (pre_ref[j - nj] - mu_ref[...]) * inv_ref[...] * lng_ref[0] + lnb_ref[0]


def _out_ln(y_rg, y_nsa, y_hg, w_out_b, x2, mod4, ln_g, ln_b, layer, alpha, t):
    m, d = x2.shape
    depth = w_out_b.shape[0]
    kb = y_rg.shape[1]
    assert y_nsa.shape[1] == 2 * kb and y_hg.shape[1] == kb and d == 4 * kb
    tm, tn = 512, 512
    nj = d // tn
    assert t % tm == 0
    mm_col = lambda j: jnp.minimum(j, nj - 1)
    ln_col = lambda j: jnp.maximum(j - nj, 0)
    aspec = lambda c: pl.BlockSpec((tm, kb), lambda i, j: (i, c))
    wspec = lambda r: pl.BlockSpec((1, kb, tn), lambda i, j: (layer, r, mm_col(j)))
    vspec = pl.BlockSpec((1, 1, tn), lambda i, j: (layer, 0, ln_col(j)))
    return pl.pallas_call(
        functools.partial(_out_ln_kernel, alpha=alpha),
        grid=(m // tm, 2 * nj),
        in_specs=[aspec(0), aspec(0), aspec(1), aspec(0), wspec(0), wspec(1), wspec(2), wspec(3),
                  pl.BlockSpec((tm, tn), lambda i, j: (i, mm_col(j))),
                  pl.BlockSpec((1, 1, 1, tn), lambda i, j: (layer, (i * tm) // t, 0, 2 * nj + mm_col(j))),
                  vspec, vspec],
        out_specs=pl.BlockSpec((tm, tn), lambda i, j: (i, ln_col(j))),
        out_shape=jax.ShapeDtypeStruct((m, d), F32),
        scratch_shapes=[pltpu.VMEM((nj, tm, tn), F32), pltpu.VMEM((tm, 1), F32), pltpu.VMEM((tm, 1), F32)],
        compiler_params=_params("parallel", "arbitrary"),
        name="out_ln",
    )(y_rg, y_nsa, y_nsa, y_hg, w_out_b, w_out_b, w_out_b, w_out_b, x2, mod4,
      ln_g.reshape(depth, 1, d), ln_b.reshape(depth, 1, d))


def kernel(x, c, w_ada, b_ada, w_in, rg_conv_w, rg_conv_b, rg_w_a, rg_b_a, rg_w_x, rg_b_x, rg_lambda,
           nsa_pe_k, nsa_pe_v, nsa_cmp_w1_k, nsa_cmp_w2_k, nsa_cmp_w1_v, nsa_cmp_w2_v,
           hg_lower_bounds, hg_norm_g, w_out, ln_g, ln_b):
    b, t, d = x.shape
    depth = w_ada.shape[0]
    m = b * t
    d_rg, d_nsa, d_hg = d // 4, d // 2, d // 4
    kv_cols = 2 * NSA_KV * HEAD_DIM
    n_gl = 3 * NSA_KV * NSA_GROUP
    alpha = (2.0 * depth) ** 0.25
    assert d_nsa == NSA_KV * NSA_GROUP * HEAD_DIM and b <= SUBLANES

    gl_pad = 512
    c_q = 2 * d_rg
    c_kvc = c_q + d_nsa
    c_kvs = c_kvc + kv_cols
    c_kvw = c_kvs + kv_cols
    c_gl = c_kvw + kv_cols
    c_sg = c_gl + gl_pad
    c_hg = c_sg + d_nsa
    w_in_p = jnp.concatenate(
        [w_in[:, :, :c_gl], jnp.pad(w_in[:, :, c_gl:c_gl + n_gl], ((0, 0), (0, 0), (0, gl_pad - n_gl))),
         w_in[:, :, c_gl + n_gl:]], axis=2).astype(MXU_DTYPE)
    w_out_b = w_out.astype(MXU_DTYPE)
    w1_k, w1_v = nsa_cmp_w1_k.astype(MXU_DTYPE), nsa_cmp_w1_v.astype(MXU_DTYPE)

    c_pad = jnp.pad(c, ((0, SUBLANES - b), (0, 0)))
    mod4 = _ada(c_pad, w_ada, b_ada).reshape(depth, SUBLANES, 1, 3 * d)

    for layer in range(depth):
        u = _modulate(x, mod4, layer)
        proj3 = _in_proj(u.reshape(m, d), w_in_p, layer).reshape(b, t, -1)
        y_rg = _rglru(proj3, rg_conv_w, rg_conv_b, rg_w_a, rg_b_a, rg_w_x, rg_b_x, rg_lambda, layer, d_rg)
        kvc = proj3[:, :, c_kvc:c_kvc + kv_cols].reshape(b, t // CMP_STRIDE, CMP_STRIDE * kv_cols)
        k_cmp, v_cmp = _compress(kvc, nsa_pe_k, nsa_pe_v, w1_k, nsa_cmp_w2_k, w1_v, nsa_cmp_w2_v, layer)
        y_nsa = _nsa(proj3, (c_q, c_kvs, c_kvw, c_gl, c_sg), k_cmp, v_cmp)
        y_hg = _hgrn2(proj3, c_hg, hg_lower_bounds, hg_norm_g, layer, d_hg)
        x = _out_ln(y_rg.reshape(m, d_rg), y_nsa.reshape(m, d_nsa), y_hg.reshape(m, d_hg), w_out_b,
                    x.reshape(m, d), mod4, ln_g, ln_b, layer, alpha, t).reshape(b, t, d)
    return x
```

```python
import functools

import numpy as np
import jax
import jax.numpy as jnp
from jax import lax
from jax.experimental import pallas as pl
from jax.experimental.pallas import tpu as pltpu

F32 = jnp.float32
MXU_DTYPE = jnp.bfloat16

HEAD_DIM = 128
RG_CONV = 4
RG_C = 8.0
NSA_KV = 2
NSA_GROUP = 8
CMP_LEN = 32
CMP_STRIDE = 16
SEL_LEN = 64
SEL_TOPN = 16
WINDOW = 512
Q_BLOCK = 64
HG_CHUNK = 64
HG_SUB = 8
LN_EPS = 1e-5
RMS_EPS = 1e-6
NEG_INF = -1e30
FORCE_SCORE = 1e9
MASK_VAL = -(2.0 ** 100)
LOG2E = 1.4426950408889634
POS_LANE = 32

LANES = 128
SUBLANES = 8
VMEM_LIMIT = 56 * 1024 * 1024


def _mm(a, b):
    return jnp.dot(a.astype(MXU_DTYPE), b.astype(MXU_DTYPE), preferred_element_type=F32)


def _mm_nt(a, b):
    return lax.dot_general(a.astype(MXU_DTYPE), b.astype(MXU_DTYPE),
                           (((1,), (1,)), ((), ())), preferred_element_type=F32)


def _mm_tn(a, b):
    return lax.dot_general(a.astype(MXU_DTYPE), b.astype(MXU_DTYPE),
                           (((0,), (0,)), ((), ())), preferred_element_type=F32)


def _sigmoid(v):
    return jax.nn.sigmoid(v)


def _silu(v):
    return v * jax.nn.sigmoid(v)


def _params(*semantics):
    return pltpu.CompilerParams(dimension_semantics=semantics, vmem_limit_bytes=VMEM_LIMIT)


def _ada_kernel(c_ref, w_ref, b_ref, o_ref):
    o_ref[0] = _mm(c_ref[...], w_ref[0]) + b_ref[0]


def _ada(c_pad, w_ada, b_ada):
    depth, d, n3 = w_ada.shape
    rows = c_pad.shape[0]
    tn = 1024
    return pl.pallas_call(
        _ada_kernel,
        grid=(depth, n3 // tn),
        in_specs=[pl.BlockSpec((rows, d), lambda l, j: (0, 0)),
                  pl.BlockSpec((1, d, tn), lambda l, j: (l, 0, j)),
                  pl.BlockSpec((1, 1, tn), lambda l, j: (l, 0, j))],
        out_specs=pl.BlockSpec((1, rows, tn), lambda l, j: (l, 0, j)),
        out_shape=jax.ShapeDtypeStruct((depth, rows, n3), F32),
        compiler_params=_params("parallel", "parallel"),
        name="ada",
    )(c_pad, w_ada, b_ada.reshape(depth, 1, n3))


def _modulate_kernel(x_ref, shift_ref, scale_ref, o_ref):
    o_ref[0] = (x_ref[0] * (1.0 + scale_ref[0, 0]) + shift_ref[0, 0]).astype(o_ref.dtype)


def _modulate(x, mod4, layer):
    b, t, d = x.shape
    tt = 256
    return pl.pallas_call(
        _modulate_kernel,
        grid=(b, t // tt),
        in_specs=[pl.BlockSpec((1, tt, d), lambda i, j: (i, j, 0)),
                  pl.BlockSpec((1, 1, 1, d), lambda i, j: (layer, i, 0, 0)),
                  pl.BlockSpec((1, 1, 1, d), lambda i, j: (layer, i, 0, 1))],
        out_specs=pl.BlockSpec((1, tt, d), lambda i, j: (i, j, 0)),
        out_shape=jax.ShapeDtypeStruct((b, t, d), MXU_DTYPE),
        compiler_params=_params("parallel", "parallel"),
        name="modulate",
    )(x, mod4, mod4)


def _prep_w_in_kernel(w_ref, o_ref, prev_ref, *, gl_blk, n_gl):
    j = pl.program_id(1)
    cur = w_ref[0]
    tn = cur.shape[1]
    lane = lax.broadcasted_iota(jnp.int32, cur.shape, 1)

    @pl.when(j < gl_blk)
    def _():
        o_ref[0] = cur.astype(o_ref.dtype)

    @pl.when(j == gl_blk)
    def _():
        o_ref[0] = jnp.where(lane < n_gl, cur, 0.0).astype(o_ref.dtype)

    @pl.when(j > gl_blk)
    def _():
        mixed = jnp.where(lane >= n_gl, prev_ref[...], cur)
        o_ref[0] = pltpu.roll(mixed, tn - n_gl, axis=1).astype(o_ref.dtype)

    prev_ref[...] = cur


def _prep_w_in(w_in, c_gl, n_gl, tn):
    depth, d, n = w_in.shape
    assert c_gl % tn == 0 and (n - n_gl) % tn == 0
    n_out = n - n_gl + tn
    return pl.pallas_call(
        functools.partial(_prep_w_in_kernel, gl_blk=c_gl // tn, n_gl=n_gl),
        grid=(depth, n_out // tn),
        in_specs=[pl.BlockSpec((1, d, tn), lambda l, j: (l, 0, j))],
        out_specs=pl.BlockSpec((1, d, tn), lambda l, j: (l, 0, j)),
        out_shape=jax.ShapeDtypeStruct((depth, d, n_out), MXU_DTYPE),
        scratch_shapes=[pltpu.VMEM((d, tn), F32)],
        compiler_params=_params("arbitrary", "arbitrary"),
        name="prep_w_in",
    )(w_in)


def _in_proj_kernel(a_ref, w_ref, o_ref):
    o_ref[...] = jnp.dot(a_ref[...], w_ref[0], preferred_element_type=F32)


def _in_proj(u, w_in_p, layer):
    m, d = u.shape
    n = w_in_p.shape[2]
    tm, tn = 1024, 512
    return pl.pallas_call(
        _in_proj_kernel,
        grid=(m // tm, n // tn),
        in_specs=[pl.BlockSpec((tm, d), lambda i, j: (i, 0)),
                  pl.BlockSpec((1, d, tn), lambda i, j: (layer, 0, j))],
        out_specs=pl.BlockSpec((tm, tn), lambda i, j: (i, j)),
        out_shape=jax.ShapeDtypeStruct((m, n), F32),
        compiler_params=_params("parallel", "parallel"),
        name="in_proj",
    )(u, w_in_p)


def _rglru_kernel(x_ref, g_ref, cw_ref, cb_ref, wa_ref, ba_ref, wx_ref, bx_ref, lam_ref, o_ref,
                  xp_ref, a_ref, b_ref, h_ref):
    t, cb = x_ref.shape[1], x_ref.shape[2]
    pad = SUBLANES
    xp_ref[0:pad, :] = jnp.zeros((pad, cb), F32)
    xp_ref[pad:, :] = x_ref[0]
    xc = cb_ref[0]
    for j in range(RG_CONV):
        xc = xc + xp_ref[pl.ds(pad - (RG_CONV - 1) + j, t), :] * cw_ref[0, j:j + 1, :]

    row = lax.broadcasted_iota(jnp.int32, (t, HEAD_DIM), 0)
    sub = jnp.bitwise_and(row, SUBLANES - 1)
    for n in range(cb // HEAD_DIM):
        sl = slice(n * HEAD_DIM, (n + 1) * HEAD_DIM)
        xb = xc[:, sl]
        r = _sigmoid(_mm(xb, wa_ref[0, n]) + ba_ref[0, :, sl])
        i = _sigmoid(_mm(xb, wx_ref[0, n]) + bx_ref[0, :, sl])
        neg_lam = -lam_ref[0, :, sl]
        softplus = jnp.maximum(neg_lam, 0.0) + jnp.log1p(jnp.exp(-jnp.abs(neg_lam)))
        log_a = (-RG_C * softplus) * r
        a = jnp.exp(log_a)
        mult = jnp.sqrt(-jnp.tanh(log_a) * (a * a + 1.0))
        mult = jnp.where(row == 0, 1.0, mult)
        bx = mult * (i * xb)
        s = 1
        while s < SUBLANES:
            a_sh = pltpu.roll(a, s, axis=0)
            b_sh = pltpu.roll(bx, s, axis=0)
            inside = sub >= s
            bx = jnp.where(inside, a * b_sh + bx, bx)
            a = jnp.where(inside, a * a_sh, a)
            s *= 2
        a_ref[:, sl] = a
        b_ref[:, sl] = bx

    def carry_rows(v, h):
        r0 = pl.multiple_of(v * SUBLANES, SUBLANES)
        hh = a_ref[pl.ds(r0, SUBLANES), :] * h + b_ref[pl.ds(r0, SUBLANES), :]
        h_ref[pl.ds(r0, SUBLANES), :] = hh
        return jnp.broadcast_to(hh[SUBLANES - 1:SUBLANES, :], hh.shape)

    lax.fori_loop(0, t // SUBLANES, carry_rows, jnp.zeros((SUBLANES, cb), F32))
    o_ref[0] = (h_ref[...] * _silu(g_ref[0])).astype(o_ref.dtype)


def _rglru(proj3, conv_w, conv_b, w_a, b_a, w_x, b_x, lam, layer, d_rg):
    b, t, _ = proj3.shape
    depth = conv_w.shape[0]
    cb = 256
    nblk = cb // HEAD_DIM
    ncb = d_rg // cb
    vec = lambda v: v.reshape(depth, 1, d_rg)
    vspec = pl.BlockSpec((1, 1, cb), lambda i, j: (layer, 0, j))
    wspec = pl.BlockSpec((1, nblk, HEAD_DIM, HEAD_DIM), lambda i, j: (layer, j, 0, 0))
    return pl.pallas_call(
        _rglru_kernel,
        grid=(b, ncb),
        in_specs=[pl.BlockSpec((1, t, cb), lambda i, j: (i, 0, j)),
                  pl.BlockSpec((1, t, cb), lambda i, j: (i, 0, ncb + j)),
                  pl.BlockSpec((1, RG_CONV, cb), lambda i, j: (layer, 0, j)),
                  vspec, wspec, vspec, wspec, vspec, vspec],
        out_specs=pl.BlockSpec((1, t, cb), lambda i, j: (i, 0, j)),
        out_shape=jax.ShapeDtypeStruct((b, t, d_rg), MXU_DTYPE),
        scratch_shapes=[pltpu.VMEM((t + SUBLANES, cb), F32), pltpu.VMEM((t, cb), F32),
                        pltpu.VMEM((t, cb), F32), pltpu.VMEM((t, cb), F32)],
        compiler_params=_params("parallel", "parallel"),
        name="rglru",
    )(proj3, proj3, conv_w, vec(conv_b), w_a, vec(b_a), w_x, vec(b_x), vec(lam))


def _hgrn2_kernel(q_ref, f_ref, v_ref, g_ref, lbr_ref, ng_ref, o_ref,
                  qs_ref, kk_ref, b_ref, oo_ref, *, layer):
    t = q_ref.shape[1]
    ch, sb = HG_CHUNK, HG_SUB
    nsb = ch // sb
    z = f_ref[0]
    ez = jnp.exp(-jnp.abs(z))
    log_sig = jnp.minimum(z, 0.0) - jnp.log(1.0 + ez)
    inv = 1.0 / (1.0 + ez)
    sig_neg = jnp.where(z >= 0.0, ez * inv, inv)
    if layer == 0:
        log_f = log_sig
        kk = sig_neg
    else:
        raw = lbr_ref[...]
        e = jnp.exp(raw - jnp.max(raw, axis=0, keepdims=True))
        p = e / jnp.sum(e, axis=0, keepdims=True)
        lb = p[1:2]
        for j in range(2, layer + 1):
            lb = lb + p[j:j + 1]
        log_lb = jnp.log(lb)
        other = jnp.log1p(-lb) + log_sig
        log_f = jnp.maximum(log_lb, other) + jnp.log(1.0 + jnp.exp(-jnp.abs(log_lb - other)))
        kk = (1.0 - lb) * sig_neg
    row = lax.broadcasted_iota(jnp.int32, (t, HEAD_DIM), 0)
    in_chunk = jnp.bitwise_and(row, ch - 1)
    bcum = log_f
    s = 1
    while s < ch:
        bcum = bcum + jnp.where(in_chunk >= s, pltpu.roll(bcum, s, axis=0), 0.0)
        s *= 2
    b_ref[...] = bcum
    kk_ref[...] = kk
    qs_ref[...] = _silu(q_ref[0])

    ones = jnp.ones((HEAD_DIM, ch), MXU_DTYPE)
    lane = lax.broadcasted_iota(jnp.int32, (sb, ch), 1)
    subrow = lax.broadcasted_iota(jnp.int32, (sb, HEAD_DIM), 0)

    def chunk(c, st):
        r0 = pl.multiple_of(c * ch, ch)
        bq = b_ref[pl.ds(r0, ch), :]
        qc = qs_ref[pl.ds(r0, ch), :]
        kc = kk_ref[pl.ds(r0, ch), :]
        vc = v_ref[0, pl.ds(r0, ch), :]
        blast = bq[ch - 1:ch, :]
        o = _mm_nt(qc * jnp.exp(bq), st)
        a_rows = []
        diag = []
        for blk in range(nsb):
            lo = blk * sb
            b_i, q_i, k_i = bq[lo:lo + sb], qc[lo:lo + sb], kc[lo:lo + sb]
            for s_ in range(sb):
                dec = jnp.where(subrow >= s_, jnp.exp(b_i - b_i[s_:s_ + 1]), 0.0)
                diag.append(q_i * (k_i[s_:s_ + 1] * dec))
            if blk == 0:
                a_rows.append(jnp.zeros((sb, ch), F32))
            else:
                m_i = bq[lo - 1:lo]
                qd = q_i * jnp.exp(b_i - m_i)
                kd = kc[0:lo] * jnp.exp(m_i - bq[0:lo])
                kd = jnp.concatenate([kd, jnp.zeros((ch - lo, HEAD_DIM), F32)], axis=0)
                a_rows.append(_mm_nt(qd, kd))
        dsum = _mm(jnp.concatenate(diag, axis=0), ones)
        for blk in range(nsb):
            acc = a_rows[blk]
            for s_ in range(sb):
                idx = blk * sb + s_
                acc = acc + jnp.where(lane == idx, dsum[idx * sb:(idx + 1) * sb], 0.0)
            a_rows[blk] = acc
        a_mat = jnp.concatenate(a_rows, axis=0)
        oo_ref[pl.ds(r0, ch), :] = o + _mm(a_mat, vc)
        kdec = kc * jnp.exp(blast - bq)
        return st * jnp.exp(blast) + _mm_tn(vc, kdec)

    lax.fori_loop(0, t // ch, chunk, jnp.zeros((HEAD_DIM, HEAD_DIM), F32), unroll=8)
    o = oo_ref[...]
    o = o * lax.rsqrt(jnp.mean(o * o, axis=-1, keepdims=True) + RMS_EPS) * ng_ref[0]
    o_ref[0] = (o * _silu(g_ref[0])).astype(o_ref.dtype)


def _hgrn2(proj3, col0, lower_bounds, norm_g, layer, d_hg):
    b, t, _ = proj3.shape
    depth = lower_bounds.shape[0]
    heads = d_hg // HEAD_DIM
    c0 = col0 // HEAD_DIM

    def colspec(k):
        return pl.BlockSpec((1, t, HEAD_DIM), lambda i, h: (i, 0, c0 + k * heads + h))

    return pl.pallas_call(
        functools.partial(_hgrn2_kernel, layer=layer),
        grid=(b, heads),
        in_specs=[colspec(0), colspec(1), colspec(2), colspec(3),
                  pl.BlockSpec((depth, HEAD_DIM), lambda i, h: (0, h)),
                  pl.BlockSpec((1, 1, HEAD_DIM), lambda i, h: (layer, 0, 0))],
        out_specs=pl.BlockSpec((1, t, HEAD_DIM), lambda i, h: (i, 0, h)),
        out_shape=jax.ShapeDtypeStruct((b, t, d_hg), MXU_DTYPE),
        scratch_shapes=[pltpu.VMEM((t, HEAD_DIM), F32)] * 4,
        compiler_params=_params("parallel", "parallel"),
        name="hgrn2",
    )(proj3, proj3, proj3, proj3, lower_bounds, norm_g.reshape(depth, 1, HEAD_DIM))


def _compress_kernel(x_ref, pek_ref, pev_ref, w1k_ref, w2k_ref, w1v_ref, w2v_ref, ko_ref, vo_ref):
    nrow = x_ref.shape[1]
    width = 2 * NSA_KV * HEAD_DIM
    row = lax.broadcasted_iota(jnp.int32, (nrow, HEAD_DIM), 0)
    branches = ((pek_ref, w1k_ref, w2k_ref, ko_ref), (pev_ref, w1v_ref, w2v_ref, vo_ref))
    for kv, (pe_ref, w1_ref, w2_ref, out_ref) in enumerate(branches):
        for g in range(NSA_KV):
            c0 = (kv * NSA_KV + g) * HEAD_DIM
            lo = jnp.concatenate(
                [x_ref[0, :, i * width + c0:i * width + c0 + HEAD_DIM] for i in range(CMP_STRIDE)], axis=1)
            hi = pltpu.roll(lo, nrow - 1, axis=0)
            blk = jnp.concatenate([lo, hi], axis=1) + pe_ref[0]
            hid = _silu(_mm(blk, w1_ref[0]))
            out = _mm(hid, w2_ref[0])
            out_ref[0, g] = jnp.where(row < nrow - 1, out, 0.0)


def _compress(kvc, pe_k, pe_v, w1_k, w2_k, w1_v, w2_v, layer):
    b, nrow, wide = kvc.shape
    depth = pe_k.shape[0]
    hidden = w1_k.shape[2]
    flat = CMP_LEN * HEAD_DIM
    pespec = pl.BlockSpec((1, 1, flat), lambda i: (layer, 0, 0))
    w1spec = pl.BlockSpec((1, flat, hidden), lambda i: (layer, 0, 0))
    w2spec = pl.BlockSpec((1, hidden, HEAD_DIM), lambda i: (layer, 0, 0))
    ospec = pl.BlockSpec((1, NSA_KV, nrow, HEAD_DIM), lambda i: (i, 0, 0, 0))
    oshape = jax.ShapeDtypeStruct((b, NSA_KV, nrow, HEAD_DIM), F32)
    return pl.pallas_call(
        _compress_kernel,
        grid=(b,),
        in_specs=[pl.BlockSpec((1, nrow, wide), lambda i: (i, 0, 0)),
                  pespec, pespec, w1spec, w2spec, w1spec, w2spec],
        out_specs=[ospec, ospec],
        out_shape=[oshape, oshape],
        compiler_params=_params("parallel"),
        name="nsa_compress",
    )(kvc, pe_k.reshape(depth, 1, flat), pe_v.reshape(depth, 1, flat), w1_k, w2_k, w1_v, w2_v)


def _nsa_kernel(slt_ref, kaux_ref, kauxc_ref, q_ref, gl_ref, sg_ref, ks_ref, vs_ref, kw_ref, vw_ref,
                kc_ref, vc_ref, ov_ref, ge_ref, o_ref,
                ksx_ref, kwx_ref, vsb_ref, vwb_ref, kcx_ref, vcb_ref):
    qb = pl.program_id(1)
    q0 = qb * Q_BLOCK
    q0a = pl.multiple_of(q0, Q_BLOCK)
    nq = NSA_GROUP * Q_BLOCK
    gw = NSA_GROUP * HEAD_DIM
    t_all = ks_ref.shape[1]
    n_sel = t_all // SEL_LEN
    kchunk = 512
    groups = range(NSA_KV)

    @pl.when(qb == 0)
    def _():
        for g in groups:
            cols = slice(g * HEAD_DIM, (g + 1) * HEAD_DIM)
            ksx_ref[g, :, 0:HEAD_DIM] = ks_ref[0, :, cols].astype(MXU_DTYPE)
            ksx_ref[g, :, HEAD_DIM:] = kaux_ref[...]
            kwx_ref[g, :, 0:HEAD_DIM] = kw_ref[0, :, cols].astype(MXU_DTYPE)
            kwx_ref[g, :, HEAD_DIM:] = kaux_ref[...]
            kcx_ref[g, :, 0:HEAD_DIM] = kc_ref[0, g].astype(MXU_DTYPE)
            kcx_ref[g, :, HEAD_DIM:] = kauxc_ref[...]
            vsb_ref[g] = vs_ref[0, :, cols].astype(MXU_DTYPE)
            vwb_ref[g] = vw_ref[0, :, cols].astype(MXU_DTYPE)
            vcb_ref[g] = vc_ref[0, g].astype(MXU_DTYPE)

    def stack(v):
        return jnp.concatenate([v] * NSA_GROUP, axis=0)

    def online(carry, s, v):
        m, l, acc = carry
        m_new = jnp.maximum(m, jnp.max(s, axis=1, keepdims=True))
        alpha = jnp.exp2(m - m_new)
        p = jnp.exp2(s - m_new)
        l = alpha * l + jnp.sum(p, axis=1, keepdims=True)
        acc = alpha * acc + jnp.dot(p.astype(MXU_DTYPE), v, preferred_element_type=F32)
        return m_new, l, acc

    def finish(carry):
        _, l, acc = carry
        return acc * jnp.where(l > 0.0, 1.0 / l, 0.0)

    init = (jnp.full((nq, 1), NEG_INF, F32), jnp.zeros((nq, 1), F32), jnp.zeros((nq, HEAD_DIM), F32))
    t_loc = lax.broadcasted_iota(jnp.int32, (Q_BLOCK, LANES), 0)
    lane = lax.broadcasted_iota(jnp.int32, (Q_BLOCK, LANES), 1)

    ncmp = kcx_ref.shape[1]
    pos_c = (CMP_LEN - 1) + CMP_STRIDE * lax.broadcasted_iota(jnp.int32, (Q_BLOCK, ncmp), 1)
    ok_c = stack(jnp.where(q0 + lax.broadcasted_iota(jnp.int32, (Q_BLOCK, ncmp), 0) >= pos_c, 1.0, 0.0)) > 0.5
    tri_low = stack(jnp.where(lax.broadcasted_iota(jnp.int32, (Q_BLOCK, Q_BLOCK), 1)
                              <= lax.broadcasted_iota(jnp.int32, (Q_BLOCK, Q_BLOCK), 0), 1.0, 0.0)) > 0.5
    n_full = WINDOW // SEL_LEN - 1
    win_bias = jnp.where(lane < n_sel,
                         jnp.where(lane >= qb - n_full, jnp.where(lane < qb, 0.0, MASK_VAL), MASK_VAL), 0.0)
    has_far = jnp.where(q0 >= WINDOW, 1.0, 0.0)
    ok_edge = stack(jnp.where(lane < Q_BLOCK,
                              jnp.where(lane > t_loc, has_far, 0.0),
                              jnp.where(lane - Q_BLOCK <= t_loc, 1.0, 0.0))
                    ) > 0.5
    w0 = pl.multiple_of(jnp.maximum(q0 - WINDOW, 0), Q_BLOCK)
    nblk = lax.broadcasted_iota(jnp.int32, (n_sel, Q_BLOCK), 0)
    forced = (nblk == 0) | (nblk == qb) | (nblk == qb - 1)

    q_plain, q_sel, o_cmp, o_win = [], [], [], []
    for g in groups:
        q = q_ref[0, :, g * gw:(g + 1) * gw] * (HEAD_DIM ** -0.5 * LOG2E)
        q_main = jnp.concatenate([q[:, r * HEAD_DIM:(r + 1) * HEAD_DIM] for r in range(NSA_GROUP)],
                                 axis=0).astype(MXU_DTYPE)
        slope_lanes = slt_ref[g]

        def with_aux(block_bias, q_main=q_main, slope_lanes=slope_lanes):
            return jnp.concatenate([q_main, (stack(block_bias) + slope_lanes).astype(MXU_DTYPE)], axis=1)

        qp = with_aux(jnp.zeros((Q_BLOCK, LANES), F32))
        q_plain.append(qp)

        carry = online(init, _mm_nt(with_aux(win_bias), kwx_ref[g, pl.ds(w0, WINDOW), :]),
                       vwb_ref[g, pl.ds(w0, WINDOW), :])
        k_edge = jnp.concatenate([kwx_ref[g, pl.ds(w0, Q_BLOCK), :], kwx_ref[g, pl.ds(q0a, Q_BLOCK), :]], axis=0)
        v_edge = jnp.concatenate([vwb_ref[g, pl.ds(w0, Q_BLOCK), :], vwb_ref[g, pl.ds(q0a, Q_BLOCK), :]], axis=0)
        s_edge = jnp.where(ok_edge, _mm_nt(qp, k_edge), MASK_VAL)
        o_win.append(finish(online(carry, s_edge, v_edge)))

        s_c = jnp.where(ok_c, _mm_nt(qp, kcx_ref[g]), MASK_VAL)
        p_c = jnp.exp2(s_c - jnp.maximum(jnp.max(s_c, axis=1, keepdims=True), NEG_INF))
        l_c = jnp.sum(p_c, axis=1, keepdims=True)
        p_c = p_c * jnp.where(l_c > 0.0, 1.0 / l_c, 0.0)
        o_cmp.append(jnp.dot(p_c.astype(MXU_DTYPE), vcb_ref[g], preferred_element_type=F32))

        p_sum = p_c[0:Q_BLOCK]
        for r in range(1, NSA_GROUP):
            p_sum = p_sum + p_c[r * Q_BLOCK:(r + 1) * Q_BLOCK]
        imp = lax.dot_general(ov_ref[...], p_sum, (((1,), (1,)), ((), ())),
                              precision=lax.Precision.HIGHEST, preferred_element_type=F32)[0:n_sel]
        imp = jnp.where(nblk > qb, NEG_INF, jnp.where(forced, FORCE_SCORE, imp))
        rank = jnp.zeros(imp.shape, F32)
        for m_ in range(n_sel):
            other = imp[m_:m_ + 1, :]
            rank = rank + jnp.where(nblk > m_, jnp.where(other >= imp, 1.0, 0.0), jnp.where(other > imp, 1.0, 0.0))
        sel_bias = jnp.where(rank < float(min(SEL_TOPN, n_sel)), jnp.where(nblk < qb, 0.0, MASK_VAL), MASK_VAL)
        sel_bias = jnp.concatenate([sel_bias, jnp.zeros((LANES - n_sel, Q_BLOCK), F32)], axis=0).T
        q_sel.append(with_aux(sel_bias))

    def sel_branch(n_chunks):
        def run():
            outs = []
            for g in groups:
                carry = init
                for ci in range(n_chunks):
                    rows = slice(ci * kchunk, (ci + 1) * kchunk)
                    carry = online(carry, _mm_nt(q_sel[g], ksx_ref[g, rows, :]), vsb_ref[g, rows, :])
                s_own = jnp.where(tri_low, _mm_nt(q_plain[g], ksx_ref[g, pl.ds(q0a, Q_BLOCK), :]), MASK_VAL)
                outs.append(finish(online(carry, s_own, vsb_ref[g, pl.ds(q0a, Q_BLOCK), :])))
            return tuple(outs)
        return run

    max_chunks = (t_all - Q_BLOCK + kchunk - 1) // kchunk
    o_sel = lax.switch((q0 + kchunk - 1) // kchunk, [sel_branch(n) for n in range(max_chunks + 1)])

    def unstack(o):
        return jnp.concatenate([o[r * Q_BLOCK:(r + 1) * Q_BLOCK] for r in range(NSA_GROUP)], axis=1)

    sig = _sigmoid(gl_ref[0])
    sig_hi = sig.astype(MXU_DTYPE)
    sig_lo = (sig - sig_hi.astype(F32)).astype(MXU_DTYPE)
    for g in groups:
        out = None
        for i, branch in enumerate((o_cmp[g], o_sel[g], o_win[g])):
            gate = (jnp.dot(sig_hi, ge_ref[g, i], preferred_element_type=F32)
                    + jnp.dot(sig_lo, ge_ref[g, i], preferred_element_type=F32))
            term = gate * unstack(branch)
            out = term if out is None else out + term
        cols = slice(g * gw, (g + 1) * gw)
        o_ref[0, :, cols] = (out * _silu(sg_ref[0, :, cols])).astype(o_ref.dtype)


def _split3(v):
    bf = jnp.bfloat16
    hi = v.astype(bf).astype(np.float64)
    mid = (v - hi).astype(bf).astype(np.float64)
    lo = (v - hi - mid).astype(bf).astype(np.float64)
    return hi, mid, lo


def _slope_lanes():
    heads = NSA_KV * NSA_GROUP
    h = np.arange(1, heads + 1, dtype=np.float32)
    s = np.power(np.float32(2.0), -8.0 * h / heads).astype(np.float32).astype(np.float64) * LOG2E
    parts = _split3(s)
    out = np.zeros((heads, LANES), np.float32)
    for rep in range(2):
        for i, part in enumerate(parts):
            out[:, POS_LANE + 3 * rep + i] = part
    out = np.repeat(out.reshape(NSA_KV, NSA_GROUP, 1, LANES), Q_BLOCK, axis=2)
    return jnp.asarray(out.reshape(NSA_KV, NSA_GROUP * Q_BLOCK, LANES))


def _key_aux(pos, blocks):
    out = np.zeros((pos.shape[0], LANES), np.float32)
    if blocks:
        out[np.arange(pos.shape[0]), pos // SEL_LEN] = 1.0
    out[:, POS_LANE:POS_LANE + 3] = (SEL_LEN * (pos // SEL_LEN))[:, None]
    out[:, POS_LANE + 3:POS_LANE + 6] = (pos % SEL_LEN)[:, None]
    return jnp.asarray(out, dtype=MXU_DTYPE)


def _overlap_t(t, ncmp_rows):
    n_cmp = (t - CMP_LEN) // CMP_STRIDE + 1
    n_sel = t // SEL_LEN
    s_c = np.arange(n_cmp) * CMP_STRIDE
    s_s = np.arange(n_sel) * SEL_LEN
    ov = np.clip(np.minimum(s_c[:, None] + CMP_LEN, s_s[None, :] + SEL_LEN)
                 - np.maximum(s_c[:, None], s_s[None, :]), 0, None).astype(np.float32) / CMP_LEN
    out = np.zeros((LANES, ncmp_rows), np.float32)
    out[:n_sel, :n_cmp] = ov.T
    return jnp.asarray(out)


def _gate_expand():
    width = NSA_GROUP * HEAD_DIM
    e = np.zeros((NSA_KV, 3, LANES, width), np.float32)
    for g in range(NSA_KV):
        for r in range(NSA_GROUP):
            for i in range(3):
                e[g, i, (g * NSA_GROUP + r) * 3 + i, r * HEAD_DIM:(r + 1) * HEAD_DIM] = 1.0
    return jnp.asarray(e, dtype=MXU_DTYPE)


def _nsa(proj3, cols, k_cmp, v_cmp):
    b, t, _ = proj3.shape
    assert t // SEL_LEN <= POS_LANE and t % 512 == 0 and Q_BLOCK == SEL_LEN
    gw = NSA_GROUP * HEAD_DIM
    width = NSA_KV * gw
    kvw = NSA_KV * HEAD_DIM
    ncmp_rows = k_cmp.shape[2]
    q_c, ks_c, kw_c, gl_c, sg_c = cols
    assert q_c % width == 0 and sg_c % width == 0 and ks_c % kvw == 0 and kw_c % kvw == 0

    def kvspec(c, is_v):
        return pl.BlockSpec((1, t, kvw), lambda i, j: (i, 0, c // kvw + (1 if is_v else 0)))

    def whole(*shape):
        return pl.BlockSpec(shape, lambda i, j: (0,) * len(shape))

    cmpspec = pl.BlockSpec((1, NSA_KV, ncmp_rows, HEAD_DIM), lambda i, j: (i, 0, 0, 0))
    cmp_pos = (CMP_LEN - 1) + CMP_STRIDE * np.arange(ncmp_rows)
    return pl.pallas_call(
        _nsa_kernel,
        grid=(b, t // Q_BLOCK),
        in_specs=[whole(NSA_KV, NSA_GROUP * Q_BLOCK, LANES), whole(t, LANES), whole(ncmp_rows, LANES),
                  pl.BlockSpec((1, Q_BLOCK, width), lambda i, j: (i, j, q_c // width)),
                  pl.BlockSpec((1, Q_BLOCK, LANES), lambda i, j: (i, j, gl_c // LANES)),
                  pl.BlockSpec((1, Q_BLOCK, width), lambda i, j: (i, j, sg_c // width)),
                  kvspec(ks_c, False), kvspec(ks_c, True), kvspec(kw_c, False), kvspec(kw_c, True),
                  cmpspec, cmpspec, whole(LANES, ncmp_rows), whole(NSA_KV, 3, LANES, gw)],
        out_specs=pl.BlockSpec((1, Q_BLOCK, width), lambda i, j: (i, j, 0)),
        out_shape=jax.ShapeDtypeStruct((b, t, width), MXU_DTYPE),
        scratch_shapes=[pltpu.VMEM((NSA_KV, t, 2 * HEAD_DIM), MXU_DTYPE),
                        pltpu.VMEM((NSA_KV, t, 2 * HEAD_DIM), MXU_DTYPE),
                        pltpu.VMEM((NSA_KV, t, HEAD_DIM), MXU_DTYPE), pltpu.VMEM((NSA_KV, t, HEAD_DIM), MXU_DTYPE),
                        pltpu.VMEM((NSA_KV, ncmp_rows, 2 * HEAD_DIM), MXU_DTYPE),
                        pltpu.VMEM((NSA_KV, ncmp_rows, HEAD_DIM), MXU_DTYPE)],
        compiler_params=_params("parallel", "arbitrary"),
        name="nsa_attn",
    )(_slope_lanes(), _key_aux(np.arange(t), True), _key_aux(cmp_pos, False),
      proj3, proj3, proj3, proj3, proj3, proj3, proj3, k_cmp, v_cmp, _overlap_t(t, ncmp_rows), _gate_expand())


def _out_ln_kernel(a0_ref, a1_ref, a2_ref, a3_ref, w0_ref, w1_ref, w2_ref, w3_ref,
                   x_ref, gate_ref, lng_ref, lnb_ref, shift_ref, scale_ref, *rest, alpha, emit_u):
    if emit_u:
        o_ref, u_ref, pre_ref, s1_ref, s2_ref = rest
    else:
        (o_ref, pre_ref, s1_ref, s2_ref), u_ref = rest, None
    i, j = pl.program_id(0), pl.program_id(1)
    n_row, nj = pl.num_programs(0) - 1, pl.num_programs(1)
    tn = o_ref.shape[1]
    slot = lax.rem(i, 2)

    @pl.when(i < n_row)
    def _():
        y = (jnp.dot(a0_ref[...], w0_ref[0], preferred_element_type=F32)
             + jnp.dot(a1_ref[...], w1_ref[0], preferred_element_type=F32)
             + jnp.dot(a2_ref[...], w2_ref[0], preferred_element_type=F32)
             + jnp.dot(a3_ref[...], w3_ref[0], preferred_element_type=F32))
        pre = alpha * x_ref[...] + (1.0 + gate_ref[0, 0]) * y
        pre_ref[slot * nj + j] = pre
        r1 = pre.sum(axis=-1, keepdims=True)
        r2 = (pre * pre).sum(axis=-1, keepdims=True)

        @pl.when(j == 0)
        def _():
            s1_ref[slot] = r1
            s2_ref[slot] = r2

        @pl.when(j > 0)
        def _():
            s1_ref[slot] = s1_ref[slot] + r1
            s2_ref[slot] = s2_ref[slot] + r2

    @pl.when(i > 0)
    def _():
        d = nj * tn
        prev = 1 - slot
        mu = s1_ref[prev] / d
        inv = lax.rsqrt(s2_ref[prev] / d - mu * mu + LN_EPS)
        out = (pre_ref[prev * nj + j] - mu) * inv * lng_ref[0] + lnb_ref[0]
        o_ref[...] = out
        if emit_u:
            u_ref[...] = (out * (1.0 + scale_ref[0, 0]) + shift_ref[0, 0]).astype(u_ref.dtype)


def _out_ln(y_rg, y_nsa, y_hg, w_out_b, x2, mod4, ln_g, ln_b, layer, alpha, t, emit_u):
    m, d = x2.shape
    depth = w_out_b.shape[0]
    kb = y_rg.shape[1]
    assert y_nsa.shape[1] == 2 * kb and y_hg.shape[1] == kb and d == 4 * kb
    tm, tn = 512, 512
    nj, n_row = d // tn, m // tm
    assert t % tm == 0
    mrow = lambda i: jnp.minimum(i, n_row - 1)
    mcol = lambda i, j: jnp.where(i < n_row, j, nj - 1)
    orow = lambda i: jnp.maximum(i - 1, 0)
    ocol = lambda i, j: jnp.where(i > 0, j, 0)
    aspec = lambda c: pl.BlockSpec((tm, kb), lambda i, j: (mrow(i), c))
    wspec = lambda r: pl.BlockSpec((1, kb, tn), lambda i, j: (layer, r, mcol(i, j)))
    vspec = pl.BlockSpec((1, 1, tn), lambda i, j: (layer, 0, ocol(i, j)))
    nxt = min(layer + 1, depth - 1)
    modspec = lambda part: pl.BlockSpec(
        (1, 1, 1, tn), lambda i, j: (nxt, (orow(i) * tm) // t, 0, part * nj + ocol(i, j)))
    ospec = pl.BlockSpec((tm, tn), lambda i, j: (orow(i), ocol(i, j)))
    oshape = jax.ShapeDtypeStruct((m, d), F32)
    return pl.pallas_call(
        functools.partial(_out_ln_kernel, alpha=alpha, emit_u=emit_u),
        grid=(n_row + 1, nj),
        in_specs=[aspec(0), aspec(0), aspec(1), aspec(0), wspec(0), wspec(1), wspec(2), wspec(3),
                  pl.BlockSpec((tm, tn), lambda i, j: (mrow(i), mcol(i, j))),
                  pl.BlockSpec((1, 1, 1, tn), lambda i, j: (layer, (mrow(i) * tm) // t, 0, 2 * nj + mcol(i, j))),
                  vspec, vspec, modspec(0), modspec(1)],
        out_specs=[ospec, ospec] if emit_u else ospec,
        out_shape=[oshape, jax.ShapeDtypeStruct((m, d), MXU_DTYPE)] if emit_u else oshape,
        scratch_shapes=[pltpu.VMEM((2 * nj, tm, tn), F32), pltpu.VMEM((2, tm, 1), F32), pltpu.VMEM((2, tm, 1), F32)],
        compiler_params=_params("arbitrary", "arbitrary"),
        name="out_ln",
    )(y_rg, y_nsa, y_nsa, y_hg, w_out_b, w_out_b, w_out_b, w_out_b, x2, mod4,
      ln_g.reshape(depth, 1, d), ln_b.reshape(depth, 1, d), mod4, mod4)


def kernel(x, c, w_ada, b_ada, w_in, rg_conv_w, rg_conv_b, rg_w_a, rg_b_a, rg_w_x, rg_b_x, rg_lambda,
           nsa_pe_k, nsa_pe_v, nsa_cmp_w1_k, nsa_cmp_w2_k, nsa_cmp_w1_v, nsa_cmp_w2_v,
           hg_lower_bounds, hg_norm_g, w_out, ln_g, ln_b):
    b, t, d = x.shape
    depth = w_ada.shape[0]
    m = b * t
    d_rg, d_nsa, d_hg = d // 4, d // 2, d // 4
    kv_cols = 2 * NSA_KV * HEAD_DIM
    n_gl = 3 * NSA_KV * NSA_GROUP
    alpha = (2.0 * depth) ** 0.25
    assert d_nsa == NSA_KV * NSA_GROUP * HEAD_DIM and b <= SUBLANES

    gl_pad = 512
    c_q = 2 * d_rg
    c_kvc = c_q + d_nsa
    c_kvs = c_kvc + kv_cols
    c_kvw = c_kvs + kv_cols
    c_gl = c_kvw + kv_cols
    c_sg = c_gl + gl_pad
    c_hg = c_sg + d_nsa
    w_in_p = _prep_w_in(w_in, c_gl, n_gl, gl_pad)
    w_out_b = w_out.astype(MXU_DTYPE)
    w1_k, w1_v = nsa_cmp_w1_k.astype(MXU_DTYPE), nsa_cmp_w1_v.astype(MXU_DTYPE)

    c_pad = jnp.pad(c, ((0, SUBLANES - b), (0, 0)))
    mod4 = _ada(c_pad, w_ada, b_ada).reshape(depth, SUBLANES, 1, 3 * d)

    u = _modulate(x, mod4, 0).reshape(m, d)
    x = x.reshape(m, d)
    for layer in range(depth):
        proj3 = _in_proj(u, w_in_p, layer).reshape(b, t, -1)
        y_rg = _rglru(proj3, rg_conv_w, rg_conv_b, rg_w_a, rg_b_a, rg_w_x, rg_b_x, rg_lambda, layer, d_rg)
        kvc = proj3[:, :, c_kvc:c_kvc + kv_cols].reshape(b, t // CMP_STRIDE, CMP_STRIDE * kv_cols)
        k_cmp, v_cmp = _compress(kvc, nsa_pe_k, nsa_pe_v, w1_k, nsa_cmp_w2_k, w1_v, nsa_cmp_w2_v, layer)
        y_nsa = _nsa(proj3, (c_q, c_kvs, c_kvw, c_gl, c_sg), k_cmp, v_cmp)
        y_hg = _hgrn2(proj3, c_hg, hg_lower_bounds, hg_norm_g, layer, d_hg)
        emit_u = layer + 1 < depth
        res = _out_ln(y_rg.reshape(m, d_rg), y_nsa.reshape(m, d_nsa), y_hg.reshape(m, d_hg), w_out_b,
                      x, mod4, ln_g, ln_b, layer, alpha, t, emit_u)
        x, u = res if emit_u else (res, None)
    return x.reshape(b, t, d)
```

```python
import functools

import numpy as np
import jax
import jax.numpy as jnp
from jax import lax
from jax.experimental import pallas as pl
from jax.experimental.pallas import tpu as pltpu

F32 = jnp.float32
MXU_DTYPE = jnp.bfloat16

HEAD_DIM = 128
RG_CONV = 4
RG_C = 8.0
NSA_KV = 2
NSA_GROUP = 8
CMP_LEN = 32
CMP_STRIDE = 16
SEL_LEN = 64
SEL_TOPN = 16
WINDOW = 512
Q_BLOCK = 64
HG_CHUNK = 64
HG_SUB = 8
LN_EPS = 1e-5
RMS_EPS = 1e-6
NEG_INF = -1e30
FORCE_SCORE = 1e9
MASK_VAL = -(2.0 ** 100)
LOG2E = 1.4426950408889634
POS_LANE = 32

LANES = 128
SUBLANES = 8
VMEM_LIMIT = 56 * 1024 * 1024


def _mm(a, b):
    return jnp.dot(a.astype(MXU_DTYPE), b.astype(MXU_DTYPE), preferred_element_type=F32)


def _mm_nt(a, b):
    return lax.dot_general(a.astype(MXU_DTYPE), b.astype(MXU_DTYPE),
                           (((1,), (1,)), ((), ())), preferred_element_type=F32)


def _mm_tn(a, b):
    return lax.dot_general(a.astype(MXU_DTYPE), b.astype(MXU_DTYPE),
                           (((0,), (0,)), ((), ())), preferred_element_type=F32)


def _sigmoid(v):
    return jax.nn.sigmoid(v)


def _silu(v):
    return v * jax.nn.sigmoid(v)


def _params(*semantics):
    return pltpu.CompilerParams(dimension_semantics=semantics, vmem_limit_bytes=VMEM_LIMIT)


def _ada_kernel(c_ref, w_ref, b_ref, o_ref):
    o_ref[0] = _mm(c_ref[...], w_ref[0]) + b_ref[0]


def _ada(c_pad, w_ada, b_ada):
    depth, d, n3 = w_ada.shape
    rows = c_pad.shape[0]
    tn = 1024
    return pl.pallas_call(
        _ada_kernel,
        grid=(depth, n3 // tn),
        in_specs=[pl.BlockSpec((rows, d), lambda l, j: (0, 0)),
                  pl.BlockSpec((1, d, tn), lambda l, j: (l, 0, j)),
                  pl.BlockSpec((1, 1, tn), lambda l, j: (l, 0, j))],
        out_specs=pl.BlockSpec((1, rows, tn), lambda l, j: (l, 0, j)),
        out_shape=jax.ShapeDtypeStruct((depth, rows, n3), F32),
        compiler_params=_params("parallel", "parallel"),
        name="ada",
    )(c_pad, w_ada, b_ada.reshape(depth, 1, n3))


def _modulate_kernel(x_ref, shift_ref, scale_ref, o_ref):
    o_ref[0] = (x_ref[0] * (1.0 + scale_ref[0, 0]) + shift_ref[0, 0]).astype(o_ref.dtype)


def _modulate(x, mod4, layer):
    b, t, d = x.shape
    tt = 256
    return pl.pallas_call(
        _modulate_kernel,
        grid=(b, t // tt),
        in_specs=[pl.BlockSpec((1, tt, d), lambda i, j: (i, j, 0)),
                  pl.BlockSpec((1, 1, 1, d), lambda i, j: (layer, i, 0, 0)),
                  pl.BlockSpec((1, 1, 1, d), lambda i, j: (layer, i, 0, 1))],
        out_specs=pl.BlockSpec((1, tt, d), lambda i, j: (i, j, 0)),
        out_shape=jax.ShapeDtypeStruct((b, t, d), MXU_DTYPE),
        compiler_params=_params("parallel", "parallel"),
        name="modulate",
    )(x, mod4, mod4)


def _prep_w_in_kernel(w_ref, o_ref, prev_ref, *, gl_blk, n_gl):
    j = pl.program_id(1)
    tn = w_ref.shape[1]

    @pl.when(j < gl_blk)
    def _():
        o_ref[0] = w_ref[0].T.astype(o_ref.dtype)

    @pl.when(j == gl_blk)
    def _():
        row = lax.broadcasted_iota(jnp.int32, w_ref.shape[1:], 0)
        o_ref[0] = jnp.where(row < n_gl, w_ref[0], 0.0).T.astype(o_ref.dtype)

    @pl.when(j > gl_blk)
    def _():
        tile = jnp.concatenate([prev_ref[n_gl:tn, :], w_ref[0, 0:n_gl, :]], axis=0)
        o_ref[0] = tile.T.astype(o_ref.dtype)

    prev_ref[...] = w_ref[0]


def _prep_w_in(w_in_t, c_gl, n_gl, tn):
    depth, n, d = w_in_t.shape
    assert c_gl % tn == 0 and (n - n_gl) % tn == 0 and n_gl % SUBLANES == 0
    n_out = n - n_gl + tn
    return pl.pallas_call(
        functools.partial(_prep_w_in_kernel, gl_blk=c_gl // tn, n_gl=n_gl),
        grid=(depth, n_out // tn),
        in_specs=[pl.BlockSpec((1, tn, d), lambda l, j: (l, j, 0))],
        out_specs=pl.BlockSpec((1, d, tn), lambda l, j: (l, 0, j)),
        out_shape=jax.ShapeDtypeStruct((depth, d, n_out), MXU_DTYPE),
        scratch_shapes=[pltpu.VMEM((tn, d), F32)],
        compiler_params=_params("arbitrary", "arbitrary"),
        name="prep_w_in",
    )(w_in_t)


def _in_proj_kernel(a_ref, w_ref, o_ref):
    o_ref[...] = jnp.dot(a_ref[...], w_ref[0], preferred_element_type=F32)


def _in_proj(u, w_in_p, layer):
    m, d = u.shape
    n = w_in_p.shape[2]
    tm, tn = 1024, 512
    return pl.pallas_call(
        _in_proj_kernel,
        grid=(m // tm, n // tn),
        in_specs=[pl.BlockSpec((tm, d), lambda i, j: (i, 0)),
                  pl.BlockSpec((1, d, tn), lambda i, j: (layer, 0, j))],
        out_specs=pl.BlockSpec((tm, tn), lambda i, j: (i, j)),
        out_shape=jax.ShapeDtypeStruct((m, n), F32),
        compiler_params=_params("parallel", "parallel"),
        name="in_proj",
    )(u, w_in_p)


def _rglru_kernel(x_ref, g_ref, cw_ref, cb_ref, wa_ref, ba_ref, wx_ref, bx_ref, lam_ref, o_ref,
                  xp_ref, a_ref, b_ref, h_ref):
    t, cb = x_ref.shape[1], x_ref.shape[2]
    pad = SUBLANES
    xp_ref[0:pad, :] = jnp.zeros((pad, cb), F32)
    xp_ref[pad:, :] = x_ref[0]
    xc = cb_ref[0]
    for j in range(RG_CONV):
        xc = xc + xp_ref[pl.ds(pad - (RG_CONV - 1) + j, t), :] * cw_ref[0, j:j + 1, :]

    row = lax.broadcasted_iota(jnp.int32, (t, HEAD_DIM), 0)
    sub = jnp.bitwise_and(row, SUBLANES - 1)
    for n in range(cb // HEAD_DIM):
        sl = slice(n * HEAD_DIM, (n + 1) * HEAD_DIM)
        xb = xc[:, sl]
        r = _sigmoid(_mm(xb, wa_ref[0, n]) + ba_ref[0, :, sl])
        i = _sigmoid(_mm(xb, wx_ref[0, n]) + bx_ref[0, :, sl])
        neg_lam = -lam_ref[0, :, sl]
        softplus = jnp.maximum(neg_lam, 0.0) + jnp.log1p(jnp.exp(-jnp.abs(neg_lam)))
        log_a = (-RG_C * softplus) * r
        a = jnp.exp(log_a)
        mult = jnp.sqrt(-jnp.tanh(log_a) * (a * a + 1.0))
        mult = jnp.where(row == 0, 1.0, mult)
        bx = mult * (i * xb)
        s = 1
        while s < SUBLANES:
            a_sh = pltpu.roll(a, s, axis=0)
            b_sh = pltpu.roll(bx, s, axis=0)
            inside = sub >= s
            bx = jnp.where(inside, a * b_sh + bx, bx)
            a = jnp.where(inside, a * a_sh, a)
            s *= 2
        a_ref[:, sl] = a
        b_ref[:, sl] = bx

    def carry_rows(v, h):
        r0 = pl.multiple_of(v * SUBLANES, SUBLANES)
        hh = a_ref[pl.ds(r0, SUBLANES), :] * h + b_ref[pl.ds(r0, SUBLANES), :]
        h_ref[pl.ds(r0, SUBLANES), :] = hh
        return jnp.broadcast_to(hh[SUBLANES - 1:SUBLANES, :], hh.shape)

    lax.fori_loop(0, t // SUBLANES, carry_rows, jnp.zeros((SUBLANES, cb), F32))
    o_ref[0] = (h_ref[...] * _silu(g_ref[0])).astype(o_ref.dtype)


def _rglru(proj3, conv_w, conv_b, w_a, b_a, w_x, b_x, lam, layer, d_rg):
    b, t, _ = proj3.shape
    depth = conv_w.shape[0]
    cb = 256
    nblk = cb // HEAD_DIM
    ncb = d_rg // cb
    vec = lambda v: v.reshape(depth, 1, d_rg)
    vspec = pl.BlockSpec((1, 1, cb), lambda i, j: (layer, 0, j))
    wspec = pl.BlockSpec((1, nblk, HEAD_DIM, HEAD_DIM), lambda i, j: (layer, j, 0, 0))
    return pl.pallas_call(
        _rglru_kernel,
        grid=(b, ncb),
        in_specs=[pl.BlockSpec((1, t, cb), lambda i, j: (i, 0, j)),
                  pl.BlockSpec((1, t, cb), lambda i, j: (i, 0, ncb + j)),
                  pl.BlockSpec((1, RG_CONV, cb), lambda i, j: (layer, 0, j)),
                  vspec, wspec, vspec, wspec, vspec, vspec],
        out_specs=pl.BlockSpec((1, t, cb), lambda i, j: (i, 0, j)),
        out_shape=jax.ShapeDtypeStruct((b, t, d_rg), MXU_DTYPE),
        scratch_shapes=[pltpu.VMEM((t + SUBLANES, cb), F32), pltpu.VMEM((t, cb), F32),
                        pltpu.VMEM((t, cb), F32), pltpu.VMEM((t, cb), F32)],
        compiler_params=_params("parallel", "parallel"),
        name="rglru",
    )(proj3, proj3, conv_w, vec(conv_b), w_a, vec(b_a), w_x, vec(b_x), vec(lam))


def _hgrn2_kernel(q_ref, f_ref, v_ref, g_ref, lbr_ref, ng_ref, o_ref,
                  qs_ref, kk_ref, b_ref, oo_ref, *, layer):
    t = q_ref.shape[1]
    ch, sb = HG_CHUNK, HG_SUB
    nsb = ch // sb
    z = f_ref[0]
    ez = jnp.exp(-jnp.abs(z))
    log_sig = jnp.minimum(z, 0.0) - jnp.log(1.0 + ez)
    inv = 1.0 / (1.0 + ez)
    sig_neg = jnp.where(z >= 0.0, ez * inv, inv)
    if layer == 0:
        log_f = log_sig
        kk = sig_neg
    else:
        raw = lbr_ref[...]
        e = jnp.exp(raw - jnp.max(raw, axis=0, keepdims=True))
        p = e / jnp.sum(e, axis=0, keepdims=True)
        lb = p[1:2]
        for j in range(2, layer + 1):
            lb = lb + p[j:j + 1]
        log_lb = jnp.log(lb)
        other = jnp.log1p(-lb) + log_sig
        log_f = jnp.maximum(log_lb, other) + jnp.log(1.0 + jnp.exp(-jnp.abs(log_lb - other)))
        kk = (1.0 - lb) * sig_neg
    row = lax.broadcasted_iota(jnp.int32, (t, HEAD_DIM), 0)
    in_chunk = jnp.bitwise_and(row, ch - 1)
    bcum = log_f
    s = 1
    while s < ch:
        bcum = bcum + jnp.where(in_chunk >= s, pltpu.roll(bcum, s, axis=0), 0.0)
        s *= 2
    b_ref[...] = bcum
    kk_ref[...] = kk
    qs_ref[...] = _silu(q_ref[0])

    ones = jnp.ones((HEAD_DIM, ch), MXU_DTYPE)
    lane = lax.broadcasted_iota(jnp.int32, (sb, ch), 1)
    subrow = lax.broadcasted_iota(jnp.int32, (sb, HEAD_DIM), 0)

    def chunk(c, st):
        r0 = pl.multiple_of(c * ch, ch)
        bq = b_ref[pl.ds(r0, ch), :]
        qc = qs_ref[pl.ds(r0, ch), :]
        kc = kk_ref[pl.ds(r0, ch), :]
        vc = v_ref[0, pl.ds(r0, ch), :]
        blast = bq[ch - 1:ch, :]
        o = _mm_nt(qc * jnp.exp(bq), st)
        a_rows = []
        diag = []
        for blk in range(nsb):
            lo = blk * sb
            b_i, q_i, k_i = bq[lo:lo + sb], qc[lo:lo + sb], kc[lo:lo + sb]
            for s_ in range(sb):
                dec = jnp.where(subrow >= s_, jnp.exp(b_i - b_i[s_:s_ + 1]), 0.0)
                diag.append(q_i * (k_i[s_:s_ + 1] * dec))
            if blk == 0:
                a_rows.append(jnp.zeros((sb, ch), F32))
            else:
                m_i = bq[lo - 1:lo]
                qd = q_i * jnp.exp(b_i - m_i)
                kd = kc[0:lo] * jnp.exp(m_i - bq[0:lo])
                kd = jnp.concatenate([kd, jnp.zeros((ch - lo, HEAD_DIM), F32)], axis=0)
                a_rows.append(_mm_nt(qd, kd))
        dsum = _mm(jnp.concatenate(diag, axis=0), ones)
        for blk in range(nsb):
            acc = a_rows[blk]
            for s_ in range(sb):
                idx = blk * sb + s_
                acc = acc + jnp.where(lane == idx, dsum[idx * sb:(idx + 1) * sb], 0.0)
            a_rows[blk] = acc
        a_mat = jnp.concatenate(a_rows, axis=0)
        oo_ref[pl.ds(r0, ch), :] = o + _mm(a_mat, vc)
        kdec = kc * jnp.exp(blast - bq)
        return st * jnp.exp(blast) + _mm_tn(vc, kdec)

    lax.fori_loop(0, t // ch, chunk, jnp.zeros((HEAD_DIM, HEAD_DIM), F32), unroll=8)
    o = oo_ref[...]
    o = o * lax.rsqrt(jnp.mean(o * o, axis=-1, keepdims=True) + RMS_EPS) * ng_ref[0]
    o_ref[0] = (o * _silu(g_ref[0])).astype(o_ref.dtype)


def _hgrn2(proj3, col0, lower_bounds, norm_g, layer, d_hg):
    b, t, _ = proj3.shape
    depth = lower_bounds.shape[0]
    heads = d_hg // HEAD_DIM
    c0 = col0 // HEAD_DIM

    def colspec(k):
        return pl.BlockSpec((1, t, HEAD_DIM), lambda i, h: (i, 0, c0 + k * heads + h))

    return pl.pallas_call(
        functools.partial(_hgrn2_kernel, layer=layer),
        grid=(b, heads),
        in_specs=[colspec(0), colspec(1), colspec(2), colspec(3),
                  pl.BlockSpec((depth, HEAD_DIM), lambda i, h: (0, h)),
                  pl.BlockSpec((1, 1, HEAD_DIM), lambda i, h: (layer, 0, 0))],
        out_specs=pl.BlockSpec((1, t, HEAD_DIM), lambda i, h: (i, 0, h)),
        out_shape=jax.ShapeDtypeStruct((b, t, d_hg), MXU_DTYPE),
        scratch_shapes=[pltpu.VMEM((t, HEAD_DIM), F32)] * 4,
        compiler_params=_params("parallel", "parallel"),
        name="hgrn2",
    )(proj3, proj3, proj3, proj3, lower_bounds, norm_g.reshape(depth, 1, HEAD_DIM))


def _compress_kernel(*refs):
    x_refs = refs[:2 * NSA_KV]
    pek_ref, pev_ref, w1k_ref, w2k_ref, w1v_ref, w2v_ref, ko_ref, vo_ref = refs[2 * NSA_KV:]
    nrow = x_refs[0].shape[1] // CMP_STRIDE
    row = lax.broadcasted_iota(jnp.int32, (nrow, HEAD_DIM), 0)
    branches = ((pek_ref, w1k_ref, w2k_ref, ko_ref), (pev_ref, w1v_ref, w2v_ref, vo_ref))
    for kv, (pe_ref, w1_ref, w2_ref, out_ref) in enumerate(branches):
        for g in range(NSA_KV):
            x_ref = x_refs[kv * NSA_KV + g]
            lo = jnp.concatenate(
                [x_ref[0, pl.ds(i, nrow, stride=CMP_STRIDE), :] for i in range(CMP_STRIDE)], axis=1)
            hi = pltpu.roll(lo, nrow - 1, axis=0)
            blk = jnp.concatenate([lo, hi], axis=1) + pe_ref[0]
            hid = _silu(_mm(blk, w1_ref[0]))
            out = _mm(hid, w2_ref[0])
            out_ref[0, g] = jnp.where(row < nrow - 1, out, 0.0)


def _compress(proj3, col, pe_k, pe_v, w1_k, w2_k, w1_v, w2_v, layer):
    b, t, _ = proj3.shape
    nrow = t // CMP_STRIDE
    nx = 2 * NSA_KV
    assert col % HEAD_DIM == 0
    xspec = lambda k: pl.BlockSpec((1, t, HEAD_DIM), lambda i: (i, 0, col // HEAD_DIM + k))
    depth = pe_k.shape[0]
    hidden = w1_k.shape[2]
    flat = CMP_LEN * HEAD_DIM
    pespec = pl.BlockSpec((1, 1, flat), lambda i: (layer, 0, 0))
    w1spec = pl.BlockSpec((1, flat, hidden), lambda i: (layer, 0, 0))
    w2spec = pl.BlockSpec((1, hidden, HEAD_DIM), lambda i: (layer, 0, 0))
    ospec = pl.BlockSpec((1, NSA_KV, nrow, HEAD_DIM), lambda i: (i, 0, 0, 0))
    oshape = jax.ShapeDtypeStruct((b, NSA_KV, nrow, HEAD_DIM), F32)
    return pl.pallas_call(
        _compress_kernel,
        grid=(b,),
        in_specs=[xspec(k) for k in range(nx)] + [pespec, pespec, w1spec, w2spec, w1spec, w2spec],
        out_specs=[ospec, ospec],
        out_shape=[oshape, oshape],
        compiler_params=_params("parallel"),
        name="nsa_compress",
    )(*([proj3] * nx), pe_k.reshape(depth, 1, flat), pe_v.reshape(depth, 1, flat), w1_k, w2_k, w1_v, w2_v)


def _nsa_kernel(slt_ref, kaux_ref, kauxc_ref, q_ref, gl_ref, sg_ref, ks_ref, vs_ref, kw_ref, vw_ref,
                kc_ref, vc_ref, ov_ref, ge_ref, o_ref,
                ksx_ref, kwx_ref, vsb_ref, vwb_ref, kcx_ref, vcb_ref):
    qb = pl.program_id(1)
    q0 = qb * Q_BLOCK
    q0a = pl.multiple_of(q0, Q_BLOCK)
    nq = NSA_GROUP * Q_BLOCK
    gw = NSA_GROUP * HEAD_DIM
    t_all = ks_ref.shape[1]
    n_sel = t_all // SEL_LEN
    kchunk = 512
    groups = range(NSA_KV)

    @pl.when(qb == 0)
    def _():
        for g in groups:
            cols = slice(g * HEAD_DIM, (g + 1) * HEAD_DIM)
            ksx_ref[g, :, 0:HEAD_DIM] = ks_ref[0, :, cols].astype(MXU_DTYPE)
            ksx_ref[g, :, HEAD_DIM:] = kaux_ref[...]
            kwx_ref[g, :, 0:HEAD_DIM] = kw_ref[0, :, cols].astype(MXU_DTYPE)
            kwx_ref[g, :, HEAD_DIM:] = kaux_ref[...]
            kcx_ref[g, :, 0:HEAD_DIM] = kc_ref[0, g].astype(MXU_DTYPE)
            kcx_ref[g, :, HEAD_DIM:] = kauxc_ref[...]
            vsb_ref[g] = vs_ref[0, :, cols].astype(MXU_DTYPE)
            vwb_ref[g] = vw_ref[0, :, cols].astype(MXU_DTYPE)
            vcb_ref[g] = vc_ref[0, g].astype(MXU_DTYPE)

    def stack(v):
        return jnp.concatenate([v] * NSA_GROUP, axis=0)

    def online(carry, s, v):
        m, l, acc = carry
        m_new = jnp.maximum(m, jnp.max(s, axis=1, keepdims=True))
        alpha = jnp.exp2(m - m_new)
        p = jnp.exp2(s - m_new)
        l = alpha * l + jnp.sum(p, axis=1, keepdims=True)
        acc = alpha * acc + jnp.dot(p.astype(MXU_DTYPE), v, preferred_element_type=F32)
        return m_new, l, acc

    def finish(carry):
        _, l, acc = carry
        return acc * jnp.where(l > 0.0, 1.0 / l, 0.0)

    init = (jnp.full((nq, 1), NEG_INF, F32), jnp.zeros((nq, 1), F32), jnp.zeros((nq, HEAD_DIM), F32))
    t_loc = lax.broadcasted_iota(jnp.int32, (Q_BLOCK, LANES), 0)
    lane = lax.broadcasted_iota(jnp.int32, (Q_BLOCK, LANES), 1)

    ncmp = kcx_ref.shape[1]
    pos_c = (CMP_LEN - 1) + CMP_STRIDE * lax.broadcasted_iota(jnp.int32, (Q_BLOCK, ncmp), 1)
    ok_c = stack(jnp.where(q0 + lax.broadcasted_iota(jnp.int32, (Q_BLOCK, ncmp), 0) >= pos_c, 1.0, 0.0)) > 0.5
    tri_low = stack(jnp.where(lax.broadcasted_iota(jnp.int32, (Q_BLOCK, Q_BLOCK), 1)
                              <= lax.broadcasted_iota(jnp.int32, (Q_BLOCK, Q_BLOCK), 0), 1.0, 0.0)) > 0.5
    n_full = WINDOW // SEL_LEN - 1
    win_bias = jnp.where(lane < n_sel,
                         jnp.where(lane >= qb - n_full, jnp.where(lane < qb, 0.0, MASK_VAL), MASK_VAL), 0.0)
    has_far = jnp.where(q0 >= WINDOW, 1.0, 0.0)
    ok_edge = stack(jnp.where(lane < Q_BLOCK,
                              jnp.where(lane > t_loc, has_far, 0.0),
                              jnp.where(lane - Q_BLOCK <= t_loc, 1.0, 0.0))
                    ) > 0.5
    w0 = pl.multiple_of(jnp.maximum(q0 - WINDOW, 0), Q_BLOCK)
    nblk = lax.broadcasted_iota(jnp.int32, (n_sel, Q_BLOCK), 0)
    forced = (nblk == 0) | (nblk == qb) | (nblk == qb - 1)

    q_plain, q_sel, o_cmp, o_win = [], [], [], []
    for g in groups:
        q = q_ref[0, :, g * gw:(g + 1) * gw] * (HEAD_DIM ** -0.5 * LOG2E)
        q_main = jnp.concatenate([q[:, r * HEAD_DIM:(r + 1) * HEAD_DIM] for r in range(NSA_GROUP)],
                                 axis=0).astype(MXU_DTYPE)
        slope_lanes = slt_ref[g]

        def with_aux(block_bias, q_main=q_main, slope_lanes=slope_lanes):
            return jnp.concatenate([q_main, (stack(block_bias) + slope_lanes).astype(MXU_DTYPE)], axis=1)

        qp = with_aux(jnp.zeros((Q_BLOCK, LANES), F32))
        q_plain.append(qp)

        carry = online(init, _mm_nt(with_aux(win_bias), kwx_ref[g, pl.ds(w0, WINDOW), :]),
                       vwb_ref[g, pl.ds(w0, WINDOW), :])
        k_edge = jnp.concatenate([kwx_ref[g, pl.ds(w0, Q_BLOCK), :], kwx_ref[g, pl.ds(q0a, Q_BLOCK), :]], axis=0)
        v_edge = jnp.concatenate([vwb_ref[g, pl.ds(w0, Q_BLOCK), :], vwb_ref[g, pl.ds(q0a, Q_BLOCK), :]], axis=0)
        s_edge = jnp.where(ok_edge, _mm_nt(qp, k_edge), MASK_VAL)
        o_win.append(finish(online(carry, s_edge, v_edge)))

        s_c = jnp.where(ok_c, _mm_nt(qp, kcx_ref[g]), MASK_VAL)
        p_c = jnp.exp2(s_c - jnp.maximum(jnp.max(s_c, axis=1, keepdims=True), NEG_INF))
        l_c = jnp.sum(p_c, axis=1, keepdims=True)
        p_c = p_c * jnp.where(l_c > 0.0, 1.0 / l_c, 0.0)
        o_cmp.append(jnp.dot(p_c.astype(MXU_DTYPE), vcb_ref[g], preferred_element_type=F32))

        p_sum = p_c[0:Q_BLOCK]
        for r in range(1, NSA_GROUP):
            p_sum = p_sum + p_c[r * Q_BLOCK:(r + 1) * Q_BLOCK]
        imp = lax.dot_general(ov_ref[...], p_sum, (((1,), (1,)), ((), ())),
                              precision=lax.Precision.HIGHEST, preferred_element_type=F32)[0:n_sel]
        imp = jnp.where(nblk > qb, NEG_INF, jnp.where(forced, FORCE_SCORE, imp))
        rank = jnp.zeros(imp.shape, F32)
        for m_ in range(n_sel):
            other = imp[m_:m_ + 1, :]
            rank = rank + jnp.where(nblk > m_, jnp.where(other >= imp, 1.0, 0.0), jnp.where(other > imp, 1.0, 0.0))
        sel_bias = jnp.where(rank < float(min(SEL_TOPN, n_sel)), jnp.where(nblk < qb, 0.0, MASK_VAL), MASK_VAL)
        sel_bias = jnp.concatenate([sel_bias, jnp.zeros((LANES - n_sel, Q_BLOCK), F32)], axis=0).T
        q_sel.append(with_aux(sel_bias))

    def sel_branch(n_chunks):
        def run():
            outs = []
            for g in groups:
                carry = init
                for ci in range(n_chunks):
                    rows = slice(ci * kchunk, (ci + 1) * kchunk)
                    carry = online(carry, _mm_nt(q_sel[g], ksx_ref[g, rows, :]), vsb_ref[g, rows, :])
                s_own = jnp.where(tri_low, _mm_nt(q_plain[g], ksx_ref[g, pl.ds(q0a, Q_BLOCK), :]), MASK_VAL)
                outs.append(finish(online(carry, s_own, vsb_ref[g, pl.ds(q0a, Q_BLOCK), :])))
            return tuple(outs)
        return run

    max_chunks = (t_all - Q_BLOCK + kchunk - 1) // kchunk
    o_sel = lax.switch((q0 + kchunk - 1) // kchunk, [sel_branch(n) for n in range(max_chunks + 1)])

    def unstack(o):
        return jnp.concatenate([o[r * Q_BLOCK:(r + 1) * Q_BLOCK] for r in range(NSA_GROUP)], axis=1)

    sig = _sigmoid(gl_ref[0])
    sig_hi = sig.astype(MXU_DTYPE)
    sig_lo = (sig - sig_hi.astype(F32)).astype(MXU_DTYPE)
    for g in groups:
        out = None
        for i, branch in enumerate((o_cmp[g], o_sel[g], o_win[g])):
            gate = (jnp.dot(sig_hi, ge_ref[g, i], preferred_element_type=F32)
                    + jnp.dot(sig_lo, ge_ref[g, i], preferred_element_type=F32))
            term = gate * unstack(branch)
            out = term if out is None else out + term
        cols = slice(g * gw, (g + 1) * gw)
        o_ref[0, :, cols] = (out * _silu(sg_ref[0, :, cols])).astype(o_ref.dtype)


def _split3(v):
    bf = jnp.bfloat16
    hi = v.astype(bf).astype(np.float64)
    mid = (v - hi).astype(bf).astype(np.float64)
    lo = (v - hi - mid).astype(bf).astype(np.float64)
    return hi, mid, lo


def _slope_lanes():
    heads = NSA_KV * NSA_GROUP
    h = np.arange(1, heads + 1, dtype=np.float32)
    s = np.power(np.float32(2.0), -8.0 * h / heads).astype(np.float32).astype(np.float64) * LOG2E
    parts = _split3(s)
    out = np.zeros((heads, LANES), np.float32)
    for rep in range(2):
        for i, part in enumerate(parts):
            out[:, POS_LANE + 3 * rep + i] = part
    out = np.repeat(out.reshape(NSA_KV, NSA_GROUP, 1, LANES), Q_BLOCK, axis=2)
    return jnp.asarray(out.reshape(NSA_KV, NSA_GROUP * Q_BLOCK, LANES))


def _key_aux(pos, blocks):
    out = np.zeros((pos.shape[0], LANES), np.float32)
    if blocks:
        out[np.arange(pos.shape[0]), pos // SEL_LEN] = 1.0
    out[:, POS_LANE:POS_LANE + 3] = (SEL_LEN * (pos // SEL_LEN))[:, None]
    out[:, POS_LANE + 3:POS_LANE + 6] = (pos % SEL_LEN)[:, None]
    return jnp.asarray(out, dtype=MXU_DTYPE)


def _overlap_t(t, ncmp_rows):
    n_cmp = (t - CMP_LEN) // CMP_STRIDE + 1
    n_sel = t // SEL_LEN
    s_c = np.arange(n_cmp) * CMP_STRIDE
    s_s = np.arange(n_sel) * SEL_LEN
    ov = np.clip(np.minimum(s_c[:, None] + CMP_LEN, s_s[None, :] + SEL_LEN)
                 - np.maximum(s_c[:, None], s_s[None, :]), 0, None).astype(np.float32) / CMP_LEN
    out = np.zeros((LANES, ncmp_rows), np.float32)
    out[:n_sel, :n_cmp] = ov.T
    return jnp.asarray(out)


def _gate_expand():
    width = NSA_GROUP * HEAD_DIM
    e = np.zeros((NSA_KV, 3, LANES, width), np.float32)
    for g in range(NSA_KV):
        for r in range(NSA_GROUP):
            for i in range(3):
                e[g, i, (g * NSA_GROUP + r) * 3 + i, r * HEAD_DIM:(r + 1) * HEAD_DIM] = 1.0
    return jnp.asarray(e, dtype=MXU_DTYPE)


def _nsa(proj3, cols, k_cmp, v_cmp):
    b, t, _ = proj3.shape
    assert t // SEL_LEN <= POS_LANE and t % 512 == 0 and Q_BLOCK == SEL_LEN
    gw = NSA_GROUP * HEAD_DIM
    width = NSA_KV * gw
    kvw = NSA_KV * HEAD_DIM
    ncmp_rows = k_cmp.shape[2]
    q_c, ks_c, kw_c, gl_c, sg_c = cols
    assert q_c % width == 0 and sg_c % width == 0 and ks_c % kvw == 0 and kw_c % kvw == 0

    def kvspec(c, is_v):
        return pl.BlockSpec((1, t, kvw), lambda i, j: (i, 0, c // kvw + (1 if is_v else 0)))

    def whole(*shape):
        return pl.BlockSpec(shape, lambda i, j: (0,) * len(shape))

    cmpspec = pl.BlockSpec((1, NSA_KV, ncmp_rows, HEAD_DIM), lambda i, j: (i, 0, 0, 0))
    cmp_pos = (CMP_LEN - 1) + CMP_STRIDE * np.arange(ncmp_rows)
    return pl.pallas_call(
        _nsa_kernel,
        grid=(b, t // Q_BLOCK),
        in_specs=[whole(NSA_KV, NSA_GROUP * Q_BLOCK, LANES), whole(t, LANES), whole(ncmp_rows, LANES),
                  pl.BlockSpec((1, Q_BLOCK, width), lambda i, j: (i, j, q_c // width)),
                  pl.BlockSpec((1, Q_BLOCK, LANES), lambda i, j: (i, j, gl_c // LANES)),
                  pl.BlockSpec((1, Q_BLOCK, width), lambda i, j: (i, j, sg_c // width)),
                  kvspec(ks_c, False), kvspec(ks_c, True), kvspec(kw_c, False), kvspec(kw_c, True),
                  cmpspec, cmpspec, whole(LANES, ncmp_rows), whole(NSA_KV, 3, LANES, gw)],
        out_specs=pl.BlockSpec((1, Q_BLOCK, width), lambda i, j: (i, j, 0)),
        out_shape=jax.ShapeDtypeStruct((b, t, width), MXU_DTYPE),
        scratch_shapes=[pltpu.VMEM((NSA_KV, t, 2 * HEAD_DIM), MXU_DTYPE),
                        pltpu.VMEM((NSA_KV, t, 2 * HEAD_DIM), MXU_DTYPE),
                        pltpu.VMEM((NSA_KV, t, HEAD_DIM), MXU_DTYPE), pltpu.VMEM((NSA_KV, t, HEAD_DIM), MXU_DTYPE),
                        pltpu.VMEM((NSA_KV, ncmp_rows, 2 * HEAD_DIM), MXU_DTYPE),
                        pltpu.VMEM((NSA_KV, ncmp_rows, HEAD_DIM), MXU_DTYPE)],
        compiler_params=_params("parallel", "arbitrary"),
        name="nsa_attn",
    )(_slope_lanes(), _key_aux(np.arange(t), True), _key_aux(cmp_pos, False),
      proj3, proj3, proj3, proj3, proj3, proj3, proj3, k_cmp, v_cmp, _overlap_t(t, ncmp_rows), _gate_expand())


def _out_ln_kernel(a0_ref, a1_ref, a2_ref, a3_ref, w0_ref, w1_ref, w2_ref, w3_ref,
                   x_ref, gate_ref, lng_ref, lnb_ref, shift_ref, scale_ref, *rest, alpha, emit_u):
    if emit_u:
        o_ref, u_ref, pre_ref, s1_ref, s2_ref = rest
    else:
        (o_ref, pre_ref, s1_ref, s2_ref), u_ref = rest, None
    i, j = pl.program_id(0), pl.program_id(1)
    n_row, nj = pl.num_programs(0) - 1, pl.num_programs(1)
    tn = o_ref.shape[1]
    slot = lax.rem(i, 2)

    @pl.when((i == 0) & (j == 0))
    def _():
        s1_ref[...] = jnp.zeros(s1_ref.shape, F32)
        s2_ref[...] = jnp.zeros(s2_ref.shape, F32)

    def build():
        y = (jnp.dot(a0_ref[...], w0_ref[0], preferred_element_type=F32)
             + jnp.dot(a1_ref[...], w1_ref[0], preferred_element_type=F32)
             + jnp.dot(a2_ref[...], w2_ref[0], preferred_element_type=F32)
             + jnp.dot(a3_ref[...], w3_ref[0], preferred_element_type=F32))
        pre = alpha * x_ref[...] + (1.0 + gate_ref[0, 0]) * y
        pre_ref[slot * nj + j] = pre
        keep = jnp.where(j == 0, 0.0, 1.0)
        s1_ref[slot] = keep * s1_ref[slot] + pre.sum(axis=-1, keepdims=True)
        s2_ref[slot] = keep * s2_ref[slot] + (pre * pre).sum(axis=-1, keepdims=True)

    def normalise():
        d = nj * tn
        prev = 1 - slot
        mu = s1_ref[prev] / d
        inv = lax.rsqrt(s2_ref[prev] / d - mu * mu + LN_EPS)
        out = (pre_ref[prev * nj + j] - mu) * inv * lng_ref[0] + lnb_ref[0]
        o_ref[...] = out
        if emit_u:
            u_ref[...] = (out * (1.0 + scale_ref[0, 0]) + shift_ref[0, 0]).astype(u_ref.dtype)

    pl.when(i == 0)(build)

    @pl.when((i > 0) & (i < n_row))
    def _():
        normalise()
        build()

    pl.when(i == n_row)(normalise)


def _out_ln(y_rg, y_nsa, y_hg, w_out_b, x2, mod4, ln_g, ln_b, layer, alpha, t, emit_u):
    m, d = x2.shape
    depth = w_out_b.shape[0]
    kb = y_rg.shape[1]
    assert y_nsa.shape[1] == 2 * kb and y_hg.shape[1] == kb and d == 4 * kb
    tm, tn = 512, 512
    nj, n_row = d // tn, m // tm
    assert t % tm == 0
    mrow = lambda i: jnp.minimum(i, n_row - 1)
    mcol = lambda i, j: jnp.where(i < n_row, j, nj - 1)
    orow = lambda i: jnp.maximum(i - 1, 0)
    ocol = lambda i, j: jnp.where(i > 0, j, 0)
    aspec = lambda c: pl.BlockSpec((tm, kb), lambda i, j: (mrow(i), c))
    wspec = lambda r: pl.BlockSpec((1, kb, tn), lambda i, j: (layer, r, mcol(i, j)))
    vspec = pl.BlockSpec((1, 1, tn), lambda i, j: (layer, 0, ocol(i, j)))
    nxt = min(layer + 1, depth - 1)
    modspec = lambda part: pl.BlockSpec(
        (1, 1, 1, tn), lambda i, j: (nxt, (orow(i) * tm) // t, 0, part * nj + ocol(i, j)))
    ospec = pl.BlockSpec((tm, tn), lambda i, j: (orow(i), ocol(i, j)))
    oshape = jax.ShapeDtypeStruct((m, d), F32)
    return pl.pallas_call(
        functools.partial(_out_ln_kernel, alpha=alpha, emit_u=emit_u),
        grid=(n_row + 1, nj),
        in_specs=[aspec(0), aspec(0), aspec(1), aspec(0), wspec(0), wspec(1), wspec(2), wspec(3),
                  pl.BlockSpec((tm, tn), lambda i, j: (mrow(i), mcol(i, j))),
                  pl.BlockSpec((1, 1, 1, tn), lambda i, j: (layer, (mrow(i) * tm) // t, 0, 2 * nj + mcol(i, j))),
                  vspec, vspec, modspec(0), modspec(1)],
        out_specs=[ospec, ospec] if emit_u else ospec,
        out_shape=[oshape, jax.ShapeDtypeStruct((m, d), MXU_DTYPE)] if emit_u else oshape,
        scratch_shapes=[pltpu.VMEM((2 * nj, tm, tn), F32), pltpu.VMEM((2, tm, 1), F32), pltpu.VMEM((2, tm, 1), F32)],
        compiler_params=_params("arbitrary", "arbitrary"),
        name="out_ln",
    )(y_rg, y_nsa, y_nsa, y_hg, w_out_b, w_out_b, w_out_b, w_out_b, x2, mod4,
      ln_g.reshape(depth, 1, d), ln_b.reshape(depth, 1, d), mod4, mod4)


def kernel(x, c, w_ada, b_ada, w_in, rg_conv_w, rg_conv_b, rg_w_a, rg_b_a, rg_w_x, rg_b_x, rg_lambda,
           nsa_pe_k, nsa_pe_v, nsa_cmp_w1_k, nsa_cmp_w2_k, nsa_cmp_w1_v, nsa_cmp_w2_v,
           hg_lower_bounds, hg_norm_g, w_out, ln_g, ln_b):
    b, t, d = x.shape
    depth = w_ada.shape[0]
    m = b * t
    d_rg, d_nsa, d_hg = d // 4, d // 2, d // 4
    kv_cols = 2 * NSA_KV * HEAD_DIM
    n_gl = 3 * NSA_KV * NSA_GROUP
    alpha = (2.0 * depth) ** 0.25
    assert d_nsa == NSA_KV * NSA_GROUP * HEAD_DIM and b <= SUBLANES

    gl_pad = 512
    c_q = 2 * d_rg
    c_kvc = c_q + d_nsa
    c_kvs = c_kvc + kv_cols
    c_kvw = c_kvs + kv_cols
    c_gl = c_kvw + kv_cols
    c_sg = c_gl + gl_pad
    c_hg = c_sg + d_nsa
    w_in_p = _prep_w_in(jnp.swapaxes(w_in, 1, 2), c_gl, n_gl, gl_pad)
    w_out_b = w_out.astype(MXU_DTYPE)
    w1_k, w1_v = nsa_cmp_w1_k.astype(MXU_DTYPE), nsa_cmp_w1_v.astype(MXU_DTYPE)

    c_pad = jnp.pad(c, ((0, SUBLANES - b), (0, 0)))
    mod4 = _ada(c_pad, w_ada, b_ada).reshape(depth, SUBLANES, 1, 3 * d)

    u = _modulate(x, mod4, 0).reshape(m, d)
    x = x.reshape(m, d)
    for layer in range(depth):
        proj3 = _in_proj(u, w_in_p, layer).reshape(b, t, -1)
        y_rg = _rglru(proj3, rg_conv_w, rg_conv_b, rg_w_a, rg_b_a, rg_w_x, rg_b_x, rg_lambda, layer, d_rg)
        k_cmp, v_cmp = _compress(proj3, c_kvc, nsa_pe_k, nsa_pe_v, w1_k, nsa_cmp_w2_k, w1_v, nsa_cmp_w2_v, layer)
        y_nsa = _nsa(proj3, (c_q, c_kvs, c_kvw, c_gl, c_sg), k_cmp, v_cmp)
        y_hg = _hgrn2(proj3, c_hg, hg_lower_bounds, hg_norm_g, layer, d_hg)
        emit_u = layer + 1 < depth
        res = _out_ln(y_rg.reshape(m, d_rg), y_nsa.reshape(m, d_nsa), y_hg.reshape(m, d_hg), w_out_b,
                      x, mod4, ln_g, ln_b, layer, alpha, t, emit_u)
        x, u = res if emit_u else (res, None)
    return x.reshape(b, t, d)
```

```python
import functools

import numpy as np
import jax
import jax.numpy as jnp
from jax import lax
from jax.experimental import pallas as pl
from jax.experimental.pallas import tpu as pltpu

F32 = jnp.float32
MXU_DTYPE = jnp.bfloat16

HEAD_DIM = 128
RG_CONV = 4
RG_C = 8.0
NSA_KV = 2
NSA_GROUP = 8
CMP_LEN = 32
CMP_STRIDE = 16
SEL_LEN = 64
SEL_TOPN = 16
WINDOW = 512
Q_BLOCK = 64
HG_CHUNK = 64
HG_SUB = 8
LN_EPS = 1e-5
RMS_EPS = 1e-6
NEG_INF = -1e30
FORCE_SCORE = 1e9
MASK_VAL = -(2.0 ** 100)
LOG2E = 1.4426950408889634
POS_LANE = 32

LANES = 128
SUBLANES = 8
VMEM_LIMIT = 56 * 1024 * 1024


def _mm(a, b):
    return jnp.dot(a.astype(MXU_DTYPE), b.astype(MXU_DTYPE), preferred_element_type=F32)


def _mm_nt(a, b):
    return lax.dot_general(a.astype(MXU_DTYPE), b.astype(MXU_DTYPE),
                           (((1,), (1,)), ((), ())), preferred_element_type=F32)


def _mm_tn(a, b):
    return lax.dot_general(a.astype(MXU_DTYPE), b.astype(MXU_DTYPE),
                           (((0,), (0,)), ((), ())), preferred_element_type=F32)


def _sigmoid(v):
    return jax.nn.sigmoid(v)


def _silu(v):
    return v * jax.nn.sigmoid(v)


def _params(*semantics):
    return pltpu.CompilerParams(dimension_semantics=semantics, vmem_limit_bytes=VMEM_LIMIT)


def _ada_kernel(c_ref, w_ref, b_ref, o_ref):
    o_ref[0] = _mm(c_ref[...], w_ref[0]) + b_ref[0]


def _ada(c_pad, w_ada, b_ada):
    depth, d, n3 = w_ada.shape
    rows = c_pad.shape[0]
    tn = 1024
    return pl.pallas_call(
        _ada_kernel,
        grid=(depth, n3 // tn),
        in_specs=[pl.BlockSpec((rows, d), lambda l, j: (0, 0)),
                  pl.BlockSpec((1, d, tn), lambda l, j: (l, 0, j)),
                  pl.BlockSpec((1, 1, tn), lambda l, j: (l, 0, j))],
        out_specs=pl.BlockSpec((1, rows, tn), lambda l, j: (l, 0, j)),
        out_shape=jax.ShapeDtypeStruct((depth, rows, n3), F32),
        compiler_params=_params("parallel", "parallel"),
        name="ada",
    )(c_pad, w_ada, b_ada.reshape(depth, 1, n3))


def _modulate_kernel(x_ref, shift_ref, scale_ref, o_ref):
    o_ref[0] = (x_ref[0] * (1.0 + scale_ref[0, 0]) + shift_ref[0, 0]).astype(o_ref.dtype)


def _modulate(x, mod4, layer):
    b, t, d = x.shape
    tt = 256
    return pl.pallas_call(
        _modulate_kernel,
        grid=(b, t // tt),
        in_specs=[pl.BlockSpec((1, tt, d), lambda i, j: (i, j, 0)),
                  pl.BlockSpec((1, 1, 1, d), lambda i, j: (layer, i, 0, 0)),
                  pl.BlockSpec((1, 1, 1, d), lambda i, j: (layer, i, 0, 1))],
        out_specs=pl.BlockSpec((1, tt, d), lambda i, j: (i, j, 0)),
        out_shape=jax.ShapeDtypeStruct((b, t, d), MXU_DTYPE),
        compiler_params=_params("parallel", "parallel"),
        name="modulate",
    )(x, mod4, mod4)


def _prep_w_in_kernel(w_ref, o_ref, prev_ref, *, gl_blk, n_gl):
    j = pl.program_id(1)
    tn = w_ref.shape[1]

    @pl.when(j < gl_blk)
    def _():
        o_ref[0] = w_ref[0].T.astype(o_ref.dtype)

    @pl.when(j == gl_blk)
    def _():
        row = lax.broadcasted_iota(jnp.int32, w_ref.shape[1:], 0)
        o_ref[0] = jnp.where(row < n_gl, w_ref[0], 0.0).T.astype(o_ref.dtype)

    @pl.when(j > gl_blk)
    def _():
        tile = jnp.concatenate([prev_ref[n_gl:tn, :], w_ref[0, 0:n_gl, :]], axis=0)
        o_ref[0] = tile.T.astype(o_ref.dtype)

    prev_ref[...] = w_ref[0]


def _prep_w_in(w_in_t, c_gl, n_gl, tn):
    depth, n, d = w_in_t.shape
    assert c_gl % tn == 0 and (n - n_gl) % tn == 0 and n_gl % SUBLANES == 0
    n_out = n - n_gl + tn
    return pl.pallas_call(
        functools.partial(_prep_w_in_kernel, gl_blk=c_gl // tn, n_gl=n_gl),
        grid=(depth, n_out // tn),
        in_specs=[pl.BlockSpec((1, tn, d), lambda l, j: (l, j, 0))],
        out_specs=pl.BlockSpec((1, d, tn), lambda l, j: (l, 0, j)),
        out_shape=jax.ShapeDtypeStruct((depth, d, n_out), MXU_DTYPE),
        scratch_shapes=[pltpu.VMEM((tn, d), F32)],
        compiler_params=_params("arbitrary", "arbitrary"),
        name="prep_w_in",
    )(w_in_t)


def _in_proj_kernel(a_ref, w_ref, o_ref):
    o_ref[...] = jnp.dot(a_ref[...], w_ref[0], preferred_element_type=F32)


def _in_proj(u, w_in_p, layer):
    m, d = u.shape
    n = w_in_p.shape[2]
    tm, tn = 1024, 1024
    return pl.pallas_call(
        _in_proj_kernel,
        grid=(m // tm, n // tn),
        in_specs=[pl.BlockSpec((tm, d), lambda i, j: (i, 0)),
                  pl.BlockSpec((1, d, tn), lambda i, j: (layer, 0, j))],
        out_specs=pl.BlockSpec((tm, tn), lambda i, j: (i, j)),
        out_shape=jax.ShapeDtypeStruct((m, n), F32),
        compiler_params=_params("parallel", "parallel"),
        name="in_proj",
    )(u, w_in_p)


def _rglru_kernel(x_ref, g_ref, cw_ref, cb_ref, wa_ref, ba_ref, wx_ref, bx_ref, lam_ref, o_ref,
                  xp_ref, a_ref, b_ref, h_ref):
    t, cb = x_ref.shape[1], x_ref.shape[2]
    pad = SUBLANES
    xp_ref[0:pad, :] = jnp.zeros((pad, cb), F32)
    xp_ref[pad:, :] = x_ref[0]
    xc = cb_ref[0]
    for j in range(RG_CONV):
        xc = xc + xp_ref[pl.ds(pad - (RG_CONV - 1) + j, t), :] * cw_ref[0, j:j + 1, :]

    row = lax.broadcasted_iota(jnp.int32, (t, HEAD_DIM), 0)
    sub = jnp.bitwise_and(row, SUBLANES - 1)
    for n in range(cb // HEAD_DIM):
        sl = slice(n * HEAD_DIM, (n + 1) * HEAD_DIM)
        xb = xc[:, sl]
        r = _sigmoid(_mm(xb, wa_ref[0, n]) + ba_ref[0, :, sl])
        i = _sigmoid(_mm(xb, wx_ref[0, n]) + bx_ref[0, :, sl])
        neg_lam = -lam_ref[0, :, sl]
        softplus = jnp.maximum(neg_lam, 0.0) + jnp.log1p(jnp.exp(-jnp.abs(neg_lam)))
        log_a = (-RG_C * softplus) * r
        a = jnp.exp(log_a)
        mult = jnp.sqrt(-jnp.tanh(log_a) * (a * a + 1.0))
        mult = jnp.where(row == 0, 1.0, mult)
        bx = mult * (i * xb)
        s = 1
        while s < SUBLANES:
            a_sh = pltpu.roll(a, s, axis=0)
            b_sh = pltpu.roll(bx, s, axis=0)
            inside = sub >= s
            bx = jnp.where(inside, a * b_sh + bx, bx)
            a = jnp.where(inside, a * a_sh, a)
            s *= 2
        a_ref[:, sl] = a
        b_ref[:, sl] = bx

    def carry_rows(v, h):
        r0 = pl.multiple_of(v * SUBLANES, SUBLANES)
        hh = a_ref[pl.ds(r0, SUBLANES), :] * h + b_ref[pl.ds(r0, SUBLANES), :]
        h_ref[pl.ds(r0, SUBLANES), :] = hh
        return jnp.broadcast_to(hh[SUBLANES - 1:SUBLANES, :], hh.shape)

    lax.fori_loop(0, t // SUBLANES, carry_rows, jnp.zeros((SUBLANES, cb), F32))
    o_ref[0] = (h_ref[...] * _silu(g_ref[0])).astype(o_ref.dtype)


def _rglru(proj3, conv_w, conv_b, w_a, b_a, w_x, b_x, lam, layer, d_rg):
    b, t, _ = proj3.shape
    depth = conv_w.shape[0]
    cb = 256
    nblk = cb // HEAD_DIM
    ncb = d_rg // cb
    vec = lambda v: v.reshape(depth, 1, d_rg)
    vspec = pl.BlockSpec((1, 1, cb), lambda i, j: (layer, 0, j))
    wspec = pl.BlockSpec((1, nblk, HEAD_DIM, HEAD_DIM), lambda i, j: (layer, j, 0, 0))
    return pl.pallas_call(
        _rglru_kernel,
        grid=(b, ncb),
        in_specs=[pl.BlockSpec((1, t, cb), lambda i, j: (i, 0, j)),
                  pl.BlockSpec((1, t, cb), lambda i, j: (i, 0, ncb + j)),
                  pl.BlockSpec((1, RG_CONV, cb), lambda i, j: (layer, 0, j)),
                  vspec, wspec, vspec, wspec, vspec, vspec],
        out_specs=pl.BlockSpec((1, t, cb), lambda i, j: (i, 0, j)),
        out_shape=jax.ShapeDtypeStruct((b, t, d_rg), MXU_DTYPE),
        scratch_shapes=[pltpu.VMEM((t + SUBLANES, cb), F32), pltpu.VMEM((t, cb), F32),
                        pltpu.VMEM((t, cb), F32), pltpu.VMEM((t, cb), F32)],
        compiler_params=_params("parallel", "parallel"),
        name="rglru",
    )(proj3, proj3, conv_w, vec(conv_b), w_a, vec(b_a), w_x, vec(b_x), vec(lam))


def _hgrn2_kernel(q_ref, f_ref, v_ref, g_ref, lbr_ref, ng_ref, o_ref,
                  qs_ref, kk_ref, b_ref, oo_ref, *, layer):
    t = q_ref.shape[1]
    ch, sb = HG_CHUNK, HG_SUB
    nsb = ch // sb
    z = f_ref[0]
    ez = jnp.exp(-jnp.abs(z))
    log_sig = jnp.minimum(z, 0.0) - jnp.log(1.0 + ez)
    inv = 1.0 / (1.0 + ez)
    sig_neg = jnp.where(z >= 0.0, ez * inv, inv)
    if layer == 0:
        log_f = log_sig
        kk = sig_neg
    else:
        raw = lbr_ref[...]
        e = jnp.exp(raw - jnp.max(raw, axis=0, keepdims=True))
        p = e / jnp.sum(e, axis=0, keepdims=True)
        lb = p[1:2]
        for j in range(2, layer + 1):
            lb = lb + p[j:j + 1]
        log_lb = jnp.log(lb)
        other = jnp.log1p(-lb) + log_sig
        log_f = jnp.maximum(log_lb, other) + jnp.log(1.0 + jnp.exp(-jnp.abs(log_lb - other)))
        kk = (1.0 - lb) * sig_neg
    width = q_ref.shape[2]
    heads = range(width // HEAD_DIM)
    row = lax.broadcasted_iota(jnp.int32, (t, width), 0)
    in_chunk = jnp.bitwise_and(row, ch - 1)
    bcum = log_f
    s = 1
    while s < ch:
        bcum = bcum + jnp.where(in_chunk >= s, pltpu.roll(bcum, s, axis=0), 0.0)
        s *= 2
    b_ref[...] = bcum * LOG2E
    kk_ref[...] = kk
    qs_ref[...] = _silu(q_ref[0])

    ones = jnp.ones((HEAD_DIM, ch), MXU_DTYPE)
    lane = lax.broadcasted_iota(jnp.int32, (sb, ch), 1)
    subrow = lax.broadcasted_iota(jnp.int32, (sb, HEAD_DIM), 0)

    def chunk_head(r0, hd, st):
        cols = slice(hd * HEAD_DIM, (hd + 1) * HEAD_DIM)
        bq = b_ref[pl.ds(r0, ch), cols]
        qc = qs_ref[pl.ds(r0, ch), cols]
        kc = kk_ref[pl.ds(r0, ch), cols]
        vc = v_ref[0, pl.ds(r0, ch), cols]
        blast = bq[ch - 1:ch, :]
        o = _mm_nt(qc * jnp.exp2(bq), st)
        a_rows = []
        diag = []
        for blk in range(nsb):
            lo = blk * sb
            b_i, q_i, k_i = bq[lo:lo + sb], qc[lo:lo + sb], kc[lo:lo + sb]
            for s_ in range(sb):
                dec = jnp.where(subrow >= s_, jnp.exp2(b_i - b_i[s_:s_ + 1]), 0.0)
                diag.append(q_i * (k_i[s_:s_ + 1] * dec))
            if blk == 0:
                a_rows.append(jnp.zeros((sb, ch), F32))
            else:
                m_i = bq[lo - 1:lo]
                qd = q_i * jnp.exp2(b_i - m_i)
                kd = kc[0:lo] * jnp.exp2(m_i - bq[0:lo])
                kd = jnp.concatenate([kd, jnp.zeros((ch - lo, HEAD_DIM), F32)], axis=0)
                a_rows.append(_mm_nt(qd, kd))
        dsum = _mm(jnp.concatenate(diag, axis=0), ones)
        for blk in range(nsb):
            acc = a_rows[blk]
            for s_ in range(sb):
                idx = blk * sb + s_
                acc = acc + jnp.where(lane == idx, dsum[idx * sb:(idx + 1) * sb], 0.0)
            a_rows[blk] = acc
        a_mat = jnp.concatenate(a_rows, axis=0)
        oo_ref[pl.ds(r0, ch), cols] = o + _mm(a_mat, vc)
        kdec = kc * jnp.exp2(blast - bq)
        return st * jnp.exp2(blast) + _mm_tn(vc, kdec)

    def chunk(c, states):
        r0 = pl.multiple_of(c * ch, ch)
        return tuple(chunk_head(r0, hd, states[hd]) for hd in heads)

    lax.fori_loop(0, t // ch, chunk, tuple(jnp.zeros((HEAD_DIM, HEAD_DIM), F32) for _ in heads), unroll=4)
    gate = _silu(g_ref[0])
    for hd in heads:
        cols = slice(hd * HEAD_DIM, (hd + 1) * HEAD_DIM)
        o = oo_ref[:, cols]
        o = o * lax.rsqrt(jnp.mean(o * o, axis=-1, keepdims=True) + RMS_EPS) * ng_ref[0]
        o_ref[0, :, cols] = (o * gate[:, cols]).astype(o_ref.dtype)


def _hgrn2(proj3, col0, lower_bounds, norm_g, layer, d_hg):
    b, t, _ = proj3.shape
    depth = lower_bounds.shape[0]
    width = 2 * HEAD_DIM
    steps = d_hg // width
    assert col0 % width == 0 and d_hg % width == 0

    def colspec(k):
        return pl.BlockSpec((1, t, width), lambda i, h: (i, 0, col0 // width + k * steps + h))

    return pl.pallas_call(
        functools.partial(_hgrn2_kernel, layer=layer),
        grid=(b, steps),
        in_specs=[colspec(0), colspec(1), colspec(2), colspec(3),
                  pl.BlockSpec((depth, width), lambda i, h: (0, h)),
                  pl.BlockSpec((1, 1, HEAD_DIM), lambda i, h: (layer, 0, 0))],
        out_specs=pl.BlockSpec((1, t, width), lambda i, h: (i, 0, h)),
        out_shape=jax.ShapeDtypeStruct((b, t, d_hg), MXU_DTYPE),
        scratch_shapes=[pltpu.VMEM((t, width), F32)] * 4,
        compiler_params=_params("parallel", "parallel"),
        name="hgrn2",
    )(proj3, proj3, proj3, proj3, lower_bounds, norm_g.reshape(depth, 1, HEAD_DIM))


def _compress_kernel(*refs):
    x_refs = refs[:2 * NSA_KV]
    pek_ref, pev_ref, w1k_ref, w2k_ref, w1v_ref, w2v_ref, ko_ref, vo_ref = refs[2 * NSA_KV:]
    nrow = x_refs[0].shape[1] // CMP_STRIDE
    row = lax.broadcasted_iota(jnp.int32, (nrow, HEAD_DIM), 0)
    branches = ((pek_ref, w1k_ref, w2k_ref, ko_ref), (pev_ref, w1v_ref, w2v_ref, vo_ref))
    for kv, (pe_ref, w1_ref, w2_ref, out_ref) in enumerate(branches):
        for g in range(NSA_KV):
            x_ref = x_refs[kv * NSA_KV + g]
            lo = jnp.concatenate(
                [x_ref[0, pl.ds(i, nrow, stride=CMP_STRIDE), :] for i in range(CMP_STRIDE)], axis=1)
            hi = pltpu.roll(lo, nrow - 1, axis=0)
            blk = jnp.concatenate([lo, hi], axis=1) + pe_ref[0]
            hid = _silu(_mm(blk, w1_ref[0]))
            out = _mm(hid, w2_ref[0])
            out_ref[0, g] = jnp.where(row < nrow - 1, out, 0.0)


def _compress(proj3, col, pe_k, pe_v, w1_k, w2_k, w1_v, w2_v, layer):
    b, t, _ = proj3.shape
    nrow = t // CMP_STRIDE
    nx = 2 * NSA_KV
    assert col % HEAD_DIM == 0
    xspec = lambda k: pl.BlockSpec((1, t, HEAD_DIM), lambda i: (i, 0, col // HEAD_DIM + k))
    depth = pe_k.shape[0]
    hidden = w1_k.shape[2]
    flat = CMP_LEN * HEAD_DIM
    pespec = pl.BlockSpec((1, 1, flat), lambda i: (layer, 0, 0))
    w1spec = pl.BlockSpec((1, flat, hidden), lambda i: (layer, 0, 0))
    w2spec = pl.BlockSpec((1, hidden, HEAD_DIM), lambda i: (layer, 0, 0))
    ospec = pl.BlockSpec((1, NSA_KV, nrow, HEAD_DIM), lambda i: (i, 0, 0, 0))
    oshape = jax.ShapeDtypeStruct((b, NSA_KV, nrow, HEAD_DIM), F32)
    return pl.pallas_call(
        _compress_kernel,
        grid=(b,),
        in_specs=[xspec(k) for k in range(nx)] + [pespec, pespec, w1spec, w2spec, w1spec, w2spec],
        out_specs=[ospec, ospec],
        out_shape=[oshape, oshape],
        compiler_params=_params("parallel"),
        name="nsa_compress",
    )(*([proj3] * nx), pe_k.reshape(depth, 1, flat), pe_v.reshape(depth, 1, flat), w1_k, w2_k, w1_v, w2_v)


def _nsa_kernel(slt_ref, kaux_ref, kauxc_ref, q_ref, gl_ref, sg_ref, ks_ref, vs_ref, kw_ref, vw_ref,
                kc_ref, vc_ref, ov_ref, ge_ref, o_ref,
                ksx_ref, kwx_ref, vsb_ref, vwb_ref, kcx_ref, vcb_ref):
    qb = pl.program_id(1)
    q0 = qb * Q_BLOCK
    q0a = pl.multiple_of(q0, Q_BLOCK)
    nq = NSA_GROUP * Q_BLOCK
    gw = NSA_GROUP * HEAD_DIM
    t_all = ks_ref.shape[1]
    n_sel = t_all // SEL_LEN
    kchunk = 512
    groups = range(NSA_KV)

    @pl.when(qb == 0)
    def _():
        for g in groups:
            cols = slice(g * HEAD_DIM, (g + 1) * HEAD_DIM)
            ksx_ref[g, :, 0:HEAD_DIM] = ks_ref[0, :, cols].astype(MXU_DTYPE)
            ksx_ref[g, :, HEAD_DIM:] = kaux_ref[...]
            kwx_ref[g, :, 0:HEAD_DIM] = kw_ref[0, :, cols].astype(MXU_DTYPE)
            kwx_ref[g, :, HEAD_DIM:] = kaux_ref[...]
            kcx_ref[g, :, 0:HEAD_DIM] = kc_ref[0, g].astype(MXU_DTYPE)
            kcx_ref[g, :, HEAD_DIM:] = kauxc_ref[...]
            vsb_ref[g] = vs_ref[0, :, cols].astype(MXU_DTYPE)
            vwb_ref[g] = vw_ref[0, :, cols].astype(MXU_DTYPE)
            vcb_ref[g] = vc_ref[0, g].astype(MXU_DTYPE)

    def stack(v):
        return jnp.concatenate([v] * NSA_GROUP, axis=0)

    def online(carry, s, v):
        m, l, acc = carry
        m_new = jnp.maximum(m, jnp.max(s, axis=1, keepdims=True))
        alpha = jnp.exp2(m - m_new)
        p = jnp.exp2(s - m_new)
        l = alpha * l + jnp.sum(p, axis=1, keepdims=True)
        acc = alpha * acc + jnp.dot(p.astype(MXU_DTYPE), v, preferred_element_type=F32)
        return m_new, l, acc

    def finish(carry):
        _, l, acc = carry
        return acc * jnp.where(l > 0.0, 1.0 / l, 0.0)

    init = (jnp.full((nq, 1), NEG_INF, F32), jnp.zeros((nq, 1), F32), jnp.zeros((nq, HEAD_DIM), F32))
    t_loc = lax.broadcasted_iota(jnp.int32, (Q_BLOCK, LANES), 0)
    lane = lax.broadcasted_iota(jnp.int32, (Q_BLOCK, LANES), 1)

    ncmp = kcx_ref.shape[1]
    pos_c = (CMP_LEN - 1) + CMP_STRIDE * lax.broadcasted_iota(jnp.int32, (Q_BLOCK, ncmp), 1)
    ok_c = stack(jnp.where(q0 + lax.broadcasted_iota(jnp.int32, (Q_BLOCK, ncmp), 0) >= pos_c, 1.0, 0.0)) > 0.5
    tri_low = stack(jnp.where(lax.broadcasted_iota(jnp.int32, (Q_BLOCK, Q_BLOCK), 1)
                              <= lax.broadcasted_iota(jnp.int32, (Q_BLOCK, Q_BLOCK), 0), 1.0, 0.0)) > 0.5
    n_full = WINDOW // SEL_LEN - 1
    win_bias = jnp.where(lane < n_sel,
                         jnp.where(lane >= qb - n_full, jnp.where(lane < qb, 0.0, MASK_VAL), MASK_VAL), 0.0)
    has_far = jnp.where(q0 >= WINDOW, 1.0, 0.0)
    ok_edge = stack(jnp.where(lane < Q_BLOCK,
                              jnp.where(lane > t_loc, has_far, 0.0),
                              jnp.where(lane - Q_BLOCK <= t_loc, 1.0, 0.0))
                    ) > 0.5
    w0 = pl.multiple_of(jnp.maximum(q0 - WINDOW, 0), Q_BLOCK)
    nblk = lax.broadcasted_iota(jnp.int32, (n_sel, Q_BLOCK), 0)
    forced = (nblk == 0) | (nblk == qb) | (nblk == qb - 1)

    q_plain, q_sel, o_cmp, o_win = [], [], [], []
    for g in groups:
        q = q_ref[0, :, g * gw:(g + 1) * gw] * (HEAD_DIM ** -0.5 * LOG2E)
        q_main = jnp.concatenate([q[:, r * HEAD_DIM:(r + 1) * HEAD_DIM] for r in range(NSA_GROUP)],
                                 axis=0).astype(MXU_DTYPE)
        slope_lanes = slt_ref[g]

        def with_aux(block_bias, q_main=q_main, slope_lanes=slope_lanes):
            return jnp.concatenate([q_main, (stack(block_bias) + slope_lanes).astype(MXU_DTYPE)], axis=1)

        qp = with_aux(jnp.zeros((Q_BLOCK, LANES), F32))
        q_plain.append(qp)

        carry = online(init, _mm_nt(with_aux(win_bias), kwx_ref[g, pl.ds(w0, WINDOW), :]),
                       vwb_ref[g, pl.ds(w0, WINDOW), :])
        k_edge = jnp.concatenate([kwx_ref[g, pl.ds(w0, Q_BLOCK), :], kwx_ref[g, pl.ds(q0a, Q_BLOCK), :]], axis=0)
        v_edge = jnp.concatenate([vwb_ref[g, pl.ds(w0, Q_BLOCK), :], vwb_ref[g, pl.ds(q0a, Q_BLOCK), :]], axis=0)
        s_edge = jnp.where(ok_edge, _mm_nt(qp, k_edge), MASK_VAL)
        o_win.append(finish(online(carry, s_edge, v_edge)))

        s_c = jnp.where(ok_c, _mm_nt(qp, kcx_ref[g]), MASK_VAL)
        p_c = jnp.exp2(s_c - jnp.maximum(jnp.max(s_c, axis=1, keepdims=True), NEG_INF))
        l_c = jnp.sum(p_c, axis=1, keepdims=True)
        p_c = p_c * jnp.where(l_c > 0.0, 1.0 / l_c, 0.0)
        o_cmp.append(jnp.dot(p_c.astype(MXU_DTYPE), vcb_ref[g], preferred_element_type=F32))

        p_sum = p_c[0:Q_BLOCK]
        for r in range(1, NSA_GROUP):
            p_sum = p_sum + p_c[r * Q_BLOCK:(r + 1) * Q_BLOCK]
        imp = lax.dot_general(ov_ref[...], p_sum, (((1,), (1,)), ((), ())),
                              precision=lax.Precision.HIGHEST, preferred_element_type=F32)[0:n_sel]
        imp = jnp.where(nblk > qb, NEG_INF, jnp.where(forced, FORCE_SCORE, imp))
        rank = jnp.zeros(imp.shape, F32)
        for m_ in range(n_sel):
            other = imp[m_:m_ + 1, :]
            rank = rank + jnp.where(nblk > m_, jnp.where(other >= imp, 1.0, 0.0), jnp.where(other > imp, 1.0, 0.0))
        sel_bias = jnp.where(rank < float(min(SEL_TOPN, n_sel)), jnp.where(nblk < qb, 0.0, MASK_VAL), MASK_VAL)
        sel_bias = jnp.concatenate([sel_bias, jnp.zeros((LANES - n_sel, Q_BLOCK), F32)], axis=0).T
        q_sel.append(with_aux(sel_bias))

    def sel_branch(n_chunks):
        def run():
            outs = []
            for g in groups:
                carry = init
                for ci in range(n_chunks):
                    rows = slice(ci * kchunk, (ci + 1) * kchunk)
                    carry = online(carry, _mm_nt(q_sel[g], ksx_ref[g, rows, :]), vsb_ref[g, rows, :])
                s_own = jnp.where(tri_low, _mm_nt(q_plain[g], ksx_ref[g, pl.ds(q0a, Q_BLOCK), :]), MASK_VAL)
                outs.append(finish(online(carry, s_own, vsb_ref[g, pl.ds(q0a, Q_BLOCK), :])))
            return tuple(outs)
        return run

    max_chunks = (t_all - Q_BLOCK + kchunk - 1) // kchunk
    o_sel = lax.switch((q0 + kchunk - 1) // kchunk, [sel_branch(n) for n in range(max_chunks + 1)])

    def unstack(o):
        return jnp.concatenate([o[r * Q_BLOCK:(r + 1) * Q_BLOCK] for r in range(NSA_GROUP)], axis=1)

    sig = _sigmoid(gl_ref[0])
    sig_hi = sig.astype(MXU_DTYPE)
    sig_lo = (sig - sig_hi.astype(F32)).astype(MXU_DTYPE)
    for g in groups:
        out = None
        for i, branch in enumerate((o_cmp[g], o_sel[g], o_win[g])):
            gate = (jnp.dot(sig_hi, ge_ref[g, i], preferred_element_type=F32)
                    + jnp.dot(sig_lo, ge_ref[g, i], preferred_element_type=F32))
            term = gate * unstack(branch)
            out = term if out is None else out + term
        cols = slice(g * gw, (g + 1) * gw)
        o_ref[0, :, cols] = (out * _silu(sg_ref[0, :, cols])).astype(o_ref.dtype)


def _split3(v):
    bf = jnp.bfloat16
    hi = v.astype(bf).astype(np.float64)
    mid = (v - hi).astype(bf).astype(np.float64)
    lo = (v - hi - mid).astype(bf).astype(np.float64)
    return hi, mid, lo


def _slope_lanes():
    heads = NSA_KV * NSA_GROUP
    h = np.arange(1, heads + 1, dtype=np.float32)
    s = np.power(np.float32(2.0), -8.0 * h / heads).astype(np.float32).astype(np.float64) * LOG2E
    parts = _split3(s)
    out = np.zeros((heads, LANES), np.float32)
    for rep in range(2):
        for i, part in enumerate(parts):
            out[:, POS_LANE + 3 * rep + i] = part
    out = np.repeat(out.reshape(NSA_KV, NSA_GROUP, 1, LANES), Q_BLOCK, axis=2)
    return jnp.asarray(out.reshape(NSA_KV, NSA_GROUP * Q_BLOCK, LANES))


def _key_aux(pos, blocks):
    out = np.zeros((pos.shape[0], LANES), np.float32)
    if blocks:
        out[np.arange(pos.shape[0]), pos // SEL_LEN] = 1.0
    out[:, POS_LANE:POS_LANE + 3] = (SEL_LEN * (pos // SEL_LEN))[:, None]
    out[:, POS_LANE + 3:POS_LANE + 6] = (pos % SEL_LEN)[:, None]
    return jnp.asarray(out, dtype=MXU_DTYPE)


def _overlap_t(t, ncmp_rows):
    n_cmp = (t - CMP_LEN) // CMP_STRIDE + 1
    n_sel = t // SEL_LEN
    s_c = np.arange(n_cmp) * CMP_STRIDE
    s_s = np.arange(n_sel) * SEL_LEN
    ov = np.clip(np.minimum(s_c[:, None] + CMP_LEN, s_s[None, :] + SEL_LEN)
                 - np.maximum(s_c[:, None], s_s[None, :]), 0, None).astype(np.float32) / CMP_LEN
    out = np.zeros((LANES, ncmp_rows), np.float32)
    out[:n_sel, :n_cmp] = ov.T
    return jnp.asarray(out)


def _gate_expand():
    width = NSA_GROUP * HEAD_DIM
    e = np.zeros((NSA_KV, 3, LANES, width), np.float32)
    for g in range(NSA_KV):
        for r in range(NSA_GROUP):
            for i in range(3):
                e[g, i, (g * NSA_GROUP + r) * 3 + i, r * HEAD_DIM:(r + 1) * HEAD_DIM] = 1.0
    return jnp.asarray(e, dtype=MXU_DTYPE)


def _nsa(proj3, cols, k_cmp, v_cmp):
    b, t, _ = proj3.shape
    assert t // SEL_LEN <= POS_LANE and t % 512 == 0 and Q_BLOCK == SEL_LEN
    gw = NSA_GROUP * HEAD_DIM
    width = NSA_KV * gw
    kvw = NSA_KV * HEAD_DIM
    ncmp_rows = k_cmp.shape[2]
    q_c, ks_c, kw_c, gl_c, sg_c = cols
    assert q_c % width == 0 and sg_c % width == 0 and ks_c % kvw == 0 and kw_c % kvw == 0

    def kvspec(c, is_v):
        return pl.BlockSpec((1, t, kvw), lambda i, j: (i, 0, c // kvw + (1 if is_v else 0)))

    def whole(*shape):
        return pl.BlockSpec(shape, lambda i, j: (0,) * len(shape))

    cmpspec = pl.BlockSpec((1, NSA_KV, ncmp_rows, HEAD_DIM), lambda i, j: (i, 0, 0, 0))
    cmp_pos = (CMP_LEN - 1) + CMP_STRIDE * np.arange(ncmp_rows)
    return pl.pallas_call(
        _nsa_kernel,
        grid=(b, t // Q_BLOCK),
        in_specs=[whole(NSA_KV, NSA_GROUP * Q_BLOCK, LANES), whole(t, LANES), whole(ncmp_rows, LANES),
                  pl.BlockSpec((1, Q_BLOCK, width), lambda i, j: (i, j, q_c // width)),
                  pl.BlockSpec((1, Q_BLOCK, LANES), lambda i, j: (i, j, gl_c // LANES)),
                  pl.BlockSpec((1, Q_BLOCK, width), lambda i, j: (i, j, sg_c // width)),
                  kvspec(ks_c, False), kvspec(ks_c, True), kvspec(kw_c, False), kvspec(kw_c, True),
                  cmpspec, cmpspec, whole(LANES, ncmp_rows), whole(NSA_KV, 3, LANES, gw)],
        out_specs=pl.BlockSpec((1, Q_BLOCK, width), lambda i, j: (i, j, 0)),
        out_shape=jax.ShapeDtypeStruct((b, t, width), MXU_DTYPE),
        scratch_shapes=[pltpu.VMEM((NSA_KV, t, 2 * HEAD_DIM), MXU_DTYPE),
                        pltpu.VMEM((NSA_KV, t, 2 * HEAD_DIM), MXU_DTYPE),
                        pltpu.VMEM((NSA_KV, t, HEAD_DIM), MXU_DTYPE), pltpu.VMEM((NSA_KV, t, HEAD_DIM), MXU_DTYPE),
                        pltpu.VMEM((NSA_KV, ncmp_rows, 2 * HEAD_DIM), MXU_DTYPE),
                        pltpu.VMEM((NSA_KV, ncmp_rows, HEAD_DIM), MXU_DTYPE)],
        compiler_params=_params("parallel", "arbitrary"),
        name="nsa_attn",
    )(_slope_lanes(), _key_aux(np.arange(t), True), _key_aux(cmp_pos, False),
      proj3, proj3, proj3, proj3, proj3, proj3, proj3, k_cmp, v_cmp, _overlap_t(t, ncmp_rows), _gate_expand())


def _out_ln_kernel(a0_ref, a1_ref, a2_ref, a3_ref, w0_ref, w1_ref, w2_ref, w3_ref,
                   x_ref, gate_ref, lng_ref, lnb_ref, shift_ref, scale_ref, *rest, alpha, emit_u):
    if emit_u:
        o_ref, u_ref, pre_ref, s1_ref, s2_ref = rest
    else:
        (o_ref, pre_ref, s1_ref, s2_ref), u_ref = rest, None
    i, j = pl.program_id(0), pl.program_id(1)
    n_row, nj = pl.num_programs(0) - 1, pl.num_programs(1)
    tn = o_ref.shape[1]
    slot = lax.rem(i, 2)

    @pl.when((i == 0) & (j == 0))
    def _():
        s1_ref[...] = jnp.zeros(s1_ref.shape, F32)
        s2_ref[...] = jnp.zeros(s2_ref.shape, F32)

    def build():
        y = (jnp.dot(a0_ref[...], w0_ref[0], preferred_element_type=F32)
             + jnp.dot(a1_ref[...], w1_ref[0], preferred_element_type=F32)
             + jnp.dot(a2_ref[...], w2_ref[0], preferred_element_type=F32)
             + jnp.dot(a3_ref[...], w3_ref[0], preferred_element_type=F32))
        pre = alpha * x_ref[...] + (1.0 + gate_ref[0, 0]) * y
        pre_ref[slot * nj + j] = pre
        keep = jnp.where(j == 0, 0.0, 1.0)
        s1_ref[slot] = keep * s1_ref[slot] + pre.sum(axis=-1, keepdims=True)
        s2_ref[slot] = keep * s2_ref[slot] + (pre * pre).sum(axis=-1, keepdims=True)

    def normalise():
        d = nj * tn
        prev = 1 - slot
        mu = s1_ref[prev] / d
        inv = lax.rsqrt(s2_ref[prev] / d - mu * mu + LN_EPS)
        out = (pre_ref[prev * nj + j] - mu) * inv * lng_ref[0] + lnb_ref[0]
        o_ref[...] = out
        if emit_u:
            u_ref[...] = (out * (1.0 + scale_ref[0, 0]) + shift_ref[0, 0]).astype(u_ref.dtype)

    pl.when(i == 0)(build)

    @pl.when((i > 0) & (i < n_row))
    def _():
        normalise()
        build()

    pl.when(i == n_row)(normalise)


def _out_ln(y_rg, y_nsa, y_hg, w_out_b, x2, mod4, ln_g, ln_b, layer, alpha, t, emit_u):
    m, d = x2.shape
    depth = w_out_b.shape[0]
    kb = y_rg.shape[1]
    assert y_nsa.shape[1] == 2 * kb and y_hg.shape[1] == kb and d == 4 * kb
    tm, tn = 512, 1024
    nj, n_row = d // tn, m // tm
    assert t % tm == 0
    mrow = lambda i: jnp.minimum(i, n_row - 1)
    mcol = lambda i, j: jnp.where(i < n_row, j, nj - 1)
    orow = lambda i: jnp.maximum(i - 1, 0)
    ocol = lambda i, j: jnp.where(i > 0, j, 0)
    aspec = lambda c: pl.BlockSpec((tm, kb), lambda i, j: (mrow(i), c))
    wspec = lambda r: pl.BlockSpec((1, kb, tn), lambda i, j: (layer, r, mcol(i, j)))
    vspec = pl.BlockSpec((1, 1, tn), lambda i, j: (layer, 0, ocol(i, j)))
    nxt = min(layer + 1, depth - 1)
    modspec = lambda part: pl.BlockSpec(
        (1, 1, 1, tn), lambda i, j: (nxt, (orow(i) * tm) // t, 0, part * nj + ocol(i, j)))
    ospec = pl.BlockSpec((tm, tn), lambda i, j: (orow(i), ocol(i, j)))
    oshape = jax.ShapeDtypeStruct((m, d), F32)
    return pl.pallas_call(
        functools.partial(_out_ln_kernel, alpha=alpha, emit_u=emit_u),
        grid=(n_row + 1, nj),
        in_specs=[aspec(0), aspec(0), aspec(1), aspec(0), wspec(0), wspec(1), wspec(2), wspec(3),
                  pl.BlockSpec((tm, tn), lambda i, j: (mrow(i), mcol(i, j))),
                  pl.BlockSpec((1, 1, 1, tn), lambda i, j: (layer, (mrow(i) * tm) // t, 0, 2 * nj + mcol(i, j))),
                  vspec, vspec, modspec(0), modspec(1)],
        out_specs=[ospec, ospec] if emit_u else ospec,
        out_shape=[oshape, jax.ShapeDtypeStruct((m, d), MXU_DTYPE)] if emit_u else oshape,
        scratch_shapes=[pltpu.VMEM((2 * nj, tm, tn), F32), pltpu.VMEM((2, tm, 1), F32), pltpu.VMEM((2, tm, 1), F32)],
        compiler_params=_params("arbitrary", "arbitrary"),
        name="out_ln",
    )(y_rg, y_nsa, y_nsa, y_hg, w_out_b, w_out_b, w_out_b, w_out_b, x2, mod4,
      ln_g.reshape(depth, 1, d), ln_b.reshape(depth, 1, d), mod4, mod4)


def kernel(x, c, w_ada, b_ada, w_in, rg_conv_w, rg_conv_b, rg_w_a, rg_b_a, rg_w_x, rg_b_x, rg_lambda,
           nsa_pe_k, nsa_pe_v, nsa_cmp_w1_k, nsa_cmp_w2_k, nsa_cmp_w1_v, nsa_cmp_w2_v,
           hg_lower_bounds, hg_norm_g, w_out, ln_g, ln_b):
    b, t, d = x.shape
    depth = w_ada.shape[0]
    m = b * t
    d_rg, d_nsa, d_hg = d // 4, d // 2, d // 4
    kv_cols = 2 * NSA_KV * HEAD_DIM
    n_gl = 3 * NSA_KV * NSA_GROUP
    alpha = (2.0 * depth) ** 0.25
    assert d_nsa == NSA_KV * NSA_GROUP * HEAD_DIM and b <= SUBLANES

    gl_pad = 512
    c_q = 2 * d_rg
    c_kvc = c_q + d_nsa
    c_kvs = c_kvc + kv_cols
    c_kvw = c_kvs + kv_cols
    c_gl = c_kvw + kv_cols
    c_sg = c_gl + gl_pad
    c_hg = c_sg + d_nsa
    w_in_p = _prep_w_in(jnp.swapaxes(w_in, 1, 2), c_gl, n_gl, gl_pad)
    w_out_b = w_out.astype(MXU_DTYPE)
    w1_k, w1_v = nsa_cmp_w1_k.astype(MXU_DTYPE), nsa_cmp_w1_v.astype(MXU_DTYPE)

    c_pad = jnp.pad(c, ((0, SUBLANES - b), (0, 0)))
    mod4 = _ada(c_pad, w_ada, b_ada).reshape(depth, SUBLANES, 1, 3 * d)

    u = _modulate(x, mod4, 0).reshape(m, d)
    x = x.reshape(m, d)
    for layer in range(depth):
        proj3 = _in_proj(u, w_in_p, layer).reshape(b, t, -1)
        y_rg = _rglru(proj3, rg_conv_w, rg_conv_b, rg_w_a, rg_b_a, rg_w_x, rg_b_x, rg_lambda, layer, d_rg)
        k_cmp, v_cmp = _compress(proj3, c_kvc, nsa_pe_k, nsa_pe_v, w1_k, nsa_cmp_w2_k, w1_v, nsa_cmp_w2_v, layer)
        y_nsa = _nsa(proj3, (c_q, c_kvs, c_kvw, c_gl, c_sg), k_cmp, v_cmp)
        y_hg = _hgrn2(proj3, c_hg, hg_lower_bounds, hg_norm_g, layer, d_hg)
        emit_u = layer + 1 < depth
        res = _out_ln(y_rg.reshape(m, d_rg), y_nsa.reshape(m, d_nsa), y_hg.reshape(m, d_hg), w_out_b,
                      x, mod4, ln_g, ln_b, layer, alpha, t, emit_u)
        x, u = res if emit_u else (res, None)
    return x.reshape(b, t, d)
```

```python
import functools

import numpy as np
import jax
import jax.numpy as jnp
from jax import lax
from jax.experimental import pallas as pl
from jax.experimental.pallas import tpu as pltpu

F32 = jnp.float32
MXU_DTYPE = jnp.bfloat16

HEAD_DIM = 128
RG_CONV = 4
RG_C = 8.0
NSA_KV = 2
NSA_GROUP = 8
CMP_LEN = 32
CMP_STRIDE = 16
SEL_LEN = 64
SEL_TOPN = 16
WINDOW = 512
Q_BLOCK = 64
HG_CHUNK = 64
HG_SUB = 8
LN_EPS = 1e-5
RMS_EPS = 1e-6
NEG_INF = -1e30
FORCE_SCORE = 1e9
MASK_VAL = -(2.0 ** 100)
LOG2E = 1.4426950408889634
POS_LANE = 32

LANES = 128
SUBLANES = 8
VMEM_LIMIT = 56 * 1024 * 1024


def _mm(a, b):
    return jnp.dot(a.astype(MXU_DTYPE), b.astype(MXU_DTYPE), preferred_element_type=F32)


def _mm_nt(a, b):
    return lax.dot_general(a.astype(MXU_DTYPE), b.astype(MXU_DTYPE),
                           (((1,), (1,)), ((), ())), preferred_element_type=F32)


def _mm_tn(a, b):
    return lax.dot_general(a.astype(MXU_DTYPE), b.astype(MXU_DTYPE),
                           (((0,), (0,)), ((), ())), preferred_element_type=F32)


def _sigmoid(v):
    return jax.nn.sigmoid(v)


def _silu(v):
    return v * jax.nn.sigmoid(v)


def _params(*semantics):
    return pltpu.CompilerParams(dimension_semantics=semantics, vmem_limit_bytes=VMEM_LIMIT)


def _ada_kernel(c_ref, w_ref, b_ref, o_ref):
    o_ref[0] = _mm(c_ref[...], w_ref[0]) + b_ref[0]


def _ada(c_pad, w_ada, b_ada):
    depth, d, n3 = w_ada.shape
    rows = c_pad.shape[0]
    tn = 1024
    return pl.pallas_call(
        _ada_kernel,
        grid=(depth, n3 // tn),
        in_specs=[pl.BlockSpec((rows, d), lambda l, j: (0, 0)),
                  pl.BlockSpec((1, d, tn), lambda l, j: (l, 0, j)),
                  pl.BlockSpec((1, 1, tn), lambda l, j: (l, 0, j))],
        out_specs=pl.BlockSpec((1, rows, tn), lambda l, j: (l, 0, j)),
        out_shape=jax.ShapeDtypeStruct((depth, rows, n3), F32),
        compiler_params=_params("parallel", "parallel"),
        name="ada",
    )(c_pad, w_ada, b_ada.reshape(depth, 1, n3))


def _modulate_kernel(x_ref, shift_ref, scale_ref, o_ref):
    o_ref[0] = (x_ref[0] * (1.0 + scale_ref[0, 0]) + shift_ref[0, 0]).astype(o_ref.dtype)


def _modulate(x, mod4, layer):
    b, t, d = x.shape
    tt = 256
    return pl.pallas_call(
        _modulate_kernel,
        grid=(b, t // tt),
        in_specs=[pl.BlockSpec((1, tt, d), lambda i, j: (i, j, 0)),
                  pl.BlockSpec((1, 1, 1, d), lambda i, j: (layer, i, 0, 0)),
                  pl.BlockSpec((1, 1, 1, d), lambda i, j: (layer, i, 0, 1))],
        out_specs=pl.BlockSpec((1, tt, d), lambda i, j: (i, j, 0)),
        out_shape=jax.ShapeDtypeStruct((b, t, d), MXU_DTYPE),
        compiler_params=_params("parallel", "parallel"),
        name="modulate",
    )(x, mod4, mod4)


def _prep_w_in_kernel(w_ref, o_ref, prev_ref, *, gl_blk, n_gl):
    j = pl.program_id(1)
    tn = w_ref.shape[1]

    @pl.when(j < gl_blk)
    def _():
        o_ref[0] = w_ref[0].T.astype(o_ref.dtype)

    @pl.when(j == gl_blk)
    def _():
        row = lax.broadcasted_iota(jnp.int32, w_ref.shape[1:], 0)
        o_ref[0] = jnp.where(row < n_gl, w_ref[0], 0.0).T.astype(o_ref.dtype)

    @pl.when(j > gl_blk)
    def _():
        tile = jnp.concatenate([prev_ref[n_gl:tn, :], w_ref[0, 0:n_gl, :]], axis=0)
        o_ref[0] = tile.T.astype(o_ref.dtype)

    prev_ref[...] = w_ref[0]


def _prep_w_in(w_in_t, c_gl, n_gl, tn):
    depth, n, d = w_in_t.shape
    assert c_gl % tn == 0 and (n - n_gl) % tn == 0 and n_gl % SUBLANES == 0
    n_out = n - n_gl + tn
    return pl.pallas_call(
        functools.partial(_prep_w_in_kernel, gl_blk=c_gl // tn, n_gl=n_gl),
        grid=(depth, n_out // tn),
        in_specs=[pl.BlockSpec((1, tn, d), lambda l, j: (l, j, 0))],
        out_specs=pl.BlockSpec((1, d, tn), lambda l, j: (l, 0, j)),
        out_shape=jax.ShapeDtypeStruct((depth, d, n_out), MXU_DTYPE),
        scratch_shapes=[pltpu.VMEM((tn, d), F32)],
        compiler_params=_params("arbitrary", "arbitrary"),
        name="prep_w_in",
    )(w_in_t)


def _in_proj_kernel(a_ref, w_ref, o_ref):
    o_ref[...] = jnp.dot(a_ref[...], w_ref[0], preferred_element_type=F32)


def _in_proj(u, w_in_p, layer):
    m, d = u.shape
    n = w_in_p.shape[2]
    tm, tn = 1024, 1024
    return pl.pallas_call(
        _in_proj_kernel,
        grid=(m // tm, n // tn),
        in_specs=[pl.BlockSpec((tm, d), lambda i, j: (i, 0)),
                  pl.BlockSpec((1, d, tn), lambda i, j: (layer, 0, j))],
        out_specs=pl.BlockSpec((tm, tn), lambda i, j: (i, j)),
        out_shape=jax.ShapeDtypeStruct((m, n), F32),
        compiler_params=_params("parallel", "parallel"),
        name="in_proj",
    )(u, w_in_p)


def _rglru_kernel(x_ref, g_ref, cw_ref, cb_ref, wa_ref, ba_ref, wx_ref, bx_ref, lam_ref, o_ref,
                  xp_ref, a_ref, b_ref, h_ref):
    t, cb = x_ref.shape[1], x_ref.shape[2]
    pad = SUBLANES
    xp_ref[0:pad, :] = jnp.zeros((pad, cb), F32)
    xp_ref[pad:, :] = x_ref[0]
    xc = cb_ref[0]
    for j in range(RG_CONV):
        xc = xc + xp_ref[pl.ds(pad - (RG_CONV - 1) + j, t), :] * cw_ref[0, j:j + 1, :]

    row = lax.broadcasted_iota(jnp.int32, (t, HEAD_DIM), 0)
    sub = jnp.bitwise_and(row, SUBLANES - 1)
    for n in range(cb // HEAD_DIM):
        sl = slice(n * HEAD_DIM, (n + 1) * HEAD_DIM)
        xb = xc[:, sl]
        r = _sigmoid(_mm(xb, wa_ref[0, n]) + ba_ref[0, :, sl])
        i = _sigmoid(_mm(xb, wx_ref[0, n]) + bx_ref[0, :, sl])
        neg_lam = -lam_ref[0, :, sl]
        softplus = jnp.maximum(neg_lam, 0.0) + jnp.log1p(jnp.exp(-jnp.abs(neg_lam)))
        log_a = (-RG_C * softplus) * r
        a = jnp.exp(log_a)
        mult = jnp.sqrt(-jnp.tanh(log_a) * (a * a + 1.0))
        mult = jnp.where(row == 0, 1.0, mult)
        bx = mult * (i * xb)
        s = 1
        while s < SUBLANES:
            a_sh = pltpu.roll(a, s, axis=0)
            b_sh = pltpu.roll(bx, s, axis=0)
            inside = sub >= s
            bx = jnp.where(inside, a * b_sh + bx, bx)
            a = jnp.where(inside, a * a_sh, a)
            s *= 2
        a_ref[:, sl] = a
        b_ref[:, sl] = bx

    def carry_rows(v, h):
        r0 = pl.multiple_of(v * SUBLANES, SUBLANES)
        hh = a_ref[pl.ds(r0, SUBLANES), :] * h + b_ref[pl.ds(r0, SUBLANES), :]
        h_ref[pl.ds(r0, SUBLANES), :] = hh
        return jnp.broadcast_to(hh[SUBLANES - 1:SUBLANES, :], hh.shape)

    lax.fori_loop(0, t // SUBLANES, carry_rows, jnp.zeros((SUBLANES, cb), F32))
    o_ref[0] = (h_ref[...] * _silu(g_ref[0])).astype(o_ref.dtype)


def _rglru(proj3, conv_w, conv_b, w_a, b_a, w_x, b_x, lam, layer, d_rg):
    b, t, _ = proj3.shape
    depth = conv_w.shape[0]
    cb = 256
    nblk = cb // HEAD_DIM
    ncb = d_rg // cb
    vec = lambda v: v.reshape(depth, 1, d_rg)
    vspec = pl.BlockSpec((1, 1, cb), lambda i, j: (layer, 0, j))
    wspec = pl.BlockSpec((1, nblk, HEAD_DIM, HEAD_DIM), lambda i, j: (layer, j, 0, 0))
    return pl.pallas_call(
        _rglru_kernel,
        grid=(b, ncb),
        in_specs=[pl.BlockSpec((1, t, cb), lambda i, j: (i, 0, j)),
                  pl.BlockSpec((1, t, cb), lambda i, j: (i, 0, ncb + j)),
                  pl.BlockSpec((1, RG_CONV, cb), lambda i, j: (layer, 0, j)),
                  vspec, wspec, vspec, wspec, vspec, vspec],
        out_specs=pl.BlockSpec((1, t, cb), lambda i, j: (i, 0, j)),
        out_shape=jax.ShapeDtypeStruct((b, t, d_rg), MXU_DTYPE),
        scratch_shapes=[pltpu.VMEM((t + SUBLANES, cb), F32), pltpu.VMEM((t, cb), F32),
                        pltpu.VMEM((t, cb), F32), pltpu.VMEM((t, cb), F32)],
        compiler_params=_params("parallel", "parallel"),
        name="rglru",
    )(proj3, proj3, conv_w, vec(conv_b), w_a, vec(b_a), w_x, vec(b_x), vec(lam))


def _hgrn2_kernel(q_ref, f_ref, v_ref, g_ref, lbr_ref, ng_ref, o_ref,
                  qs_ref, kk_ref, b_ref, oo_ref, *, layer):
    t = q_ref.shape[1]
    ch, sb = HG_CHUNK, HG_SUB
    nsb = ch // sb
    z = f_ref[0]
    ez = jnp.exp(-jnp.abs(z))
    log_sig = jnp.minimum(z, 0.0) - jnp.log(1.0 + ez)
    inv = 1.0 / (1.0 + ez)
    sig_neg = jnp.where(z >= 0.0, ez * inv, inv)
    if layer == 0:
        log_f = log_sig
        kk = sig_neg
    else:
        raw = lbr_ref[...]
        e = jnp.exp(raw - jnp.max(raw, axis=0, keepdims=True))
        p = e / jnp.sum(e, axis=0, keepdims=True)
        lb = p[1:2]
        for j in range(2, layer + 1):
            lb = lb + p[j:j + 1]
        log_lb = jnp.log(lb)
        other = jnp.log1p(-lb) + log_sig
        log_f = jnp.maximum(log_lb, other) + jnp.log(1.0 + jnp.exp(-jnp.abs(log_lb - other)))
        kk = (1.0 - lb) * sig_neg
    width = q_ref.shape[2]
    heads = range(width // HEAD_DIM)
    row = lax.broadcasted_iota(jnp.int32, (t, width), 0)
    in_chunk = jnp.bitwise_and(row, ch - 1)
    bcum = log_f
    s = 1
    while s < ch:
        bcum = bcum + jnp.where(in_chunk >= s, pltpu.roll(bcum, s, axis=0), 0.0)
        s *= 2
    b_ref[...] = bcum * LOG2E
    kk_ref[...] = kk
    qs_ref[...] = _silu(q_ref[0])

    ones = jnp.ones((HEAD_DIM, ch), MXU_DTYPE)
    lane = lax.broadcasted_iota(jnp.int32, (sb, ch), 1)
    subrow = lax.broadcasted_iota(jnp.int32, (sb, HEAD_DIM), 0)

    def chunk_head(r0, hd, st):
        cols = slice(hd * HEAD_DIM, (hd + 1) * HEAD_DIM)
        bq = b_ref[pl.ds(r0, ch), cols]
        qc = qs_ref[pl.ds(r0, ch), cols]
        kc = kk_ref[pl.ds(r0, ch), cols]
        vc = v_ref[0, pl.ds(r0, ch), cols]
        blast = bq[ch - 1:ch, :]
        o = _mm_nt(qc * jnp.exp2(bq), st)
        a_rows = []
        diag = []
        for blk in range(nsb):
            lo = blk * sb
            b_i, q_i, k_i = bq[lo:lo + sb], qc[lo:lo + sb], kc[lo:lo + sb]
            for s_ in range(sb):
                dec = jnp.where(subrow >= s_, jnp.exp2(b_i - b_i[s_:s_ + 1]), 0.0)
                diag.append(q_i * (k_i[s_:s_ + 1] * dec))
            if blk == 0:
                a_rows.append(jnp.zeros((sb, ch), F32))
            else:
                m_i = bq[lo - 1:lo]
                qd = q_i * jnp.exp2(b_i - m_i)
                kd = kc[0:lo] * jnp.exp2(m_i - bq[0:lo])
                kd = jnp.concatenate([kd, jnp.zeros((ch - lo, HEAD_DIM), F32)], axis=0)
                a_rows.append(_mm_nt(qd, kd))
        dsum = _mm(jnp.concatenate(diag, axis=0), ones)
        for blk in range(nsb):
            acc = a_rows[blk]
            for s_ in range(sb):
                idx = blk * sb + s_
                acc = acc + jnp.where(lane == idx, dsum[idx * sb:(idx + 1) * sb], 0.0)
            a_rows[blk] = acc
        a_mat = jnp.concatenate(a_rows, axis=0)
        oo_ref[pl.ds(r0, ch), cols] = o + _mm(a_mat, vc)
        kdec = kc * jnp.exp2(blast - bq)
        return st * jnp.exp2(blast) + _mm_tn(vc, kdec)

    def chunk(c, states):
        r0 = pl.multiple_of(c * ch, ch)
        return tuple(chunk_head(r0, hd, states[hd]) for hd in heads)

    lax.fori_loop(0, t // ch, chunk, tuple(jnp.zeros((HEAD_DIM, HEAD_DIM), F32) for _ in heads), unroll=4)
    gate = _silu(g_ref[0])
    for hd in heads:
        cols = slice(hd * HEAD_DIM, (hd + 1) * HEAD_DIM)
        o = oo_ref[:, cols]
        o = o * lax.rsqrt(jnp.mean(o * o, axis=-1, keepdims=True) + RMS_EPS) * ng_ref[0]
        o_ref[0, :, cols] = (o * gate[:, cols]).astype(o_ref.dtype)


def _hgrn2(proj3, col0, lower_bounds, norm_g, layer, d_hg):
    b, t, _ = proj3.shape
    depth = lower_bounds.shape[0]
    width = 2 * HEAD_DIM
    steps = d_hg // width
    assert col0 % width == 0 and d_hg % width == 0

    def colspec(k):
        return pl.BlockSpec((1, t, width), lambda i, h: (i, 0, col0 // width + k * steps + h))

    return pl.pallas_call(
        functools.partial(_hgrn2_kernel, layer=layer),
        grid=(b, steps),
        in_specs=[colspec(0), colspec(1), colspec(2), colspec(3),
                  pl.BlockSpec((depth, width), lambda i, h: (0, h)),
                  pl.BlockSpec((1, 1, HEAD_DIM), lambda i, h: (layer, 0, 0))],
        out_specs=pl.BlockSpec((1, t, width), lambda i, h: (i, 0, h)),
        out_shape=jax.ShapeDtypeStruct((b, t, d_hg), MXU_DTYPE),
        scratch_shapes=[pltpu.VMEM((t, width), F32)] * 4,
        compiler_params=_params("parallel", "parallel"),
        name="hgrn2",
    )(proj3, proj3, proj3, proj3, lower_bounds, norm_g.reshape(depth, 1, HEAD_DIM))


def _compress_kernel(*refs):
    x_refs = refs[:2 * NSA_KV]
    pek_ref, pev_ref, w1k_ref, w2k_ref, w1v_ref, w2v_ref, ko_ref, vo_ref = refs[2 * NSA_KV:]
    nrow = x_refs[0].shape[1] // CMP_STRIDE
    row = lax.broadcasted_iota(jnp.int32, (nrow, HEAD_DIM), 0)
    branches = ((pek_ref, w1k_ref, w2k_ref, ko_ref), (pev_ref, w1v_ref, w2v_ref, vo_ref))
    for kv, (pe_ref, w1_ref, w2_ref, out_ref) in enumerate(branches):
        for g in range(NSA_KV):
            x_ref = x_refs[kv * NSA_KV + g]
            lo = jnp.concatenate(
                [x_ref[0, pl.ds(i, nrow, stride=CMP_STRIDE), :] for i in range(CMP_STRIDE)], axis=1)
            hi = pltpu.roll(lo, nrow - 1, axis=0)
            blk = jnp.concatenate([lo, hi], axis=1) + pe_ref[0]
            hid = _silu(_mm(blk, w1_ref[0]))
            out = _mm(hid, w2_ref[0])
            out_ref[0, g] = jnp.where(row < nrow - 1, out, 0.0)


def _compress(proj3, col, pe_k, pe_v, w1_k, w2_k, w1_v, w2_v, layer):
    b, t, _ = proj3.shape
    nrow = t // CMP_STRIDE
    nx = 2 * NSA_KV
    assert col % HEAD_DIM == 0
    xspec = lambda k: pl.BlockSpec((1, t, HEAD_DIM), lambda i: (i, 0, col // HEAD_DIM + k))
    depth = pe_k.shape[0]
    hidden = w1_k.shape[2]
    flat = CMP_LEN * HEAD_DIM
    pespec = pl.BlockSpec((1, 1, flat), lambda i: (layer, 0, 0))
    w1spec = pl.BlockSpec((1, flat, hidden), lambda i: (layer, 0, 0))
    w2spec = pl.BlockSpec((1, hidden, HEAD_DIM), lambda i: (layer, 0, 0))
    ospec = pl.BlockSpec((1, NSA_KV, nrow, HEAD_DIM), lambda i: (i, 0, 0, 0))
    oshape = jax.ShapeDtypeStruct((b, NSA_KV, nrow, HEAD_DIM), F32)
    return pl.pallas_call(
        _compress_kernel,
        grid=(b,),
        in_specs=[xspec(k) for k in range(nx)] + [pespec, pespec, w1spec, w2spec, w1spec, w2spec],
        out_specs=[ospec, ospec],
        out_shape=[oshape, oshape],
        compiler_params=_params("parallel"),
        name="nsa_compress",
    )(*([proj3] * nx), pe_k.reshape(depth, 1, flat), pe_v.reshape(depth, 1, flat), w1_k, w2_k, w1_v, w2_v)


def _nsa_kernel(slt_ref, kaux_ref, kauxc_ref, q_ref, gl_ref, sg_ref, ks_ref, vs_ref, kw_ref, vw_ref,
                kc_ref, vc_ref, ov_ref, ge_ref, o_ref,
                ksx_ref, kwx_ref, vsb_ref, vwb_ref, kcx_ref, vcb_ref):
    qb = pl.program_id(1)
    q0 = qb * Q_BLOCK
    q0a = pl.multiple_of(q0, Q_BLOCK)
    nq = NSA_GROUP * Q_BLOCK
    gw = NSA_GROUP * HEAD_DIM
    t_all = ks_ref.shape[1]
    n_sel = t_all // SEL_LEN
    kchunk = 512
    groups = range(NSA_KV)

    @pl.when(qb == 0)
    def _():
        for g in groups:
            cols = slice(g * HEAD_DIM, (g + 1) * HEAD_DIM)
            ksx_ref[g, :, 0:HEAD_DIM] = ks_ref[0, :, cols].astype(MXU_DTYPE)
            ksx_ref[g, :, HEAD_DIM:] = kaux_ref[...]
            kwx_ref[g, :, 0:HEAD_DIM] = kw_ref[0, :, cols].astype(MXU_DTYPE)
            kwx_ref[g, :, HEAD_DIM:] = kaux_ref[...]
            kcx_ref[g, :, 0:HEAD_DIM] = kc_ref[0, g].astype(MXU_DTYPE)
            kcx_ref[g, :, HEAD_DIM:] = kauxc_ref[...]
            vsb_ref[g] = vs_ref[0, :, cols].astype(MXU_DTYPE)
            vwb_ref[g] = vw_ref[0, :, cols].astype(MXU_DTYPE)
            vcb_ref[g] = vc_ref[0, g].astype(MXU_DTYPE)

    def stack(v):
        return jnp.concatenate([v] * NSA_GROUP, axis=0)

    def online(carry, s, v):
        m, l, acc = carry
        m_new = jnp.maximum(m, jnp.max(s, axis=1, keepdims=True))
        alpha = jnp.exp2(m - m_new)
        p = jnp.exp2(s - m_new)
        l = alpha * l + jnp.sum(p, axis=1, keepdims=True)
        acc = alpha * acc + jnp.dot(p.astype(MXU_DTYPE), v, preferred_element_type=F32)
        return m_new, l, acc

    def finish(carry):
        _, l, acc = carry
        return acc * jnp.where(l > 0.0, 1.0 / l, 0.0)

    init = (jnp.full((nq, 1), NEG_INF, F32), jnp.zeros((nq, 1), F32), jnp.zeros((nq, HEAD_DIM), F32))
    t_loc = lax.broadcasted_iota(jnp.int32, (Q_BLOCK, LANES), 0)
    lane = lax.broadcasted_iota(jnp.int32, (Q_BLOCK, LANES), 1)

    ncmp = kcx_ref.shape[1]
    pos_c = (CMP_LEN - 1) + CMP_STRIDE * lax.broadcasted_iota(jnp.int32, (Q_BLOCK, ncmp), 1)
    ok_c = stack(jnp.where(q0 + lax.broadcasted_iota(jnp.int32, (Q_BLOCK, ncmp), 0) >= pos_c, 1.0, 0.0)) > 0.5
    tri_low = stack(jnp.where(lax.broadcasted_iota(jnp.int32, (Q_BLOCK, Q_BLOCK), 1)
                              <= lax.broadcasted_iota(jnp.int32, (Q_BLOCK, Q_BLOCK), 0), 1.0, 0.0)) > 0.5
    n_full = WINDOW // SEL_LEN - 1
    win_bias = jnp.where(lane < n_sel,
                         jnp.where(lane >= qb - n_full, jnp.where(lane < qb, 0.0, MASK_VAL), MASK_VAL), 0.0)
    has_far = jnp.where(q0 >= WINDOW, 1.0, 0.0)
    ok_edge = stack(jnp.where(lane < Q_BLOCK,
                              jnp.where(lane > t_loc, has_far, 0.0),
                              jnp.where(lane - Q_BLOCK <= t_loc, 1.0, 0.0))
                    ) > 0.5
    w0 = pl.multiple_of(jnp.maximum(q0 - WINDOW, 0), Q_BLOCK)
    nblk = lax.broadcasted_iota(jnp.int32, (n_sel, Q_BLOCK), 0)
    forced = (nblk == 0) | (nblk == qb) | (nblk == qb - 1)

    q_plain, q_sel, o_cmp, o_win = [], [], [], []
    for g in groups:
        q = q_ref[0, :, g * gw:(g + 1) * gw] * (HEAD_DIM ** -0.5 * LOG2E)
        q_main = jnp.concatenate([q[:, r * HEAD_DIM:(r + 1) * HEAD_DIM] for r in range(NSA_GROUP)],
                                 axis=0).astype(MXU_DTYPE)
        slope_lanes = slt_ref[g]

        def with_aux(block_bias, q_main=q_main, slope_lanes=slope_lanes):
            return jnp.concatenate([q_main, (stack(block_bias) + slope_lanes).astype(MXU_DTYPE)], axis=1)

        qp = with_aux(jnp.zeros((Q_BLOCK, LANES), F32))
        q_plain.append(qp)

        carry = online(init, _mm_nt(with_aux(win_bias), kwx_ref[g, pl.ds(w0, WINDOW), :]),
                       vwb_ref[g, pl.ds(w0, WINDOW), :])
        k_edge = jnp.concatenate([kwx_ref[g, pl.ds(w0, Q_BLOCK), :], kwx_ref[g, pl.ds(q0a, Q_BLOCK), :]], axis=0)
        v_edge = jnp.concatenate([vwb_ref[g, pl.ds(w0, Q_BLOCK), :], vwb_ref[g, pl.ds(q0a, Q_BLOCK), :]], axis=0)
        s_edge = jnp.where(ok_edge, _mm_nt(qp, k_edge), MASK_VAL)
        o_win.append(finish(online(carry, s_edge, v_edge)))

        s_c = jnp.where(ok_c, _mm_nt(qp, kcx_ref[g]), MASK_VAL)
        p_c = jnp.exp2(s_c - jnp.maximum(jnp.max(s_c, axis=1, keepdims=True), NEG_INF))
        l_c = jnp.sum(p_c, axis=1, keepdims=True)
        p_c = p_c * jnp.where(l_c > 0.0, 1.0 / l_c, 0.0)
        o_cmp.append(jnp.dot(p_c.astype(MXU_DTYPE), vcb_ref[g], preferred_element_type=F32))

        p_sum = p_c[0:Q_BLOCK]
        for r in range(1, NSA_GROUP):
            p_sum = p_sum + p_c[r * Q_BLOCK:(r + 1) * Q_BLOCK]
        imp = lax.dot_general(ov_ref[...], p_sum, (((1,), (1,)), ((), ())),
                              precision=lax.Precision.HIGHEST, preferred_element_type=F32)[0:n_sel]
        imp = jnp.where(nblk > qb, NEG_INF, jnp.where(forced, FORCE_SCORE, imp))
        rank = jnp.zeros(imp.shape, F32)
        for m_ in range(n_sel):
            other = imp[m_:m_ + 1, :]
            rank = rank + jnp.where(nblk > m_, jnp.where(other >= imp, 1.0, 0.0), jnp.where(other > imp, 1.0, 0.0))
        sel_bias = jnp.where(rank < float(min(SEL_TOPN, n_sel)), jnp.where(nblk < qb, 0.0, MASK_VAL), MASK_VAL)
        sel_bias = jnp.concatenate([sel_bias, jnp.zeros((LANES - n_sel, Q_BLOCK), F32)], axis=0).T
        q_sel.append(with_aux(sel_bias))

    def sel_branch(n_chunks):
        def run():
            outs = []
            for g in groups:
                carry = init
                for ci in range(n_chunks):
                    rows = slice(ci * kchunk, (ci + 1) * kchunk)
                    carry = online(carry, _mm_nt(q_sel[g], ksx_ref[g, rows, :]), vsb_ref[g, rows, :])
                s_own = jnp.where(tri_low, _mm_nt(q_plain[g], ksx_ref[g, pl.ds(q0a, Q_BLOCK), :]), MASK_VAL)
                outs.append(finish(online(carry, s_own, vsb_ref[g, pl.ds(q0a, Q_BLOCK), :])))
            return tuple(outs)
        return run

    max_chunks = (t_all - Q_BLOCK + kchunk - 1) // kchunk
    o_sel = lax.switch((q0 + kchunk - 1) // kchunk, [sel_branch(n) for n in range(max_chunks + 1)])

    def unstack(o):
        return jnp.concatenate([o[r * Q_BLOCK:(r + 1) * Q_BLOCK] for r in range(NSA_GROUP)], axis=1)

    sig = _sigmoid(gl_ref[0])
    sig_hi = sig.astype(MXU_DTYPE)
    sig_lo = (sig - sig_hi.astype(F32)).astype(MXU_DTYPE)
    for g in groups:
        out = None
        for i, branch in enumerate((o_cmp[g], o_sel[g], o_win[g])):
            gate = (jnp.dot(sig_hi, ge_ref[g, i], preferred_element_type=F32)
                    + jnp.dot(sig_lo, ge_ref[g, i], preferred_element_type=F32))
            term = gate * unstack(branch)
            out = term if out is None else out + term
        cols = slice(g * gw, (g + 1) * gw)
        o_ref[0, :, cols] = (out * _silu(sg_ref[0, :, cols])).astype(o_ref.dtype)


def _split3(v):
    bf = jnp.bfloat16
    hi = v.astype(bf).astype(np.float64)
    mid = (v - hi).astype(bf).astype(np.float64)
    lo = (v - hi - mid).astype(bf).astype(np.float64)
    return hi, mid, lo


def _slope_lanes():
    heads = NSA_KV * NSA_GROUP
    h = np.arange(1, heads + 1, dtype=np.float32)
    s = np.power(np.float32(2.0), -8.0 * h / heads).astype(np.float32).astype(np.float64) * LOG2E
    parts = _split3(s)
    out = np.zeros((heads, LANES), np.float32)
    for rep in range(2):
        for i, part in enumerate(parts):
            out[:, POS_LANE + 3 * rep + i] = part
    out = np.repeat(out.reshape(NSA_KV, NSA_GROUP, 1, LANES), Q_BLOCK, axis=2)
    return jnp.asarray(out.reshape(NSA_KV, NSA_GROUP * Q_BLOCK, LANES))


def _key_aux(pos, blocks):
    out = np.zeros((pos.shape[0], LANES), np.float32)
    if blocks:
        out[np.arange(pos.shape[0]), pos // SEL_LEN] = 1.0
    out[:, POS_LANE:POS_LANE + 3] = (SEL_LEN * (pos // SEL_LEN))[:, None]
    out[:, POS_LANE + 3:POS_LANE + 6] = (pos % SEL_LEN)[:, None]
    return jnp.asarray(out, dtype=MXU_DTYPE)


def _overlap_t(t, ncmp_rows):
    n_cmp = (t - CMP_LEN) // CMP_STRIDE + 1
    n_sel = t // SEL_LEN
    s_c = np.arange(n_cmp) * CMP_STRIDE
    s_s = np.arange(n_sel) * SEL_LEN
    ov = np.clip(np.minimum(s_c[:, None] + CMP_LEN, s_s[None, :] + SEL_LEN)
                 - np.maximum(s_c[:, None], s_s[None, :]), 0, None).astype(np.float32) / CMP_LEN
    out = np.zeros((LANES, ncmp_rows), np.float32)
    out[:n_sel, :n_cmp] = ov.T
    return jnp.asarray(out)


def _gate_expand():
    width = NSA_GROUP * HEAD_DIM
    e = np.zeros((NSA_KV, 3, LANES, width), np.float32)
    for g in range(NSA_KV):
        for r in range(NSA_GROUP):
            for i in range(3):
                e[g, i, (g * NSA_GROUP + r) * 3 + i, r * HEAD_DIM:(r + 1) * HEAD_DIM] = 1.0
    return jnp.asarray(e, dtype=MXU_DTYPE)


def _nsa(proj3, cols, k_cmp, v_cmp):
    b, t, _ = proj3.shape
    assert t // SEL_LEN <= POS_LANE and t % 512 == 0 and Q_BLOCK == SEL_LEN
    gw = NSA_GROUP * HEAD_DIM
    width = NSA_KV * gw
    kvw = NSA_KV * HEAD_DIM
    ncmp_rows = k_cmp.shape[2]
    q_c, ks_c, kw_c, gl_c, sg_c = cols
    assert q_c % width == 0 and sg_c % width == 0 and ks_c % kvw == 0 and kw_c % kvw == 0

    def kvspec(c, is_v):
        return pl.BlockSpec((1, t, kvw), lambda i, j: (i, 0, c // kvw + (1 if is_v else 0)))

    def whole(*shape):
        return pl.BlockSpec(shape, lambda i, j: (0,) * len(shape))

    cmpspec = pl.BlockSpec((1, NSA_KV, ncmp_rows, HEAD_DIM), lambda i, j: (i, 0, 0, 0))
    cmp_pos = (CMP_LEN - 1) + CMP_STRIDE * np.arange(ncmp_rows)
    return pl.pallas_call(
        _nsa_kernel,
        grid=(b, t // Q_BLOCK),
        in_specs=[whole(NSA_KV, NSA_GROUP * Q_BLOCK, LANES), whole(t, LANES), whole(ncmp_rows, LANES),
                  pl.BlockSpec((1, Q_BLOCK, width), lambda i, j: (i, j, q_c // width)),
                  pl.BlockSpec((1, Q_BLOCK, LANES), lambda i, j: (i, j, gl_c // LANES)),
                  pl.BlockSpec((1, Q_BLOCK, width), lambda i, j: (i, j, sg_c // width)),
                  kvspec(ks_c, False), kvspec(ks_c, True), kvspec(kw_c, False), kvspec(kw_c, True),
                  cmpspec, cmpspec, whole(LANES, ncmp_rows), whole(NSA_KV, 3, LANES, gw)],
        out_specs=pl.BlockSpec((1, Q_BLOCK, width), lambda i, j: (i, j, 0)),
        out_shape=jax.ShapeDtypeStruct((b, t, width), MXU_DTYPE),
        scratch_shapes=[pltpu.VMEM((NSA_KV, t, 2 * HEAD_DIM), MXU_DTYPE),
                        pltpu.VMEM((NSA_KV, t, 2 * HEAD_DIM), MXU_DTYPE),
                        pltpu.VMEM((NSA_KV, t, HEAD_DIM), MXU_DTYPE), pltpu.VMEM((NSA_KV, t, HEAD_DIM), MXU_DTYPE),
                        pltpu.VMEM((NSA_KV, ncmp_rows, 2 * HEAD_DIM), MXU_DTYPE),
                        pltpu.VMEM((NSA_KV, ncmp_rows, HEAD_DIM), MXU_DTYPE)],
        compiler_params=_params("parallel", "arbitrary"),
        name="nsa_attn",
    )(_slope_lanes(), _key_aux(np.arange(t), True), _key_aux(cmp_pos, False),
      proj3, proj3, proj3, proj3, proj3, proj3, proj3, k_cmp, v_cmp, _overlap_t(t, ncmp_rows), _gate_expand())


def _out_ln_kernel(a0_ref, a1_ref, a2_ref, a3_ref, w0_ref, w1_ref, w2_ref, w3_ref,
                   x_ref, gate_ref, lng_ref, lnb_ref, shift_ref, scale_ref, *rest, alpha, emit_u):
    if emit_u:
        o_ref, u_ref, pre_ref, c_ref, s1_ref, s2_ref = rest
    else:
        (o_ref, pre_ref, c_ref, s1_ref, s2_ref), u_ref = rest, None
    i, j = pl.program_id(0), pl.program_id(1)
    n_row, nj = pl.num_programs(0) - 1, pl.num_programs(1)
    tn = o_ref.shape[1]
    slot = lax.rem(i, 2)

    @pl.when((i == 0) & (j == 0))
    def _():
        c_ref[...] = jnp.zeros(c_ref.shape, F32)
        s1_ref[...] = jnp.zeros(s1_ref.shape, F32)
        s2_ref[...] = jnp.zeros(s2_ref.shape, F32)

    def build():
        y = (jnp.dot(a0_ref[...], w0_ref[0], preferred_element_type=F32)
             + jnp.dot(a1_ref[...], w1_ref[0], preferred_element_type=F32)
             + jnp.dot(a2_ref[...], w2_ref[0], preferred_element_type=F32)
             + jnp.dot(a3_ref[...], w3_ref[0], preferred_element_type=F32))
        pre = alpha * x_ref[...] + (1.0 + gate_ref[0, 0]) * y
        pre_ref[slot * nj + j] = pre
        keep = jnp.where(j == 0, 0.0, 1.0)
        c = keep * c_ref[slot] + (1.0 - keep) * (pre.sum(axis=-1, keepdims=True) / tn)
        c_ref[slot] = c
        dev = pre - c
        s1_ref[slot] = keep * s1_ref[slot] + dev.sum(axis=-1, keepdims=True)
        s2_ref[slot] = keep * s2_ref[slot] + (dev * dev).sum(axis=-1, keepdims=True)

    def normalise():
        d = nj * tn
        prev = 1 - slot
        off = s1_ref[prev] / d
        mu = c_ref[prev] + off
        inv = lax.rsqrt(s2_ref[prev] / d - off * off + LN_EPS)
        out = (pre_ref[prev * nj + j] - mu) * inv * lng_ref[0] + lnb_ref[0]
        o_ref[...] = out
        if emit_u:
            u_ref[...] = (out * (1.0 + scale_ref[0, 0]) + shift_ref[0, 0]).astype(u_ref.dtype)

    pl.when(i == 0)(build)

    @pl.when((i > 0) & (i < n_row))
    def _():
        normalise()
        build()

    pl.when(i == n_row)(normalise)


def _out_ln(y_rg, y_nsa, y_hg, w_out_b, x2, mod4, ln_g, ln_b, layer, alpha, t, emit_u):
    m, d = x2.shape
    depth = w_out_b.shape[0]
    kb = y_rg.shape[1]
    assert y_nsa.shape[1] == 2 * kb and y_hg.shape[1] == kb and d == 4 * kb
    tm, tn = 512, 1024
    nj, n_row = d // tn, m // tm
    assert t % tm == 0
    mrow = lambda i: jnp.minimum(i, n_row - 1)
    mcol = lambda i, j: jnp.where(i < n_row, j, nj - 1)
    orow = lambda i: jnp.maximum(i - 1, 0)
    ocol = lambda i, j: jnp.where(i > 0, j, 0)
    aspec = lambda c: pl.BlockSpec((tm, kb), lambda i, j: (mrow(i), c))
    wspec = lambda r: pl.BlockSpec((1, kb, tn), lambda i, j: (layer, r, mcol(i, j)))
    vspec = pl.BlockSpec((1, 1, tn), lambda i, j: (layer, 0, ocol(i, j)))
    nxt = min(layer + 1, depth - 1)
    modspec = lambda part: pl.BlockSpec(
        (1, 1, 1, tn), lambda i, j: (nxt, (orow(i) * tm) // t, 0, part * nj + ocol(i, j)))
    ospec = pl.BlockSpec((tm, tn), lambda i, j: (orow(i), ocol(i, j)))
    oshape = jax.ShapeDtypeStruct((m, d), F32)
    return pl.pallas_call(
        functools.partial(_out_ln_kernel, alpha=alpha, emit_u=emit_u),
        grid=(n_row + 1, nj),
        in_specs=[aspec(0), aspec(0), aspec(1), aspec(0), wspec(0), wspec(1), wspec(2), wspec(3),
                  pl.BlockSpec((tm, tn), lambda i, j: (mrow(i), mcol(i, j))),
                  pl.BlockSpec((1, 1, 1, tn), lambda i, j: (layer, (mrow(i) * tm) // t, 0, 2 * nj + mcol(i, j))),
                  vspec, vspec, modspec(0), modspec(1)],
        out_specs=[ospec, ospec] if emit_u else ospec,
        out_shape=[oshape, jax.ShapeDtypeStruct((m, d), MXU_DTYPE)] if emit_u else oshape,
        scratch_shapes=[pltpu.VMEM((2 * nj, tm, tn), F32)] + [pltpu.VMEM((2, tm, 1), F32)] * 3,
        compiler_params=_params("arbitrary", "arbitrary"),
        name="out_ln",
    )(y_rg, y_nsa, y_nsa, y_hg, w_out_b, w_out_b, w_out_b, w_out_b, x2, mod4,
      ln_g.reshape(depth, 1, d), ln_b.reshape(depth, 1, d), mod4, mod4)


def kernel(x, c, w_ada, b_ada, w_in, rg_conv_w, rg_conv_b, rg_w_a, rg_b_a, rg_w_x, rg_b_x, rg_lambda,
           nsa_pe_k, nsa_pe_v, nsa_cmp_w1_k, nsa_cmp_w2_k, nsa_cmp_w1_v, nsa_cmp_w2_v,
           hg_lower_bounds, hg_norm_g, w_out, ln_g, ln_b):
    b, t, d = x.shape
    depth = w_ada.shape[0]
    m = b * t
    d_rg, d_nsa, d_hg = d // 4, d // 2, d // 4
    kv_cols = 2 * NSA_KV * HEAD_DIM
    n_gl = 3 * NSA_KV * NSA_GROUP
    alpha = (2.0 * depth) ** 0.25
    assert d_nsa == NSA_KV * NSA_GROUP * HEAD_DIM and b <= SUBLANES

    gl_pad = 512
    c_q = 2 * d_rg
    c_kvc = c_q + d_nsa
    c_kvs = c_kvc + kv_cols
    c_kvw = c_kvs + kv_cols
    c_gl = c_kvw + kv_cols
    c_sg = c_gl + gl_pad
    c_hg = c_sg + d_nsa
    w_in_p = _prep_w_in(jnp.swapaxes(w_in, 1, 2), c_gl, n_gl, gl_pad)
    w_out_b = w_out.astype(MXU_DTYPE)
    w1_k, w1_v = nsa_cmp_w1_k.astype(MXU_DTYPE), nsa_cmp_w1_v.astype(MXU_DTYPE)

    c_pad = jnp.pad(c, ((0, SUBLANES - b), (0, 0)))
    mod4 = _ada(c_pad, w_ada, b_ada).reshape(depth, SUBLANES, 1, 3 * d)

    u = _modulate(x, mod4, 0).reshape(m, d)
    x = x.reshape(m, d)
    for layer in range(depth):
        proj3 = _in_proj(u, w_in_p, layer).reshape(b, t, -1)
        y_rg = _rglru(proj3, rg_conv_w, rg_conv_b, rg_w_a, rg_b_a, rg_w_x, rg_b_x, rg_lambda, layer, d_rg)
        k_cmp, v_cmp = _compress(proj3, c_kvc, nsa_pe_k, nsa_pe_v, w1_k, nsa_cmp_w2_k, w1_v, nsa_cmp_w2_v, layer)
        y_nsa = _nsa(proj3, (c_q, c_kvs, c_kvw, c_gl, c_sg), k_cmp, v_cmp)
        y_hg = _hgrn2(proj3, c_hg, hg_lower_bounds, hg_norm_g, layer, d_hg)
        emit_u = layer + 1 < depth
        res = _out_ln(y_rg.reshape(m, d_rg), y_nsa.reshape(m, d_nsa), y_hg.reshape(m, d_hg), w_out_b,
                      x, mod4, ln_g, ln_b, layer, alpha, t, emit_u)
        x, u = res if emit_u else (res, None)
    return x.reshape(b, t, d)
```

```python
import functools

import numpy as np
import jax
import jax.numpy as jnp
from jax import lax
from jax.experimental import pallas as pl
from jax.experimental.pallas import tpu as pltpu

F32 = jnp.float32
MXU_DTYPE = jnp.bfloat16

HEAD_DIM = 128
RG_CONV = 4
RG_C = 8.0
NSA_KV = 2
NSA_GROUP = 8
CMP_LEN = 32
CMP_STRIDE = 16
SEL_LEN = 64
SEL_TOPN = 16
WINDOW = 512
Q_BLOCK = 64
HG_CHUNK = 64
HG_SUB = 8
LN_EPS = 1e-5
RMS_EPS = 1e-6
NEG_INF = -1e30
FORCE_SCORE = 1e9
MASK_VAL = -(2.0 ** 100)
LOG2E = 1.4426950408889634
POS_LANE = 32

LANES = 128
SUBLANES = 8
VMEM_LIMIT = 56 * 1024 * 1024


def _mm(a, b):
    return jnp.dot(a.astype(MXU_DTYPE), b.astype(MXU_DTYPE), preferred_element_type=F32)


def _mm_nt(a, b):
    return lax.dot_general(a.astype(MXU_DTYPE), b.astype(MXU_DTYPE),
                           (((1,), (1,)), ((), ())), preferred_element_type=F32)


def _mm_tn(a, b):
    return lax.dot_general(a.astype(MXU_DTYPE), b.astype(MXU_DTYPE),
                           (((0,), (0,)), ((), ())), preferred_element_type=F32)


def _sigmoid(v):
    return jax.nn.sigmoid(v)


def _silu(v):
    return v * jax.nn.sigmoid(v)


def _params(*semantics):
    return pltpu.CompilerParams(dimension_semantics=semantics, vmem_limit_bytes=VMEM_LIMIT)


def _ada_kernel(c_ref, w_ref, b_ref, o_ref):
    o_ref[0] = _mm(c_ref[...], w_ref[0]) + b_ref[0]


def _ada(c_pad, w_ada, b_ada):
    depth, d, n3 = w_ada.shape
    rows = c_pad.shape[0]
    tn = 1024
    return pl.pallas_call(
        _ada_kernel,
        grid=(depth, n3 // tn),
        in_specs=[pl.BlockSpec((rows, d), lambda l, j: (0, 0)),
                  pl.BlockSpec((1, d, tn), lambda l, j: (l, 0, j)),
                  pl.BlockSpec((1, 1, tn), lambda l, j: (l, 0, j))],
        out_specs=pl.BlockSpec((1, rows, tn), lambda l, j: (l, 0, j)),
        out_shape=jax.ShapeDtypeStruct((depth, rows, n3), F32),
        compiler_params=_params("parallel", "parallel"),
        name="ada",
    )(c_pad, w_ada, b_ada.reshape(depth, 1, n3))


def _modulate_kernel(x_ref, shift_ref, scale_ref, o_ref):
    o_ref[0] = (x_ref[0] * (1.0 + scale_ref[0, 0]) + shift_ref[0, 0]).astype(o_ref.dtype)


def _modulate(x, mod4, layer):
    b, t, d = x.shape
    tt = 512
    return pl.pallas_call(
        _modulate_kernel,
        grid=(b, t // tt),
        in_specs=[pl.BlockSpec((1, tt, d), lambda i, j: (i, j, 0)),
                  pl.BlockSpec((1, 1, 1, d), lambda i, j: (layer, i, 0, 0)),
                  pl.BlockSpec((1, 1, 1, d), lambda i, j: (layer, i, 0, 1))],
        out_specs=pl.BlockSpec((1, tt, d), lambda i, j: (i, j, 0)),
        out_shape=jax.ShapeDtypeStruct((b, t, d), MXU_DTYPE),
        compiler_params=_params("parallel", "parallel"),
        name="modulate",
    )(x, mod4, mod4)


def _prep_w_in_kernel(w_ref, o_ref, prev_ref, *, gl_blk, n_gl):
    j = pl.program_id(1)
    tn = w_ref.shape[1]

    @pl.when(j < gl_blk)
    def _():
        o_ref[0] = w_ref[0].T.astype(o_ref.dtype)

    @pl.when(j == gl_blk)
    def _():
        row = lax.broadcasted_iota(jnp.int32, w_ref.shape[1:], 0)
        o_ref[0] = jnp.where(row < n_gl, w_ref[0], 0.0).T.astype(o_ref.dtype)

    @pl.when(j > gl_blk)
    def _():
        tile = jnp.concatenate([prev_ref[n_gl:tn, :], w_ref[0, 0:n_gl, :]], axis=0)
        o_ref[0] = tile.T.astype(o_ref.dtype)

    prev_ref[...] = w_ref[0]


def _prep_w_in(w_in_t, c_gl, n_gl, tn):
    depth, n, d = w_in_t.shape
    assert c_gl % tn == 0 and (n - n_gl) % tn == 0 and n_gl % SUBLANES == 0
    n_out = n - n_gl + tn
    return pl.pallas_call(
        functools.partial(_prep_w_in_kernel, gl_blk=c_gl // tn, n_gl=n_gl),
        grid=(depth, n_out // tn),
        in_specs=[pl.BlockSpec((1, tn, d), lambda l, j: (l, j, 0))],
        out_specs=pl.BlockSpec((1, d, tn), lambda l, j: (l, 0, j)),
        out_shape=jax.ShapeDtypeStruct((depth, d, n_out), MXU_DTYPE),
        scratch_shapes=[pltpu.VMEM((tn, d), F32)],
        compiler_params=_params("arbitrary", "arbitrary"),
        name="prep_w_in",
    )(w_in_t)


def _in_proj_kernel(a_ref, w_ref, o_ref):
    o_ref[...] = jnp.dot(a_ref[...], w_ref[0], preferred_element_type=F32)


def _in_proj(u, w_in_p, layer):
    m, d = u.shape
    n = w_in_p.shape[2]
    tm, tn = 1024, 1024
    return pl.pallas_call(
        _in_proj_kernel,
        grid=(m // tm, n // tn),
        in_specs=[pl.BlockSpec((tm, d), lambda i, j: (i, 0)),
                  pl.BlockSpec((1, d, tn), lambda i, j: (layer, 0, j))],
        out_specs=pl.BlockSpec((tm, tn), lambda i, j: (i, j)),
        out_shape=jax.ShapeDtypeStruct((m, n), F32),
        compiler_params=_params("parallel", "parallel"),
        name="in_proj",
    )(u, w_in_p)


def _rglru_kernel(x_ref, g_ref, cw_ref, cb_ref, wa_ref, ba_ref, wx_ref, bx_ref, lam_ref, o_ref,
                  xp_ref, a_ref, b_ref, h_ref):
    t, cb = x_ref.shape[1], x_ref.shape[2]
    pad = SUBLANES
    xp_ref[0:pad, :] = jnp.zeros((pad, cb), F32)
    xp_ref[pad:, :] = x_ref[0]
    xc = cb_ref[0]
    for j in range(RG_CONV):
        xc = xc + xp_ref[pl.ds(pad - (RG_CONV - 1) + j, t), :] * cw_ref[0, j:j + 1, :]

    row = lax.broadcasted_iota(jnp.int32, (t, HEAD_DIM), 0)
    sub = jnp.bitwise_and(row, SUBLANES - 1)
    for n in range(cb // HEAD_DIM):
        sl = slice(n * HEAD_DIM, (n + 1) * HEAD_DIM)
        xb = xc[:, sl]
        r = _sigmoid(_mm(xb, wa_ref[0, n]) + ba_ref[0, :, sl])
        i = _sigmoid(_mm(xb, wx_ref[0, n]) + bx_ref[0, :, sl])
        neg_lam = -lam_ref[0, :, sl]
        softplus = jnp.maximum(neg_lam, 0.0) + jnp.log1p(jnp.exp(-jnp.abs(neg_lam)))
        log_a = (-RG_C * softplus) * r
        a = jnp.exp(log_a)
        mult = jnp.sqrt(-jnp.tanh(log_a) * (a * a + 1.0))
        mult = jnp.where(row == 0, 1.0, mult)
        bx = mult * (i * xb)
        s = 1
        while s < SUBLANES:
            a_sh = pltpu.roll(a, s, axis=0)
            b_sh = pltpu.roll(bx, s, axis=0)
            inside = sub >= s
            bx = jnp.where(inside, a * b_sh + bx, bx)
            a = jnp.where(inside, a * a_sh, a)
            s *= 2
        a_ref[:, sl] = a
        b_ref[:, sl] = bx

    def carry_rows(v, h):
        r0 = pl.multiple_of(v * SUBLANES, SUBLANES)
        hh = a_ref[pl.ds(r0, SUBLANES), :] * h + b_ref[pl.ds(r0, SUBLANES), :]
        h_ref[pl.ds(r0, SUBLANES), :] = hh
        return jnp.broadcast_to(hh[SUBLANES - 1:SUBLANES, :], hh.shape)

    lax.fori_loop(0, t // SUBLANES, carry_rows, jnp.zeros((SUBLANES, cb), F32))
    o_ref[0] = (h_ref[...] * _silu(g_ref[0])).astype(o_ref.dtype)


def _rglru(proj3, conv_w, conv_b, w_a, b_a, w_x, b_x, lam, layer, d_rg):
    b, t, _ = proj3.shape
    depth = conv_w.shape[0]
    cb = 256
    nblk = cb // HEAD_DIM
    ncb = d_rg // cb
    vec = lambda v: v.reshape(depth, 1, d_rg)
    vspec = pl.BlockSpec((1, 1, cb), lambda i, j: (layer, 0, j))
    wspec = pl.BlockSpec((1, nblk, HEAD_DIM, HEAD_DIM), lambda i, j: (layer, j, 0, 0))
    return pl.pallas_call(
        _rglru_kernel,
        grid=(b, ncb),
        in_specs=[pl.BlockSpec((1, t, cb), lambda i, j: (i, 0, j)),
                  pl.BlockSpec((1, t, cb), lambda i, j: (i, 0, ncb + j)),
                  pl.BlockSpec((1, RG_CONV, cb), lambda i, j: (layer, 0, j)),
                  vspec, wspec, vspec, wspec, vspec, vspec],
        out_specs=pl.BlockSpec((1, t, cb), lambda i, j: (i, 0, j)),
        out_shape=jax.ShapeDtypeStruct((b, t, d_rg), MXU_DTYPE),
        scratch_shapes=[pltpu.VMEM((t + SUBLANES, cb), F32), pltpu.VMEM((t, cb), F32),
                        pltpu.VMEM((t, cb), F32), pltpu.VMEM((t, cb), F32)],
        compiler_params=_params("parallel", "parallel"),
        name="rglru",
    )(proj3, proj3, conv_w, vec(conv_b), w_a, vec(b_a), w_x, vec(b_x), vec(lam))


def _hgrn2_kernel(q_ref, f_ref, v_ref, g_ref, lbr_ref, ng_ref, o_ref,
                  qs_ref, kk_ref, b_ref, oo_ref, *, layer):
    t = q_ref.shape[1]
    ch, sb = HG_CHUNK, HG_SUB
    nsb = ch // sb
    z = f_ref[0]
    ez = jnp.exp(-jnp.abs(z))
    log_sig = jnp.minimum(z, 0.0) - jnp.log(1.0 + ez)
    inv = 1.0 / (1.0 + ez)
    sig_neg = jnp.where(z >= 0.0, ez * inv, inv)
    if layer == 0:
        log_f = log_sig
        kk = sig_neg
    else:
        raw = lbr_ref[...]
        e = jnp.exp(raw - jnp.max(raw, axis=0, keepdims=True))
        p = e / jnp.sum(e, axis=0, keepdims=True)
        lb = p[1:2]
        for j in range(2, layer + 1):
            lb = lb + p[j:j + 1]
        log_lb = jnp.log(lb)
        other = jnp.log1p(-lb) + log_sig
        log_f = jnp.maximum(log_lb, other) + jnp.log(1.0 + jnp.exp(-jnp.abs(log_lb - other)))
        kk = (1.0 - lb) * sig_neg
    width = q_ref.shape[2]
    heads = range(width // HEAD_DIM)
    row = lax.broadcasted_iota(jnp.int32, (t, width), 0)
    in_chunk = jnp.bitwise_and(row, ch - 1)
    bcum = log_f
    s = 1
    while s < ch:
        bcum = bcum + jnp.where(in_chunk >= s, pltpu.roll(bcum, s, axis=0), 0.0)
        s *= 2
    b_ref[...] = bcum * LOG2E
    kk_ref[...] = kk
    qs_ref[...] = _silu(q_ref[0])

    ones = jnp.ones((HEAD_DIM, ch), MXU_DTYPE)
    lane = lax.broadcasted_iota(jnp.int32, (sb, ch), 1)
    subrow = lax.broadcasted_iota(jnp.int32, (sb, HEAD_DIM), 0)

    def chunk_head(r0, hd, st):
        cols = slice(hd * HEAD_DIM, (hd + 1) * HEAD_DIM)
        bq = b_ref[pl.ds(r0, ch), cols]
        qc = qs_ref[pl.ds(r0, ch), cols]
        kc = kk_ref[pl.ds(r0, ch), cols]
        vc = v_ref[0, pl.ds(r0, ch), cols]
        blast = bq[ch - 1:ch, :]
        o = _mm_nt(qc * jnp.exp2(bq), st)
        a_rows = []
        diag = []
        for blk in range(nsb):
            lo = blk * sb
            b_i, q_i, k_i = bq[lo:lo + sb], qc[lo:lo + sb], kc[lo:lo + sb]
            for s_ in range(sb):
                dec = jnp.where(subrow >= s_, jnp.exp2(b_i - b_i[s_:s_ + 1]), 0.0)
                diag.append(q_i * (k_i[s_:s_ + 1] * dec))
            if blk == 0:
                a_rows.append(jnp.zeros((sb, ch), F32))
            else:
                m_i = bq[lo - 1:lo]
                qd = q_i * jnp.exp2(b_i - m_i)
                kd = kc[0:lo] * jnp.exp2(m_i - bq[0:lo])
                kd = jnp.concatenate([kd, jnp.zeros((ch - lo, HEAD_DIM), F32)], axis=0)
                a_rows.append(_mm_nt(qd, kd))
        dsum = _mm(jnp.concatenate(diag, axis=0), ones)
        for blk in range(nsb):
            acc = a_rows[blk]
            for s_ in range(sb):
                idx = blk * sb + s_
                acc = acc + jnp.where(lane == idx, dsum[idx * sb:(idx + 1) * sb], 0.0)
            a_rows[blk] = acc
        a_mat = jnp.concatenate(a_rows, axis=0)
        oo_ref[pl.ds(r0, ch), cols] = o + _mm(a_mat, vc)
        kdec = kc * jnp.exp2(blast - bq)
        return st * jnp.exp2(blast) + _mm_tn(vc, kdec)

    def chunk(c, states):
        r0 = pl.multiple_of(c * ch, ch)
        return tuple(chunk_head(r0, hd, states[hd]) for hd in heads)

    lax.fori_loop(0, t // ch, chunk, tuple(jnp.zeros((HEAD_DIM, HEAD_DIM), F32) for _ in heads), unroll=4)
    gate = _silu(g_ref[0])
    for hd in heads:
        cols = slice(hd * HEAD_DIM, (hd + 1) * HEAD_DIM)
        o = oo_ref[:, cols]
        o = o * lax.rsqrt(jnp.mean(o * o, axis=-1, keepdims=True) + RMS_EPS) * ng_ref[0]
        o_ref[0, :, cols] = (o * gate[:, cols]).astype(o_ref.dtype)


def _hgrn2(proj3, col0, lower_bounds, norm_g, layer, d_hg):
    b, t, _ = proj3.shape
    depth = lower_bounds.shape[0]
    width = 2 * HEAD_DIM
    steps = d_hg // width
    assert col0 % width == 0 and d_hg % width == 0

    def colspec(k):
        return pl.BlockSpec((1, t, width), lambda i, h: (i, 0, col0 // width + k * steps + h))

    return pl.pallas_call(
        functools.partial(_hgrn2_kernel, layer=layer),
        grid=(b, steps),
        in_specs=[colspec(0), colspec(1), colspec(2), colspec(3),
                  pl.BlockSpec((depth, width), lambda i, h: (0, h)),
                  pl.BlockSpec((1, 1, HEAD_DIM), lambda i, h: (layer, 0, 0))],
        out_specs=pl.BlockSpec((1, t, width), lambda i, h: (i, 0, h)),
        out_shape=jax.ShapeDtypeStruct((b, t, d_hg), MXU_DTYPE),
        scratch_shapes=[pltpu.VMEM((t, width), F32)] * 4,
        compiler_params=_params("parallel", "parallel"),
        name="hgrn2",
    )(proj3, proj3, proj3, proj3, lower_bounds, norm_g.reshape(depth, 1, HEAD_DIM))


def _compress_kernel(*refs):
    x_refs = refs[:2 * NSA_KV]
    pek_ref, pev_ref, w1k_ref, w2k_ref, w1v_ref, w2v_ref, ko_ref, vo_ref = refs[2 * NSA_KV:]
    nrow = x_refs[0].shape[1] // CMP_STRIDE
    row = lax.broadcasted_iota(jnp.int32, (nrow, HEAD_DIM), 0)
    branches = ((pek_ref, w1k_ref, w2k_ref, ko_ref), (pev_ref, w1v_ref, w2v_ref, vo_ref))
    for kv, (pe_ref, w1_ref, w2_ref, out_ref) in enumerate(branches):
        for g in range(NSA_KV):
            x_ref = x_refs[kv * NSA_KV + g]
            lo = jnp.concatenate(
                [x_ref[0, pl.ds(i, nrow, stride=CMP_STRIDE), :] for i in range(CMP_STRIDE)], axis=1)
            hi = pltpu.roll(lo, nrow - 1, axis=0)
            blk = jnp.concatenate([lo, hi], axis=1) + pe_ref[0]
            hid = _silu(_mm(blk, w1_ref[0]))
            out = _mm(hid, w2_ref[0])
            out_ref[0, g] = jnp.where(row < nrow - 1, out, 0.0)


def _compress(proj3, col, pe_k, pe_v, w1_k, w2_k, w1_v, w2_v, layer):
    b, t, _ = proj3.shape
    nrow = t // CMP_STRIDE
    nx = 2 * NSA_KV
    assert col % HEAD_DIM == 0
    xspec = lambda k: pl.BlockSpec((1, t, HEAD_DIM), lambda i: (i, 0, col // HEAD_DIM + k))
    depth = pe_k.shape[0]
    hidden = w1_k.shape[2]
    flat = CMP_LEN * HEAD_DIM
    pespec = pl.BlockSpec((1, 1, flat), lambda i: (layer, 0, 0))
    w1spec = pl.BlockSpec((1, flat, hidden), lambda i: (layer, 0, 0))
    w2spec = pl.BlockSpec((1, hidden, HEAD_DIM), lambda i: (layer, 0, 0))
    ospec = pl.BlockSpec((1, NSA_KV, nrow, HEAD_DIM), lambda i: (i, 0, 0, 0))
    oshape = jax.ShapeDtypeStruct((b, NSA_KV, nrow, HEAD_DIM), F32)
    return pl.pallas_call(
        _compress_kernel,
        grid=(b,),
        in_specs=[xspec(k) for k in range(nx)] + [pespec, pespec, w1spec, w2spec, w1spec, w2spec],
        out_specs=[ospec, ospec],
        out_shape=[oshape, oshape],
        compiler_params=_params("parallel"),
        name="nsa_compress",
    )(*([proj3] * nx), pe_k.reshape(depth, 1, flat), pe_v.reshape(depth, 1, flat), w1_k, w2_k, w1_v, w2_v)


def _nsa_kernel(slt_ref, kaux_ref, kauxc_ref, q_ref, gl_ref, sg_ref, ks_ref, vs_ref, kw_ref, vw_ref,
                kc_ref, vc_ref, ov_ref, ge_ref, o_ref,
                ksx_ref, kwx_ref, vsb_ref, vwb_ref, kcx_ref, vcb_ref):
    qb = pl.program_id(1)
    q0 = qb * Q_BLOCK
    q0a = pl.multiple_of(q0, Q_BLOCK)
    nq = NSA_GROUP * Q_BLOCK
    gw = NSA_GROUP * HEAD_DIM
    t_all = ks_ref.shape[1]
    n_sel = t_all // SEL_LEN
    kchunk = 512
    groups = range(NSA_KV)

    @pl.when(qb == 0)
    def _():
        for g in groups:
            cols = slice(g * HEAD_DIM, (g + 1) * HEAD_DIM)
            ksx_ref[g, :, 0:HEAD_DIM] = ks_ref[0, :, cols].astype(MXU_DTYPE)
            ksx_ref[g, :, HEAD_DIM:] = kaux_ref[...]
            kwx_ref[g, :, 0:HEAD_DIM] = kw_ref[0, :, cols].astype(MXU_DTYPE)
            kwx_ref[g, :, HEAD_DIM:] = kaux_ref[...]
            kcx_ref[g, :, 0:HEAD_DIM] = kc_ref[0, g].astype(MXU_DTYPE)
            kcx_ref[g, :, HEAD_DIM:] = kauxc_ref[...]
            vsb_ref[g] = vs_ref[0, :, cols].astype(MXU_DTYPE)
            vwb_ref[g] = vw_ref[0, :, cols].astype(MXU_DTYPE)
            vcb_ref[g] = vc_ref[0, g].astype(MXU_DTYPE)

    def stack(v):
        return jnp.concatenate([v] * NSA_GROUP, axis=0)

    def online(carry, s, v):
        m, l, acc = carry
        m_new = jnp.maximum(m, jnp.max(s, axis=1, keepdims=True))
        alpha = jnp.exp2(m - m_new)
        p = jnp.exp2(s - m_new)
        l = alpha * l + jnp.sum(p, axis=1, keepdims=True)
        acc = alpha * acc + jnp.dot(p.astype(MXU_DTYPE), v, preferred_element_type=F32)
        return m_new, l, acc

    def finish(carry):
        _, l, acc = carry
        return acc * jnp.where(l > 0.0, 1.0 / l, 0.0)

    init = (jnp.full((nq, 1), NEG_INF, F32), jnp.zeros((nq, 1), F32), jnp.zeros((nq, HEAD_DIM), F32))
    t_loc = lax.broadcasted_iota(jnp.int32, (Q_BLOCK, LANES), 0)
    lane = lax.broadcasted_iota(jnp.int32, (Q_BLOCK, LANES), 1)

    ncmp = kcx_ref.shape[1]
    pos_c = (CMP_LEN - 1) + CMP_STRIDE * lax.broadcasted_iota(jnp.int32, (Q_BLOCK, ncmp), 1)
    ok_c = stack(jnp.where(q0 + lax.broadcasted_iota(jnp.int32, (Q_BLOCK, ncmp), 0) >= pos_c, 1.0, 0.0)) > 0.5
    tri_low = stack(jnp.where(lax.broadcasted_iota(jnp.int32, (Q_BLOCK, Q_BLOCK), 1)
                              <= lax.broadcasted_iota(jnp.int32, (Q_BLOCK, Q_BLOCK), 0), 1.0, 0.0)) > 0.5
    n_full = WINDOW // SEL_LEN - 1
    win_bias = jnp.where(lane < n_sel,
                         jnp.where(lane >= qb - n_full, jnp.where(lane < qb, 0.0, MASK_VAL), MASK_VAL), 0.0)
    has_far = jnp.where(q0 >= WINDOW, 1.0, 0.0)
    ok_edge = stack(jnp.where(lane < Q_BLOCK,
                              jnp.where(lane > t_loc, has_far, 0.0),
                              jnp.where(lane - Q_BLOCK <= t_loc, 1.0, 0.0))
                    ) > 0.5
    w0 = pl.multiple_of(jnp.maximum(q0 - WINDOW, 0), Q_BLOCK)
    nblk = lax.broadcasted_iota(jnp.int32, (n_sel, Q_BLOCK), 0)
    forced = (nblk == 0) | (nblk == qb) | (nblk == qb - 1)

    q_plain, q_sel, o_cmp, o_win = [], [], [], []
    for g in groups:
        q = q_ref[0, :, g * gw:(g + 1) * gw] * (HEAD_DIM ** -0.5 * LOG2E)
        q_main = jnp.concatenate([q[:, r * HEAD_DIM:(r + 1) * HEAD_DIM] for r in range(NSA_GROUP)],
                                 axis=0).astype(MXU_DTYPE)
        slope_lanes = slt_ref[g]

        def with_aux(block_bias, q_main=q_main, slope_lanes=slope_lanes):
            return jnp.concatenate([q_main, (stack(block_bias) + slope_lanes).astype(MXU_DTYPE)], axis=1)

        qp = with_aux(jnp.zeros((Q_BLOCK, LANES), F32))
        q_plain.append(qp)

        carry = online(init, _mm_nt(with_aux(win_bias), kwx_ref[g, pl.ds(w0, WINDOW), :]),
                       vwb_ref[g, pl.ds(w0, WINDOW), :])
        k_edge = jnp.concatenate([kwx_ref[g, pl.ds(w0, Q_BLOCK), :], kwx_ref[g, pl.ds(q0a, Q_BLOCK), :]], axis=0)
        v_edge = jnp.concatenate([vwb_ref[g, pl.ds(w0, Q_BLOCK), :], vwb_ref[g, pl.ds(q0a, Q_BLOCK), :]], axis=0)
        s_edge = jnp.where(ok_edge, _mm_nt(qp, k_edge), MASK_VAL)
        o_win.append(finish(online(carry, s_edge, v_edge)))

        s_c = jnp.where(ok_c, _mm_nt(qp, kcx_ref[g]), MASK_VAL)
        p_c = jnp.exp2(s_c - jnp.maximum(jnp.max(s_c, axis=1, keepdims=True), NEG_INF))
        l_c = jnp.sum(p_c, axis=1, keepdims=True)
        p_c = p_c * jnp.where(l_c > 0.0, 1.0 / l_c, 0.0)
        o_cmp.append(jnp.dot(p_c.astype(MXU_DTYPE), vcb_ref[g], preferred_element_type=F32))

        p_sum = p_c[0:Q_BLOCK]
        for r in range(1, NSA_GROUP):
            p_sum = p_sum + p_c[r * Q_BLOCK:(r + 1) * Q_BLOCK]
        imp = lax.dot_general(ov_ref[...], p_sum, (((1,), (1,)), ((), ())),
                              precision=lax.Precision.HIGHEST, preferred_element_type=F32)[0:n_sel]
        imp = jnp.where(nblk > qb, NEG_INF, jnp.where(forced, FORCE_SCORE, imp))
        rank = jnp.zeros(imp.shape, F32)
        for m_ in range(n_sel):
            other = imp[m_:m_ + 1, :]
            rank = rank + jnp.where(nblk > m_, jnp.where(other >= imp, 1.0, 0.0), jnp.where(other > imp, 1.0, 0.0))
        sel_bias = jnp.where(rank < float(min(SEL_TOPN, n_sel)), jnp.where(nblk < qb, 0.0, MASK_VAL), MASK_VAL)
        sel_bias = jnp.concatenate([sel_bias, jnp.zeros((LANES - n_sel, Q_BLOCK), F32)], axis=0).T
        q_sel.append(with_aux(sel_bias))

    def sel_branch(n_chunks):
        def run():
            outs = []
            for g in groups:
                carry = init
                for ci in range(n_chunks):
                    rows = slice(ci * kchunk, (ci + 1) * kchunk)
                    carry = online(carry, _mm_nt(q_sel[g], ksx_ref[g, rows, :]), vsb_ref[g, rows, :])
                s_own = jnp.where(tri_low, _mm_nt(q_plain[g], ksx_ref[g, pl.ds(q0a, Q_BLOCK), :]), MASK_VAL)
                outs.append(finish(online(carry, s_own, vsb_ref[g, pl.ds(q0a, Q_BLOCK), :])))
            return tuple(outs)
        return run

    max_chunks = (t_all - Q_BLOCK + kchunk - 1) // kchunk
    o_sel = lax.switch((q0 + kchunk - 1) // kchunk, [sel_branch(n) for n in range(max_chunks + 1)])

    def unstack(o):
        return jnp.concatenate([o[r * Q_BLOCK:(r + 1) * Q_BLOCK] for r in range(NSA_GROUP)], axis=1)

    sig = _sigmoid(gl_ref[0])
    sig_hi = sig.astype(MXU_DTYPE)
    sig_lo = (sig - sig_hi.astype(F32)).astype(MXU_DTYPE)
    for g in groups:
        out = None
        for i, branch in enumerate((o_cmp[g], o_sel[g], o_win[g])):
            gate = (jnp.dot(sig_hi, ge_ref[g, i], preferred_element_type=F32)
                    + jnp.dot(sig_lo, ge_ref[g, i], preferred_element_type=F32))
            term = gate * unstack(branch)
            out = term if out is None else out + term
        cols = slice(g * gw, (g + 1) * gw)
        o_ref[0, :, cols] = (out * _silu(sg_ref[0, :, cols])).astype(o_ref.dtype)


def _split3(v):
    bf = jnp.bfloat16
    hi = v.astype(bf).astype(np.float64)
    mid = (v - hi).astype(bf).astype(np.float64)
    lo = (v - hi - mid).astype(bf).astype(np.float64)
    return hi, mid, lo


def _slope_lanes():
    heads = NSA_KV * NSA_GROUP
    h = np.arange(1, heads + 1, dtype=np.float32)
    s = np.power(np.float32(2.0), -8.0 * h / heads).astype(np.float32).astype(np.float64) * LOG2E
    parts = _split3(s)
    out = np.zeros((heads, LANES), np.float32)
    for rep in range(2):
        for i, part in enumerate(parts):
            out[:, POS_LANE + 3 * rep + i] = part
    out = np.repeat(out.reshape(NSA_KV, NSA_GROUP, 1, LANES), Q_BLOCK, axis=2)
    return jnp.asarray(out.reshape(NSA_KV, NSA_GROUP * Q_BLOCK, LANES))


def _key_aux(pos, blocks):
    out = np.zeros((pos.shape[0], LANES), np.float32)
    if blocks:
        out[np.arange(pos.shape[0]), pos // SEL_LEN] = 1.0
    out[:, POS_LANE:POS_LANE + 3] = (SEL_LEN * (pos // SEL_LEN))[:, None]
    out[:, POS_LANE + 3:POS_LANE + 6] = (pos % SEL_LEN)[:, None]
    return jnp.asarray(out, dtype=MXU_DTYPE)


def _overlap_t(t, ncmp_rows):
    n_cmp = (t - CMP_LEN) // CMP_STRIDE + 1
    n_sel = t // SEL_LEN
    s_c = np.arange(n_cmp) * CMP_STRIDE
    s_s = np.arange(n_sel) * SEL_LEN
    ov = np.clip(np.minimum(s_c[:, None] + CMP_LEN, s_s[None, :] + SEL_LEN)
                 - np.maximum(s_c[:, None], s_s[None, :]), 0, None).astype(np.float32) / CMP_LEN
    out = np.zeros((LANES, ncmp_rows), np.float32)
    out[:n_sel, :n_cmp] = ov.T
    return jnp.asarray(out)


def _gate_expand():
    width = NSA_GROUP * HEAD_DIM
    e = np.zeros((NSA_KV, 3, LANES, width), np.float32)
    for g in range(NSA_KV):
        for r in range(NSA_GROUP):
            for i in range(3):
                e[g, i, (g * NSA_GROUP + r) * 3 + i, r * HEAD_DIM:(r + 1) * HEAD_DIM] = 1.0
    return jnp.asarray(e, dtype=MXU_DTYPE)


def _nsa(proj3, cols, k_cmp, v_cmp):
    b, t, _ = proj3.shape
    assert t // SEL_LEN <= POS_LANE and t % 512 == 0 and Q_BLOCK == SEL_LEN
    gw = NSA_GROUP * HEAD_DIM
    width = NSA_KV * gw
    kvw = NSA_KV * HEAD_DIM
    ncmp_rows = k_cmp.shape[2]
    q_c, ks_c, kw_c, gl_c, sg_c = cols
    assert q_c % width == 0 and sg_c % width == 0 and ks_c % kvw == 0 and kw_c % kvw == 0

    def kvspec(c, is_v):
        return pl.BlockSpec((1, t, kvw), lambda i, j: (i, 0, c // kvw + (1 if is_v else 0)))

    def whole(*shape):
        return pl.BlockSpec(shape, lambda i, j: (0,) * len(shape))

    cmpspec = pl.BlockSpec((1, NSA_KV, ncmp_rows, HEAD_DIM), lambda i, j: (i, 0, 0, 0))
    cmp_pos = (CMP_LEN - 1) + CMP_STRIDE * np.arange(ncmp_rows)
    return pl.pallas_call(
        _nsa_kernel,
        grid=(b, t // Q_BLOCK),
        in_specs=[whole(NSA_KV, NSA_GROUP * Q_BLOCK, LANES), whole(t, LANES), whole(ncmp_rows, LANES),
                  pl.BlockSpec((1, Q_BLOCK, width), lambda i, j: (i, j, q_c // width)),
                  pl.BlockSpec((1, Q_BLOCK, LANES), lambda i, j: (i, j, gl_c // LANES)),
                  pl.BlockSpec((1, Q_BLOCK, width), lambda i, j: (i, j, sg_c // width)),
                  kvspec(ks_c, False), kvspec(ks_c, True), kvspec(kw_c, False), kvspec(kw_c, True),
                  cmpspec, cmpspec, whole(LANES, ncmp_rows), whole(NSA_KV, 3, LANES, gw)],
        out_specs=pl.BlockSpec((1, Q_BLOCK, width), lambda i, j: (i, j, 0)),
        out_shape=jax.ShapeDtypeStruct((b, t, width), MXU_DTYPE),
        scratch_shapes=[pltpu.VMEM((NSA_KV, t, 2 * HEAD_DIM), MXU_DTYPE),
                        pltpu.VMEM((NSA_KV, t, 2 * HEAD_DIM), MXU_DTYPE),
                        pltpu.VMEM((NSA_KV, t, HEAD_DIM), MXU_DTYPE), pltpu.VMEM((NSA_KV, t, HEAD_DIM), MXU_DTYPE),
                        pltpu.VMEM((NSA_KV, ncmp_rows, 2 * HEAD_DIM), MXU_DTYPE),
                        pltpu.VMEM((NSA_KV, ncmp_rows, HEAD_DIM), MXU_DTYPE)],
        compiler_params=_params("parallel", "arbitrary"),
        name="nsa_attn",
    )(_slope_lanes(), _key_aux(np.arange(t), True), _key_aux(cmp_pos, False),
      proj3, proj3, proj3, proj3, proj3, proj3, proj3, k_cmp, v_cmp, _overlap_t(t, ncmp_rows), _gate_expand())


def _out_ln_kernel(a0_ref, a1_ref, a2_ref, a3_ref, w0_ref, w1_ref, w2_ref, w3_ref,
                   x_ref, gate_ref, lng_ref, lnb_ref, shift_ref, scale_ref, *rest, alpha, emit_u):
    if emit_u:
        o_ref, u_ref, pre_ref, c_ref, s1_ref, s2_ref = rest
    else:
        (o_ref, pre_ref, c_ref, s1_ref, s2_ref), u_ref = rest, None
    i, j = pl.program_id(0), pl.program_id(1)
    n_row, nj = pl.num_programs(0) - 1, pl.num_programs(1)
    tn = o_ref.shape[1]
    slot = lax.rem(i, 2)

    @pl.when((i == 0) & (j == 0))
    def _():
        c_ref[...] = jnp.zeros(c_ref.shape, F32)
        s1_ref[...] = jnp.zeros(s1_ref.shape, F32)
        s2_ref[...] = jnp.zeros(s2_ref.shape, F32)

    def build():
        y = (jnp.dot(a0_ref[...], w0_ref[0], preferred_element_type=F32)
             + jnp.dot(a1_ref[...], w1_ref[0], preferred_element_type=F32)
             + jnp.dot(a2_ref[...], w2_ref[0], preferred_element_type=F32)
             + jnp.dot(a3_ref[...], w3_ref[0], preferred_element_type=F32))
        pre_ref[slot * nj + j] = alpha * x_ref[...] + (1.0 + gate_ref[0, 0]) * y

    def add_stats(sidx, jj):
        v = pre_ref[sidx * nj + jj]
        keep = jnp.where(jj == 0, 0.0, 1.0)
        c = keep * c_ref[sidx] + (1.0 - keep) * (v.sum(axis=-1, keepdims=True) / tn)
        c_ref[sidx] = c
        dev = v - c
        s1_ref[sidx] = keep * s1_ref[sidx] + dev.sum(axis=-1, keepdims=True)
        s2_ref[sidx] = keep * s2_ref[sidx] + (dev * dev).sum(axis=-1, keepdims=True)

    def normalise():
        d = nj * tn
        prev = 1 - slot
        off = s1_ref[prev] / d
        mu = c_ref[prev] + off
        inv = lax.rsqrt(s2_ref[prev] / d - off * off + LN_EPS)
        out = (pre_ref[prev * nj + j] - mu) * inv * lng_ref[0] + lnb_ref[0]
        o_ref[...] = out
        if emit_u:
            u_ref[...] = (out * (1.0 + scale_ref[0, 0]) + shift_ref[0, 0]).astype(u_ref.dtype)

    @pl.when(i == 0)
    def _():
        pl.when(j > 0)(lambda: add_stats(slot, j - 1))
        build()

    @pl.when((i > 0) & (i < n_row))
    def _():
        add_stats(jnp.where(j > 0, slot, 1 - slot), jnp.where(j > 0, j - 1, nj - 1))
        normalise()
        build()

    @pl.when(i == n_row)
    def _():
        pl.when(j == 0)(lambda: add_stats(1 - slot, nj - 1))
        normalise()


def _out_ln(y_rg, y_nsa, y_hg, w_out_b, x2, mod4, ln_g, ln_b, layer, alpha, t, emit_u):
    m, d = x2.shape
    depth = w_out_b.shape[0]
    kb = y_rg.shape[1]
    assert y_nsa.shape[1] == 2 * kb and y_hg.shape[1] == kb and d == 4 * kb
    tm, tn = 512, 1024
    nj, n_row = d // tn, m // tm
    assert t % tm == 0
    mrow = lambda i: jnp.minimum(i, n_row - 1)
    mcol = lambda i, j: jnp.where(i < n_row, j, nj - 1)
    orow = lambda i: jnp.maximum(i - 1, 0)
    ocol = lambda i, j: jnp.where(i > 0, j, 0)
    aspec = lambda c: pl.BlockSpec((tm, kb), lambda i, j: (mrow(i), c))
    wspec = lambda r: pl.BlockSpec((1, kb, tn), lambda i, j: (layer, r, mcol(i, j)))
    vspec = pl.BlockSpec((1, 1, tn), lambda i, j: (layer, 0, ocol(i, j)))
    nxt = min(layer + 1, depth - 1)
    modspec = lambda part: pl.BlockSpec(
        (1, 1, 1, tn), lambda i, j: (nxt, (orow(i) * tm) // t, 0, part * nj + ocol(i, j)))
    ospec = pl.BlockSpec((tm, tn), lambda i, j: (orow(i), ocol(i, j)))
    oshape = jax.ShapeDtypeStruct((m, d), F32)
    return pl.pallas_call(
        functools.partial(_out_ln_kernel, alpha=alpha, emit_u=emit_u),
        grid=(n_row + 1, nj),
        in_specs=[aspec(0), aspec(0), aspec(1), aspec(0), wspec(0), wspec(1), wspec(2), wspec(3),
                  pl.BlockSpec((tm, tn), lambda i, j: (mrow(i), mcol(i, j))),
                  pl.BlockSpec((1, 1, 1, tn), lambda i, j: (layer, (mrow(i) * tm) // t, 0, 2 * nj + mcol(i, j))),
                  vspec, vspec, modspec(0), modspec(1)],
        out_specs=[ospec, ospec] if emit_u else ospec,
        out_shape=[oshape, jax.ShapeDtypeStruct((m, d), MXU_DTYPE)] if emit_u else oshape,
        scratch_shapes=[pltpu.VMEM((2 * nj, tm, tn), F32)] + [pltpu.VMEM((2, tm, 1), F32)] * 3,
        compiler_params=_params("arbitrary", "arbitrary"),
        name="out_ln",
    )(y_rg, y_nsa, y_nsa, y_hg, w_out_b, w_out_b, w_out_b, w_out_b, x2, mod4,
      ln_g.reshape(depth, 1, d), ln_b.reshape(depth, 1, d), mod4, mod4)


def kernel(x, c, w_ada, b_ada, w_in, rg_conv_w, rg_conv_b, rg_w_a, rg_b_a, rg_w_x, rg_b_x, rg_lambda,
           nsa_pe_k, nsa_pe_v, nsa_cmp_w1_k, nsa_cmp_w2_k, nsa_cmp_w1_v, nsa_cmp_w2_v,
           hg_lower_bounds, hg_norm_g, w_out, ln_g, ln_b):
    b, t, d = x.shape
    depth = w_ada.shape[0]
    m = b * t
    d_rg, d_nsa, d_hg = d // 4, d // 2, d // 4
    kv_cols = 2 * NSA_KV * HEAD_DIM
    n_gl = 3 * NSA_KV * NSA_GROUP
    alpha = (2.0 * depth) ** 0.25
    assert d_nsa == NSA_KV * NSA_GROUP * HEAD_DIM and b <= SUBLANES

    gl_pad = 512
    c_q = 2 * d_rg
    c_kvc = c_q + d_nsa
    c_kvs = c_kvc + kv_cols
    c_kvw = c_kvs + kv_cols
    c_gl = c_kvw + kv_cols
    c_sg = c_gl + gl_pad
    c_hg = c_sg + d_nsa
    w_in_p = _prep_w_in(jnp.swapaxes(w_in, 1, 2), c_gl, n_gl, gl_pad)
    w_out_b = w_out.astype(MXU_DTYPE)
    w1_k, w1_v = nsa_cmp_w1_k.astype(MXU_DTYPE), nsa_cmp_w1_v.astype(MXU_DTYPE)

    c_pad = jnp.pad(c, ((0, SUBLANES - b), (0, 0)))
    mod4 = _ada(c_pad, w_ada, b_ada).reshape(depth, SUBLANES, 1, 3 * d)

    u = _modulate(x, mod4, 0).reshape(m, d)
    x = x.reshape(m, d)
    for layer in range(depth):
        proj3 = _in_proj(u, w_in_p, layer).reshape(b, t, -1)
        y_rg = _rglru(proj3, rg_conv_w, rg_conv_b, rg_w_a, rg_b_a, rg_w_x, rg_b_x, rg_lambda, layer, d_rg)
        k_cmp, v_cmp = _compress(proj3, c_kvc, nsa_pe_k, nsa_pe_v, w1_k, nsa_cmp_w2_k, w1_v, nsa_cmp_w2_v, layer)
        y_nsa = _nsa(proj3, (c_q, c_kvs, c_kvw, c_gl, c_sg), k_cmp, v_cmp)
        y_hg = _hgrn2(proj3, c_hg, hg_lower_bounds, hg_norm_g, layer, d_hg)
        emit_u = layer + 1 < depth
        res = _out_ln(y_rg.reshape(m, d_rg), y_nsa.reshape(m, d_nsa), y_hg.reshape(m, d_hg), w_out_b,
                      x, mod4, ln_g, ln_b, layer, alpha, t, emit_u)
        x, u = res if emit_u else (res, None)
    return x.reshape(b, t, d)
```

```python
import functools

import numpy as np
import jax
import jax.numpy as jnp
from jax import lax
from jax.experimental import pallas as pl
from jax.experimental.pallas import tpu as pltpu

F32 = jnp.float32
MXU_DTYPE = jnp.bfloat16

HEAD_DIM = 128
RG_CONV = 4
RG_C = 8.0
NSA_KV = 2
NSA_GROUP = 8
CMP_LEN = 32
CMP_STRIDE = 16
SEL_LEN = 64
SEL_TOPN = 16
WINDOW = 512
Q_BLOCK = 64
HG_CHUNK = 64
HG_SUB = 8
LN_EPS = 1e-5
RMS_EPS = 1e-6
NEG_INF = -1e30
FORCE_SCORE = 1e9
MASK_VAL = -(2.0 ** 100)
LOG2E = 1.4426950408889634
POS_LANE = 32

LANES = 128
SUBLANES = 8
VMEM_LIMIT = 56 * 1024 * 1024


def _mm(a, b):
    return jnp.dot(a.astype(MXU_DTYPE), b.astype(MXU_DTYPE), preferred_element_type=F32)


def _mm_nt(a, b):
    return lax.dot_general(a.astype(MXU_DTYPE), b.astype(MXU_DTYPE),
                           (((1,), (1,)), ((), ())), preferred_element_type=F32)


def _mm_tn(a, b):
    return lax.dot_general(a.astype(MXU_DTYPE), b.astype(MXU_DTYPE),
                           (((0,), (0,)), ((), ())), preferred_element_type=F32)


def _sigmoid(v):
    return jax.nn.sigmoid(v)


def _silu(v):
    return v * jax.nn.sigmoid(v)


def _params(*semantics):
    return pltpu.CompilerParams(dimension_semantics=semantics, vmem_limit_bytes=VMEM_LIMIT)


def _ada_kernel(c_ref, w_ref, b_ref, o_ref):
    o_ref[0] = _mm(c_ref[...], w_ref[0]) + b_ref[0]


def _ada(c_pad, w_ada, b_ada):
    depth, d, n3 = w_ada.shape
    rows = c_pad.shape[0]
    tn = 1024
    return pl.pallas_call(
        _ada_kernel,
        grid=(depth, n3 // tn),
        in_specs=[pl.BlockSpec((rows, d), lambda l, j: (0, 0)),
                  pl.BlockSpec((1, d, tn), lambda l, j: (l, 0, j)),
                  pl.BlockSpec((1, 1, tn), lambda l, j: (l, 0, j))],
        out_specs=pl.BlockSpec((1, rows, tn), lambda l, j: (l, 0, j)),
        out_shape=jax.ShapeDtypeStruct((depth, rows, n3), F32),
        compiler_params=_params("parallel", "parallel"),
        name="ada",
    )(c_pad, w_ada, b_ada.reshape(depth, 1, n3))


def _modulate_kernel(x_ref, shift_ref, scale_ref, o_ref):
    o_ref[0] = (x_ref[0] * (1.0 + scale_ref[0, 0]) + shift_ref[0, 0]).astype(o_ref.dtype)


def _modulate(x, mod4, layer):
    b, t, d = x.shape
    tt = 512
    return pl.pallas_call(
        _modulate_kernel,
        grid=(b, t // tt),
        in_specs=[pl.BlockSpec((1, tt, d), lambda i, j: (i, j, 0)),
                  pl.BlockSpec((1, 1, 1, d), lambda i, j: (layer, i, 0, 0)),
                  pl.BlockSpec((1, 1, 1, d), lambda i, j: (layer, i, 0, 1))],
        out_specs=pl.BlockSpec((1, tt, d), lambda i, j: (i, j, 0)),
        out_shape=jax.ShapeDtypeStruct((b, t, d), MXU_DTYPE),
        compiler_params=_params("parallel", "parallel"),
        name="modulate",
    )(x, mod4, mod4)


def _prep_w_in_kernel(w_ref, o_ref, prev_ref, *, gl_blk, n_gl):
    j = pl.program_id(1)
    tn = w_ref.shape[1]

    @pl.when(j < gl_blk)
    def _():
        o_ref[0] = w_ref[0].T.astype(o_ref.dtype)

    @pl.when(j == gl_blk)
    def _():
        row = lax.broadcasted_iota(jnp.int32, w_ref.shape[1:], 0)
        o_ref[0] = jnp.where(row < n_gl, w_ref[0], 0.0).T.astype(o_ref.dtype)

    @pl.when(j > gl_blk)
    def _():
        tile = jnp.concatenate([prev_ref[n_gl:tn, :], w_ref[0, 0:n_gl, :]], axis=0)
        o_ref[0] = tile.T.astype(o_ref.dtype)

    prev_ref[...] = w_ref[0]


def _prep_w_in(w_in_t, c_gl, n_gl, tn):
    depth, n, d = w_in_t.shape
    assert c_gl % tn == 0 and (n - n_gl) % tn == 0 and n_gl % SUBLANES == 0
    n_out = n - n_gl + tn
    return pl.pallas_call(
        functools.partial(_prep_w_in_kernel, gl_blk=c_gl // tn, n_gl=n_gl),
        grid=(depth, n_out // tn),
        in_specs=[pl.BlockSpec((1, tn, d), lambda l, j: (l, j, 0))],
        out_specs=pl.BlockSpec((1, d, tn), lambda l, j: (l, 0, j)),
        out_shape=jax.ShapeDtypeStruct((depth, d, n_out), MXU_DTYPE),
        scratch_shapes=[pltpu.VMEM((tn, d), F32)],
        compiler_params=_params("arbitrary", "arbitrary"),
        name="prep_w_in",
    )(w_in_t)


def _in_proj_kernel(a_ref, w_ref, o_ref):
    o_ref[...] = jnp.dot(a_ref[...], w_ref[0], preferred_element_type=F32)


def _in_proj(u, w_in_p, layer):
    m, d = u.shape
    n = w_in_p.shape[2]
    tm, tn = 1024, 1024
    return pl.pallas_call(
        _in_proj_kernel,
        grid=(m // tm, n // tn),
        in_specs=[pl.BlockSpec((tm, d), lambda i, j: (i, 0)),
                  pl.BlockSpec((1, d, tn), lambda i, j: (layer, 0, j))],
        out_specs=pl.BlockSpec((tm, tn), lambda i, j: (i, j)),
        out_shape=jax.ShapeDtypeStruct((m, n), F32),
        compiler_params=_params("parallel", "parallel"),
        name="in_proj",
    )(u, w_in_p)


def _rglru_kernel(x_ref, g_ref, cw_ref, cb_ref, wa_ref, ba_ref, wx_ref, bx_ref, lam_ref, o_ref,
                  xp_ref, a_ref, b_ref, h_ref):
    t, cb = x_ref.shape[1], x_ref.shape[2]
    pad = SUBLANES
    xp_ref[0:pad, :] = jnp.zeros((pad, cb), F32)
    xp_ref[pad:, :] = x_ref[0]
    xc = cb_ref[0]
    for j in range(RG_CONV):
        xc = xc + xp_ref[pl.ds(pad - (RG_CONV - 1) + j, t), :] * cw_ref[0, j:j + 1, :]

    row = lax.broadcasted_iota(jnp.int32, (t, HEAD_DIM), 0)
    sub = jnp.bitwise_and(row, SUBLANES - 1)
    for n in range(cb // HEAD_DIM):
        sl = slice(n * HEAD_DIM, (n + 1) * HEAD_DIM)
        xb = xc[:, sl]
        r = _sigmoid(_mm(xb, wa_ref[0, n]) + ba_ref[0, :, sl])
        i = _sigmoid(_mm(xb, wx_ref[0, n]) + bx_ref[0, :, sl])
        neg_lam = -lam_ref[0, :, sl]
        softplus = jnp.maximum(neg_lam, 0.0) + jnp.log1p(jnp.exp(-jnp.abs(neg_lam)))
        log_a = (-RG_C * softplus) * r
        a = jnp.exp(log_a)
        mult = jnp.sqrt(-jnp.tanh(log_a) * (a * a + 1.0))
        mult = jnp.where(row == 0, 1.0, mult)
        bx = mult * (i * xb)
        s = 1
        while s < SUBLANES:
            a_sh = pltpu.roll(a, s, axis=0)
            b_sh = pltpu.roll(bx, s, axis=0)
            inside = sub >= s
            bx = jnp.where(inside, a * b_sh + bx, bx)
            a = jnp.where(inside, a * a_sh, a)
            s *= 2
        a_ref[:, sl] = a
        b_ref[:, sl] = bx

    def carry_rows(v, h):
        r0 = pl.multiple_of(v * SUBLANES, SUBLANES)
        hh = a_ref[pl.ds(r0, SUBLANES), :] * h + b_ref[pl.ds(r0, SUBLANES), :]
        h_ref[pl.ds(r0, SUBLANES), :] = hh
        return jnp.broadcast_to(hh[SUBLANES - 1:SUBLANES, :], hh.shape)

    lax.fori_loop(0, t // SUBLANES, carry_rows, jnp.zeros((SUBLANES, cb), F32))
    o_ref[0] = (h_ref[...] * _silu(g_ref[0])).astype(o_ref.dtype)


def _rglru(proj3, conv_w, conv_b, w_a, b_a, w_x, b_x, lam, layer, d_rg):
    b, t, _ = proj3.shape
    depth = conv_w.shape[0]
    cb = 256
    nblk = cb // HEAD_DIM
    ncb = d_rg // cb
    vec = lambda v: v.reshape(depth, 1, d_rg)
    vspec = pl.BlockSpec((1, 1, cb), lambda i, j: (layer, 0, j))
    wspec = pl.BlockSpec((1, nblk, HEAD_DIM, HEAD_DIM), lambda i, j: (layer, j, 0, 0))
    return pl.pallas_call(
        _rglru_kernel,
        grid=(b, ncb),
        in_specs=[pl.BlockSpec((1, t, cb), lambda i, j: (i, 0, j)),
                  pl.BlockSpec((1, t, cb), lambda i, j: (i, 0, ncb + j)),
                  pl.BlockSpec((1, RG_CONV, cb), lambda i, j: (layer, 0, j)),
                  vspec, wspec, vspec, wspec, vspec, vspec],
        out_specs=pl.BlockSpec((1, t, cb), lambda i, j: (i, 0, j)),
        out_shape=jax.ShapeDtypeStruct((b, t, d_rg), MXU_DTYPE),
        scratch_shapes=[pltpu.VMEM((t + SUBLANES, cb), F32), pltpu.VMEM((t, cb), F32),
                        pltpu.VMEM((t, cb), F32), pltpu.VMEM((t, cb), F32)],
        compiler_params=_params("parallel", "parallel"),
        name="rglru",
    )(proj3, proj3, conv_w, vec(conv_b), w_a, vec(b_a), w_x, vec(b_x), vec(lam))


def _hgrn2_kernel(q_ref, f_ref, v_ref, g_ref, lbr_ref, ng_ref, o_ref,
                  qs_ref, kk_ref, b_ref, oo_ref, *, layer):
    t = q_ref.shape[1]
    ch, sb = HG_CHUNK, HG_SUB
    nsb = ch // sb
    z = f_ref[0]
    ez = jnp.exp(-jnp.abs(z))
    log_sig = jnp.minimum(z, 0.0) - jnp.log(1.0 + ez)
    inv = 1.0 / (1.0 + ez)
    sig_neg = jnp.where(z >= 0.0, ez * inv, inv)
    if layer == 0:
        log_f = log_sig
        kk = sig_neg
    else:
        raw = lbr_ref[...]
        e = jnp.exp(raw - jnp.max(raw, axis=0, keepdims=True))
        p = e / jnp.sum(e, axis=0, keepdims=True)
        lb = p[1:2]
        for j in range(2, layer + 1):
            lb = lb + p[j:j + 1]
        log_lb = jnp.log(lb)
        other = jnp.log1p(-lb) + log_sig
        log_f = jnp.maximum(log_lb, other) + jnp.log(1.0 + jnp.exp(-jnp.abs(log_lb - other)))
        kk = (1.0 - lb) * sig_neg
    width = q_ref.shape[2]
    heads = range(width // HEAD_DIM)
    row = lax.broadcasted_iota(jnp.int32, (t, width), 0)
    in_chunk = jnp.bitwise_and(row, ch - 1)
    bcum = log_f
    s = 1
    while s < ch:
        bcum = bcum + jnp.where(in_chunk >= s, pltpu.roll(bcum, s, axis=0), 0.0)
        s *= 2
    b_ref[...] = bcum * LOG2E
    kk_ref[...] = kk
    qs_ref[...] = _silu(q_ref[0])

    ones = jnp.ones((HEAD_DIM, ch), MXU_DTYPE)
    lane = lax.broadcasted_iota(jnp.int32, (sb, ch), 1)
    subrow = lax.broadcasted_iota(jnp.int32, (sb, HEAD_DIM), 0)

    def chunk_head(r0, hd, st):
        cols = slice(hd * HEAD_DIM, (hd + 1) * HEAD_DIM)
        bq = b_ref[pl.ds(r0, ch), cols]
        qc = qs_ref[pl.ds(r0, ch), cols]
        kc = kk_ref[pl.ds(r0, ch), cols]
        vc = v_ref[0, pl.ds(r0, ch), cols]
        blast = bq[ch - 1:ch, :]
        o = _mm_nt(qc * jnp.exp2(bq), st)
        a_rows = []
        diag = []
        for blk in range(nsb):
            lo = blk * sb
            b_i, q_i, k_i = bq[lo:lo + sb], qc[lo:lo + sb], kc[lo:lo + sb]
            for s_ in range(sb):
                dec = jnp.where(subrow >= s_, jnp.exp2(b_i - b_i[s_:s_ + 1]), 0.0)
                diag.append(q_i * (k_i[s_:s_ + 1] * dec))
            if blk == 0:
                a_rows.append(jnp.zeros((sb, ch), F32))
            else:
                m_i = bq[lo - 1:lo]
                qd = q_i * jnp.exp2(b_i - m_i)
                kd = kc[0:lo] * jnp.exp2(m_i - bq[0:lo])
                kd = jnp.concatenate([kd, jnp.zeros((ch - lo, HEAD_DIM), F32)], axis=0)
                a_rows.append(_mm_nt(qd, kd))
        dsum = _mm(jnp.concatenate(diag, axis=0), ones)
        for blk in range(nsb):
            acc = a_rows[blk]
            for s_ in range(sb):
                idx = blk * sb + s_
                acc = acc + jnp.where(lane == idx, dsum[idx * sb:(idx + 1) * sb], 0.0)
            a_rows[blk] = acc
        a_mat = jnp.concatenate(a_rows, axis=0)
        oo_ref[pl.ds(r0, ch), cols] = o + _mm(a_mat, vc)
        kdec = kc * jnp.exp2(blast - bq)
        return st * jnp.exp2(blast) + _mm_tn(vc, kdec)

    def chunk(c, states):
        r0 = pl.multiple_of(c * ch, ch)
        return tuple(chunk_head(r0, hd, states[hd]) for hd in heads)

    lax.fori_loop(0, t // ch, chunk, tuple(jnp.zeros((HEAD_DIM, HEAD_DIM), F32) for _ in heads), unroll=4)
    gate = _silu(g_ref[0])
    for hd in heads:
        cols = slice(hd * HEAD_DIM, (hd + 1) * HEAD_DIM)
        o = oo_ref[:, cols]
        o = o * lax.rsqrt(jnp.mean(o * o, axis=-1, keepdims=True) + RMS_EPS) * ng_ref[0]
        o_ref[0, :, cols] = (o * gate[:, cols]).astype(o_ref.dtype)


def _hgrn2(proj3, col0, lower_bounds, norm_g, layer, d_hg):
    b, t, _ = proj3.shape
    depth = lower_bounds.shape[0]
    width = 2 * HEAD_DIM
    steps = d_hg // width
    assert col0 % width == 0 and d_hg % width == 0

    def colspec(k):
        return pl.BlockSpec((1, t, width), lambda i, h: (i, 0, col0 // width + k * steps + h))

    return pl.pallas_call(
        functools.partial(_hgrn2_kernel, layer=layer),
        grid=(b, steps),
        in_specs=[colspec(0), colspec(1), colspec(2), colspec(3),
                  pl.BlockSpec((depth, width), lambda i, h: (0, h)),
                  pl.BlockSpec((1, 1, HEAD_DIM), lambda i, h: (layer, 0, 0))],
        out_specs=pl.BlockSpec((1, t, width), lambda i, h: (i, 0, h)),
        out_shape=jax.ShapeDtypeStruct((b, t, d_hg), MXU_DTYPE),
        scratch_shapes=[pltpu.VMEM((t, width), F32)] * 4,
        compiler_params=_params("parallel", "parallel"),
        name="hgrn2",
    )(proj3, proj3, proj3, proj3, lower_bounds, norm_g.reshape(depth, 1, HEAD_DIM))


def _compress_kernel(*refs):
    x_refs = refs[:2 * NSA_KV]
    pek_ref, pev_ref, w1k_ref, w2k_ref, w1v_ref, w2v_ref, ko_ref, vo_ref = refs[2 * NSA_KV:]
    nrow = x_refs[0].shape[1] // CMP_STRIDE
    row = lax.broadcasted_iota(jnp.int32, (nrow, HEAD_DIM), 0)
    branches = ((pek_ref, w1k_ref, w2k_ref, ko_ref), (pev_ref, w1v_ref, w2v_ref, vo_ref))
    for kv, (pe_ref, w1_ref, w2_ref, out_ref) in enumerate(branches):
        for g in range(NSA_KV):
            x_ref = x_refs[kv * NSA_KV + g]
            lo = jnp.concatenate(
                [x_ref[0, pl.ds(i, nrow, stride=CMP_STRIDE), :] for i in range(CMP_STRIDE)], axis=1)
            hi = pltpu.roll(lo, nrow - 1, axis=0)
            blk = jnp.concatenate([lo, hi], axis=1) + pe_ref[0]
            hid = _silu(_mm(blk, w1_ref[0]))
            out = _mm(hid, w2_ref[0])
            out_ref[0, g] = jnp.where(row < nrow - 1, out, 0.0)


def _compress(proj3, col, pe_k, pe_v, w1_k, w2_k, w1_v, w2_v, layer):
    b, t, _ = proj3.shape
    nrow = t // CMP_STRIDE
    nx = 2 * NSA_KV
    assert col % HEAD_DIM == 0
    xspec = lambda k: pl.BlockSpec((1, t, HEAD_DIM), lambda i: (i, 0, col // HEAD_DIM + k))
    depth = pe_k.shape[0]
    hidden = w1_k.shape[2]
    flat = CMP_LEN * HEAD_DIM
    pespec = pl.BlockSpec((1, 1, flat), lambda i: (layer, 0, 0))
    w1spec = pl.BlockSpec((1, flat, hidden), lambda i: (layer, 0, 0))
    w2spec = pl.BlockSpec((1, hidden, HEAD_DIM), lambda i: (layer, 0, 0))
    ospec = pl.BlockSpec((1, NSA_KV, nrow, HEAD_DIM), lambda i: (i, 0, 0, 0))
    oshape = jax.ShapeDtypeStruct((b, NSA_KV, nrow, HEAD_DIM), F32)
    return pl.pallas_call(
        _compress_kernel,
        grid=(b,),
        in_specs=[xspec(k) for k in range(nx)] + [pespec, pespec, w1spec, w2spec, w1spec, w2spec],
        out_specs=[ospec, ospec],
        out_shape=[oshape, oshape],
        compiler_params=_params("parallel"),
        name="nsa_compress",
    )(*([proj3] * nx), pe_k.reshape(depth, 1, flat), pe_v.reshape(depth, 1, flat), w1_k, w2_k, w1_v, w2_v)


def _nsa_kernel(slt_ref, kaux_ref, kauxc_ref, q_ref, gl_ref, sg_ref, ks_ref, vs_ref, kw_ref, vw_ref,
                kc_ref, vc_ref, ov_ref, ge_ref, o_ref,
                ksx_ref, kwx_ref, vsb_ref, vwb_ref, kcx_ref, vcb_ref):
    qb = pl.program_id(1)
    q0 = qb * Q_BLOCK
    q0a = pl.multiple_of(q0, Q_BLOCK)
    nq = NSA_GROUP * Q_BLOCK
    gw = NSA_GROUP * HEAD_DIM
    t_all = ks_ref.shape[1]
    n_sel = t_all // SEL_LEN
    kchunk = 512
    groups = range(NSA_KV)

    @pl.when(qb == 0)
    def _():
        for g in groups:
            cols = slice(g * HEAD_DIM, (g + 1) * HEAD_DIM)
            ksx_ref[g, :, 0:HEAD_DIM] = ks_ref[0, :, cols].astype(MXU_DTYPE)
            ksx_ref[g, :, HEAD_DIM:] = kaux_ref[...]
            kwx_ref[g, :, 0:HEAD_DIM] = kw_ref[0, :, cols].astype(MXU_DTYPE)
            kwx_ref[g, :, HEAD_DIM:] = kaux_ref[...]
            kcx_ref[g, :, 0:HEAD_DIM] = kc_ref[0, g].astype(MXU_DTYPE)
            kcx_ref[g, :, HEAD_DIM:] = kauxc_ref[...]
            vsb_ref[g] = vs_ref[0, :, cols].astype(MXU_DTYPE)
            vwb_ref[g] = vw_ref[0, :, cols].astype(MXU_DTYPE)
            vcb_ref[g] = vc_ref[0, g].astype(MXU_DTYPE)

    def stack(v):
        return jnp.concatenate([v] * NSA_GROUP, axis=0)

    def online(carry, s, v):
        m, l, acc = carry
        m_new = jnp.maximum(m, jnp.max(s, axis=1, keepdims=True))
        alpha = jnp.exp2(m - m_new)
        p = jnp.exp2(s - m_new)
        l = alpha * l + jnp.sum(p, axis=1, keepdims=True)
        acc = alpha * acc + jnp.dot(p.astype(MXU_DTYPE), v, preferred_element_type=F32)
        return m_new, l, acc

    def finish(carry):
        _, l, acc = carry
        return acc * jnp.where(l > 0.0, 1.0 / l, 0.0)

    init = (jnp.full((nq, 1), NEG_INF, F32), jnp.zeros((nq, 1), F32), jnp.zeros((nq, HEAD_DIM), F32))
    t_loc = lax.broadcasted_iota(jnp.int32, (Q_BLOCK, LANES), 0)
    lane = lax.broadcasted_iota(jnp.int32, (Q_BLOCK, LANES), 1)

    ncmp = kcx_ref.shape[1]
    pos_c = (CMP_LEN - 1) + CMP_STRIDE * lax.broadcasted_iota(jnp.int32, (Q_BLOCK, ncmp), 1)
    ok_c = stack(jnp.where(q0 + lax.broadcasted_iota(jnp.int32, (Q_BLOCK, ncmp), 0) >= pos_c, 1.0, 0.0)) > 0.5
    tri_low = stack(jnp.where(lax.broadcasted_iota(jnp.int32, (Q_BLOCK, Q_BLOCK), 1)
                              <= lax.broadcasted_iota(jnp.int32, (Q_BLOCK, Q_BLOCK), 0), 1.0, 0.0)) > 0.5
    n_full = WINDOW // SEL_LEN - 1
    win_bias = jnp.where(lane < n_sel,
                         jnp.where(lane >= qb - n_full, jnp.where(lane < qb, 0.0, MASK_VAL), MASK_VAL), 0.0)
    has_far = jnp.where(q0 >= WINDOW, 1.0, 0.0)
    ok_edge = stack(jnp.where(lane < Q_BLOCK,
                              jnp.where(lane > t_loc, has_far, 0.0),
                              jnp.where(lane - Q_BLOCK <= t_loc, 1.0, 0.0))
                    ) > 0.5
    w0 = pl.multiple_of(jnp.maximum(q0 - WINDOW, 0), Q_BLOCK)
    nblk = lax.broadcasted_iota(jnp.int32, (n_sel, Q_BLOCK), 0)
    forced = (nblk == 0) | (nblk == qb) | (nblk == qb - 1)

    q_plain, q_sel, o_cmp, o_win = [], [], [], []
    for g in groups:
        q = q_ref[0, :, g * gw:(g + 1) * gw] * (HEAD_DIM ** -0.5 * LOG2E)
        q_main = jnp.concatenate([q[:, r * HEAD_DIM:(r + 1) * HEAD_DIM] for r in range(NSA_GROUP)],
                                 axis=0).astype(MXU_DTYPE)
        slope_lanes = slt_ref[g]

        def with_aux(block_bias, q_main=q_main, slope_lanes=slope_lanes):
            return jnp.concatenate([q_main, (stack(block_bias) + slope_lanes).astype(MXU_DTYPE)], axis=1)

        qp = with_aux(jnp.zeros((Q_BLOCK, LANES), F32))
        q_plain.append(qp)

        carry = online(init, _mm_nt(with_aux(win_bias), kwx_ref[g, pl.ds(w0, WINDOW), :]),
                       vwb_ref[g, pl.ds(w0, WINDOW), :])
        k_edge = jnp.concatenate([kwx_ref[g, pl.ds(w0, Q_BLOCK), :], kwx_ref[g, pl.ds(q0a, Q_BLOCK), :]], axis=0)
        v_edge = jnp.concatenate([vwb_ref[g, pl.ds(w0, Q_BLOCK), :], vwb_ref[g, pl.ds(q0a, Q_BLOCK), :]], axis=0)
        s_edge = jnp.where(ok_edge, _mm_nt(qp, k_edge), MASK_VAL)
        o_win.append(finish(online(carry, s_edge, v_edge)))

        s_c = jnp.where(ok_c, _mm_nt(qp, kcx_ref[g]), MASK_VAL)
        p_c = jnp.exp2(s_c - jnp.maximum(jnp.max(s_c, axis=1, keepdims=True), NEG_INF))
        l_c = jnp.sum(p_c, axis=1, keepdims=True)
        p_c = p_c * jnp.where(l_c > 0.0, 1.0 / l_c, 0.0)
        o_cmp.append(jnp.dot(p_c.astype(MXU_DTYPE), vcb_ref[g], preferred_element_type=F32))

        p_sum = p_c[0:Q_BLOCK]
        for r in range(1, NSA_GROUP):
            p_sum = p_sum + p_c[r * Q_BLOCK:(r + 1) * Q_BLOCK]
        imp = lax.dot_general(ov_ref[...], p_sum, (((1,), (1,)), ((), ())),
                              precision=lax.Precision.HIGHEST, preferred_element_type=F32)[0:n_sel]
        imp = jnp.where(nblk > qb, NEG_INF, jnp.where(forced, FORCE_SCORE, imp))
        rank = jnp.zeros(imp.shape, F32)
        for m_ in range(n_sel):
            other = imp[m_:m_ + 1, :]
            rank = rank + jnp.where(nblk > m_, jnp.where(other >= imp, 1.0, 0.0), jnp.where(other > imp, 1.0, 0.0))
        sel_bias = jnp.where(rank < float(min(SEL_TOPN, n_sel)), jnp.where(nblk < qb, 0.0, MASK_VAL), MASK_VAL)
        sel_bias = jnp.concatenate([sel_bias, jnp.zeros((LANES - n_sel, Q_BLOCK), F32)], axis=0).T
        q_sel.append(with_aux(sel_bias))

    def sel_branch(n_chunks):
        def run():
            outs = []
            for g in groups:
                carry = init
                if n_chunks:
                    rows = slice(0, n_chunks * kchunk)
                    carry = online(carry, _mm_nt(q_sel[g], ksx_ref[g, rows, :]), vsb_ref[g, rows, :])
                s_own = jnp.where(tri_low, _mm_nt(q_plain[g], ksx_ref[g, pl.ds(q0a, Q_BLOCK), :]), MASK_VAL)
                outs.append(finish(online(carry, s_own, vsb_ref[g, pl.ds(q0a, Q_BLOCK), :])))
            return tuple(outs)
        return run

    max_chunks = (t_all - Q_BLOCK + kchunk - 1) // kchunk
    o_sel = lax.switch((q0 + kchunk - 1) // kchunk, [sel_branch(n) for n in range(max_chunks + 1)])

    def unstack(o):
        return jnp.concatenate([o[r * Q_BLOCK:(r + 1) * Q_BLOCK] for r in range(NSA_GROUP)], axis=1)

    sig = _sigmoid(gl_ref[0])
    sig_hi = sig.astype(MXU_DTYPE)
    sig_lo = (sig - sig_hi.astype(F32)).astype(MXU_DTYPE)
    for g in groups:
        out = None
        for i, branch in enumerate((o_cmp[g], o_sel[g], o_win[g])):
            gate = (jnp.dot(sig_hi, ge_ref[g, i], preferred_element_type=F32)
                    + jnp.dot(sig_lo, ge_ref[g, i], preferred_element_type=F32))
            term = gate * unstack(branch)
            out = term if out is None else out + term
        cols = slice(g * gw, (g + 1) * gw)
        o_ref[0, :, cols] = (out * _silu(sg_ref[0, :, cols])).astype(o_ref.dtype)


def _split3(v):
    bf = jnp.bfloat16
    hi = v.astype(bf).astype(np.float64)
    mid = (v - hi).astype(bf).astype(np.float64)
    lo = (v - hi - mid).astype(bf).astype(np.float64)
    return hi, mid, lo


def _slope_lanes():
    heads = NSA_KV * NSA_GROUP
    h = np.arange(1, heads + 1, dtype=np.float32)
    s = np.power(np.float32(2.0), -8.0 * h / heads).astype(np.float32).astype(np.float64) * LOG2E
    parts = _split3(s)
    out = np.zeros((heads, LANES), np.float32)
    for rep in range(2):
        for i, part in enumerate(parts):
            out[:, POS_LANE + 3 * rep + i] = part
    out = np.repeat(out.reshape(NSA_KV, NSA_GROUP, 1, LANES), Q_BLOCK, axis=2)
    return jnp.asarray(out.reshape(NSA_KV, NSA_GROUP * Q_BLOCK, LANES))


def _key_aux(pos, blocks):
    out = np.zeros((pos.shape[0], LANES), np.float32)
    if blocks:
        out[np.arange(pos.shape[0]), pos // SEL_LEN] = 1.0
    out[:, POS_LANE:POS_LANE + 3] = (SEL_LEN * (pos // SEL_LEN))[:, None]
    out[:, POS_LANE + 3:POS_LANE + 6] = (pos % SEL_LEN)[:, None]
    return jnp.asarray(out, dtype=MXU_DTYPE)


def _overlap_t(t, ncmp_rows):
    n_cmp = (t - CMP_LEN) // CMP_STRIDE + 1
    n_sel = t // SEL_LEN
    s_c = np.arange(n_cmp) * CMP_STRIDE
    s_s = np.arange(n_sel) * SEL_LEN
    ov = np.clip(np.minimum(s_c[:, None] + CMP_LEN, s_s[None, :] + SEL_LEN)
                 - np.maximum(s_c[:, None], s_s[None, :]), 0, None).astype(np.float32) / CMP_LEN
    out = np.zeros((LANES, ncmp_rows), np.float32)
    out[:n_sel, :n_cmp] = ov.T
    return jnp.asarray(out)


def _gate_expand():
    width = NSA_GROUP * HEAD_DIM
    e = np.zeros((NSA_KV, 3, LANES, width), np.float32)
    for g in range(NSA_KV):
        for r in range(NSA_GROUP):
            for i in range(3):
                e[g, i, (g * NSA_GROUP + r) * 3 + i, r * HEAD_DIM:(r + 1) * HEAD_DIM] = 1.0
    return jnp.asarray(e, dtype=MXU_DTYPE)


def _nsa(proj3, cols, k_cmp, v_cmp):
    b, t, _ = proj3.shape
    assert t // SEL_LEN <= POS_LANE and t % 512 == 0 and Q_BLOCK == SEL_LEN
    gw = NSA_GROUP * HEAD_DIM
    width = NSA_KV * gw
    kvw = NSA_KV * HEAD_DIM
    ncmp_rows = k_cmp.shape[2]
    q_c, ks_c, kw_c, gl_c, sg_c = cols
    assert q_c % width == 0 and sg_c % width == 0 and ks_c % kvw == 0 and kw_c % kvw == 0

    def kvspec(c, is_v):
        return pl.BlockSpec((1, t, kvw), lambda i, j: (i, 0, c // kvw + (1 if is_v else 0)))

    def whole(*shape):
        return pl.BlockSpec(shape, lambda i, j: (0,) * len(shape))

    cmpspec = pl.BlockSpec((1, NSA_KV, ncmp_rows, HEAD_DIM), lambda i, j: (i, 0, 0, 0))
    cmp_pos = (CMP_LEN - 1) + CMP_STRIDE * np.arange(ncmp_rows)
    return pl.pallas_call(
        _nsa_kernel,
        grid=(b, t // Q_BLOCK),
        in_specs=[whole(NSA_KV, NSA_GROUP * Q_BLOCK, LANES), whole(t, LANES), whole(ncmp_rows, LANES),
                  pl.BlockSpec((1, Q_BLOCK, width), lambda i, j: (i, j, q_c // width)),
                  pl.BlockSpec((1, Q_BLOCK, LANES), lambda i, j: (i, j, gl_c // LANES)),
                  pl.BlockSpec((1, Q_BLOCK, width), lambda i, j: (i, j, sg_c // width)),
                  kvspec(ks_c, False), kvspec(ks_c, True), kvspec(kw_c, False), kvspec(kw_c, True),
                  cmpspec, cmpspec, whole(LANES, ncmp_rows), whole(NSA_KV, 3, LANES, gw)],
        out_specs=pl.BlockSpec((1, Q_BLOCK, width), lambda i, j: (i, j, 0)),
        out_shape=jax.ShapeDtypeStruct((b, t, width), MXU_DTYPE),
        scratch_shapes=[pltpu.VMEM((NSA_KV, t, 2 * HEAD_DIM), MXU_DTYPE),
                        pltpu.VMEM((NSA_KV, t, 2 * HEAD_DIM), MXU_DTYPE),
                        pltpu.VMEM((NSA_KV, t, HEAD_DIM), MXU_DTYPE), pltpu.VMEM((NSA_KV, t, HEAD_DIM), MXU_DTYPE),
                        pltpu.VMEM((NSA_KV, ncmp_rows, 2 * HEAD_DIM), MXU_DTYPE),
                        pltpu.VMEM((NSA_KV, ncmp_rows, HEAD_DIM), MXU_DTYPE)],
        compiler_params=_params("parallel", "arbitrary"),
        name="nsa_attn",
    )(_slope_lanes(), _key_aux(np.arange(t), True), _key_aux(cmp_pos, False),
      proj3, proj3, proj3, proj3, proj3, proj3, proj3, k_cmp, v_cmp, _overlap_t(t, ncmp_rows), _gate_expand())


def _out_ln_kernel(a0_ref, a1_ref, a2_ref, a3_ref, w0_ref, w1_ref, w2_ref, w3_ref,
                   x_ref, gate_ref, lng_ref, lnb_ref, shift_ref, scale_ref, *rest, alpha, emit_u):
    if emit_u:
        o_ref, u_ref, pre_ref, c_ref, s1_ref, s2_ref = rest
    else:
        (o_ref, pre_ref, c_ref, s1_ref, s2_ref), u_ref = rest, None
    i, j = pl.program_id(0), pl.program_id(1)
    n_row, nj = pl.num_programs(0) - 1, pl.num_programs(1)
    tn = o_ref.shape[1]
    slot = lax.rem(i, 2)

    @pl.when((i == 0) & (j == 0))
    def _():
        c_ref[...] = jnp.zeros(c_ref.shape, F32)
        s1_ref[...] = jnp.zeros(s1_ref.shape, F32)
        s2_ref[...] = jnp.zeros(s2_ref.shape, F32)

    def build():
        y = (jnp.dot(a0_ref[...], w0_ref[0], preferred_element_type=F32)
             + jnp.dot(a1_ref[...], w1_ref[0], preferred_element_type=F32)
             + jnp.dot(a2_ref[...], w2_ref[0], preferred_element_type=F32)
             + jnp.dot(a3_ref[...], w3_ref[0], preferred_element_type=F32))
        pre_ref[slot * nj + j] = alpha * x_ref[...] + (1.0 + gate_ref[0, 0]) * y

    def add_stats(sidx, jj):
        v = pre_ref[sidx * nj + jj]
        keep = jnp.where(jj == 0, 0.0, 1.0)
        c = keep * c_ref[sidx] + (1.0 - keep) * (v.sum(axis=-1, keepdims=True) / tn)
        c_ref[sidx] = c
        dev = v - c
        s1_ref[sidx] = keep * s1_ref[sidx] + dev.sum(axis=-1, keepdims=True)
        s2_ref[sidx] = keep * s2_ref[sidx] + (dev * dev).sum(axis=-1, keepdims=True)

    def normalise():
        d = nj * tn
        prev = 1 - slot
        off = s1_ref[prev] / d
        mu = c_ref[prev] + off
        inv = lax.rsqrt(s2_ref[prev] / d - off * off + LN_EPS)
        out = (pre_ref[prev * nj + j] - mu) * inv * lng_ref[0] + lnb_ref[0]
        o_ref[...] = out
        if emit_u:
            u_ref[...] = (out * (1.0 + scale_ref[0, 0]) + shift_ref[0, 0]).astype(u_ref.dtype)

    @pl.when(i == 0)
    def _():
        pl.when(j > 0)(lambda: add_stats(slot, j - 1))
        build()

    @pl.when((i > 0) & (i < n_row))
    def _():
        add_stats(jnp.where(j > 0, slot, 1 - slot), jnp.where(j > 0, j - 1, nj - 1))
        normalise()
        build()

    @pl.when(i == n_row)
    def _():
        pl.when(j == 0)(lambda: add_stats(1 - slot, nj - 1))
        normalise()


def _out_ln(y_rg, y_nsa, y_hg, w_out_b, x2, mod4, ln_g, ln_b, layer, alpha, t, emit_u):
    m, d = x2.shape
    depth = w_out_b.shape[0]
    kb = y_rg.shape[1]
    assert y_nsa.shape[1] == 2 * kb and y_hg.shape[1] == kb and d == 4 * kb
    tm, tn = 512, 1024
    nj, n_row = d // tn, m // tm
    assert t % tm == 0
    mrow = lambda i: jnp.minimum(i, n_row - 1)
    mcol = lambda i, j: jnp.where(i < n_row, j, nj - 1)
    orow = lambda i: jnp.maximum(i - 1, 0)
    ocol = lambda i, j: jnp.where(i > 0, j, 0)
    aspec = lambda c: pl.BlockSpec((tm, kb), lambda i, j: (mrow(i), c))
    wspec = lambda r: pl.BlockSpec((1, kb, tn), lambda i, j: (layer, r, mcol(i, j)))
    vspec = pl.BlockSpec((1, 1, tn), lambda i, j: (layer, 0, ocol(i, j)))
    nxt = min(layer + 1, depth - 1)
    modspec = lambda part: pl.BlockSpec(
        (1, 1, 1, tn), lambda i, j: (nxt, (orow(i) * tm) // t, 0, part * nj + ocol(i, j)))
    ospec = pl.BlockSpec((tm, tn), lambda i, j: (orow(i), ocol(i, j)))
    oshape = jax.ShapeDtypeStruct((m, d), F32)
    return pl.pallas_call(
        functools.partial(_out_ln_kernel, alpha=alpha, emit_u=emit_u),
        grid=(n_row + 1, nj),
        in_specs=[aspec(0), aspec(0), aspec(1), aspec(0), wspec(0), wspec(1), wspec(2), wspec(3),
                  pl.BlockSpec((tm, tn), lambda i, j: (mrow(i), mcol(i, j))),
                  pl.BlockSpec((1, 1, 1, tn), lambda i, j: (layer, (mrow(i) * tm) // t, 0, 2 * nj + mcol(i, j))),
                  vspec, vspec, modspec(0), modspec(1)],
        out_specs=[ospec, ospec] if emit_u else ospec,
        out_shape=[oshape, jax.ShapeDtypeStruct((m, d), MXU_DTYPE)] if emit_u else oshape,
        scratch_shapes=[pltpu.VMEM((2 * nj, tm, tn), F32)] + [pltpu.VMEM((2, tm, 1), F32)] * 3,
        compiler_params=_params("arbitrary", "arbitrary"),
        name="out_ln",
    )(y_rg, y_nsa, y_nsa, y_hg, w_out_b, w_out_b, w_out_b, w_out_b, x2, mod4,
      ln_g.reshape(depth, 1, d), ln_b.reshape(depth, 1, d), mod4, mod4)


def kernel(x, c, w_ada, b_ada, w_in, rg_conv_w, rg_conv_b, rg_w_a, rg_b_a, rg_w_x, rg_b_x, rg_lambda,
           nsa_pe_k, nsa_pe_v, nsa_cmp_w1_k, nsa_cmp_w2_k, nsa_cmp_w1_v, nsa_cmp_w2_v,
           hg_lower_bounds, hg_norm_g, w_out, ln_g, ln_b):
    b, t, d = x.shape
    depth = w_ada.shape[0]
    m = b * t
    d_rg, d_nsa, d_hg = d // 4, d // 2, d // 4
    kv_cols = 2 * NSA_KV * HEAD_DIM
    n_gl = 3 * NSA_KV * NSA_GROUP
    alpha = (2.0 * depth) ** 0.25
    assert d_nsa == NSA_KV * NSA_GROUP * HEAD_DIM and b <= SUBLANES

    gl_pad = 512
    c_q = 2 * d_rg
    c_kvc = c_q + d_nsa
    c_kvs = c_kvc + kv_cols
    c_kvw = c_kvs + kv_cols
    c_gl = c_kvw + kv_cols
    c_sg = c_gl + gl_pad
    c_hg = c_sg + d_nsa
    w_in_p = _prep_w_in(jnp.swapaxes(w_in, 1, 2), c_gl, n_gl, gl_pad)
    w_out_b = w_out.astype(MXU_DTYPE)
    w1_k, w1_v = nsa_cmp_w1_k.astype(MXU_DTYPE), nsa_cmp_w1_v.astype(MXU_DTYPE)

    c_pad = jnp.pad(c, ((0, SUBLANES - b), (0, 0)))
    mod4 = _ada(c_pad, w_ada, b_ada).reshape(depth, SUBLANES, 1, 3 * d)

    u = _modulate(x, mod4, 0).reshape(m, d)
    x = x.reshape(m, d)
    for layer in range(depth):
        proj3 = _in_proj(u, w_in_p, layer).reshape(b, t, -1)
        y_rg = _rglru(proj3, rg_conv_w, rg_conv_b, rg_w_a, rg_b_a, rg_w_x, rg_b_x, rg_lambda, layer, d_rg)
        k_cmp, v_cmp = _compress(proj3, c_kvc, nsa_pe_k, nsa_pe_v, w1_k, nsa_cmp_w2_k, w1_v, nsa_cmp_w2_v, layer)
        y_nsa = _nsa(proj3, (c_q, c_kvs, c_kvw, c_gl, c_sg), k_cmp, v_cmp)
        y_hg = _hgrn2(proj3, c_hg, hg_lower_bounds, hg_norm_g, layer, d_hg)
        emit_u = layer + 1 < depth
        res = _out_ln(y_rg.reshape(m, d_rg), y_nsa.reshape(m, d_nsa), y_hg.reshape(m, d_hg), w_out_b,
                      x, mod4, ln_g, ln_b, layer, alpha, t, emit_u)
        x, u = res if emit_u else (res, None)
    return x.reshape(b, t, d)
```

```python
import functools

import numpy as np
import jax
import jax.numpy as jnp
from jax import lax
from jax.experimental import pallas as pl
from jax.experimental.pallas import tpu as pltpu

F32 = jnp.float32
MXU_DTYPE = jnp.bfloat16

HEAD_DIM = 128
RG_CONV = 4
RG_C = 8.0
NSA_KV = 2
NSA_GROUP = 8
CMP_LEN = 32
CMP_STRIDE = 16
SEL_LEN = 64
SEL_TOPN = 16
WINDOW = 512
Q_BLOCK = 64
HG_CHUNK = 64
HG_SUB = 8
LN_EPS = 1e-5
RMS_EPS = 1e-6
NEG_INF = -1e30
FORCE_SCORE = 1e9
MASK_VAL = -(2.0 ** 100)
LOG2E = 1.4426950408889634
POS_LANE = 32

LANES = 128
SUBLANES = 8
VMEM_LIMIT = 56 * 1024 * 1024

ADA_TN = 1024
MODULATE_ROWS = 512
IN_PROJ_TILE = (1024, 1024)
OUT_TILE = (512, 1024)
RG_COLS = 256
HG_HEADS_PER_STEP = 2
HG_UNROLL = 4
NSA_KEY_CHUNK = 512


def _mm(a, b):
    return jnp.dot(a.astype(MXU_DTYPE), b.astype(MXU_DTYPE), preferred_element_type=F32)


def _mm_nt(a, b):
    return lax.dot_general(a.astype(MXU_DTYPE), b.astype(MXU_DTYPE),
                           (((1,), (1,)), ((), ())), preferred_element_type=F32)


def _mm_tn(a, b):
    return lax.dot_general(a.astype(MXU_DTYPE), b.astype(MXU_DTYPE),
                           (((0,), (0,)), ((), ())), preferred_element_type=F32)


def _sigmoid(v):
    return jax.nn.sigmoid(v)


def _silu(v):
    return v * jax.nn.sigmoid(v)


def _params(*semantics):
    return pltpu.CompilerParams(dimension_semantics=semantics, vmem_limit_bytes=VMEM_LIMIT)


def _ada_kernel(c_ref, w_ref, b_ref, o_ref):
    o_ref[0] = _mm(c_ref[...], w_ref[0]) + b_ref[0]


def _ada(c_pad, w_ada, b_ada):
    depth, d, n3 = w_ada.shape
    rows = c_pad.shape[0]
    tn = ADA_TN
    return pl.pallas_call(
        _ada_kernel,
        grid=(depth, n3 // tn),
        in_specs=[pl.BlockSpec((rows, d), lambda l, j: (0, 0)),
                  pl.BlockSpec((1, d, tn), lambda l, j: (l, 0, j)),
                  pl.BlockSpec((1, 1, tn), lambda l, j: (l, 0, j))],
        out_specs=pl.BlockSpec((1, rows, tn), lambda l, j: (l, 0, j)),
        out_shape=jax.ShapeDtypeStruct((depth, rows, n3), F32),
        compiler_params=_params("parallel", "parallel"),
        name="ada",
    )(c_pad, w_ada, b_ada.reshape(depth, 1, n3))


def _modulate_kernel(x_ref, shift_ref, scale_ref, o_ref):
    o_ref[0] = (x_ref[0] * (1.0 + scale_ref[0, 0]) + shift_ref[0, 0]).astype(o_ref.dtype)


def _modulate(x, mod4, layer):
    b, t, d = x.shape
    tt = MODULATE_ROWS
    return pl.pallas_call(
        _modulate_kernel,
        grid=(b, t // tt),
        in_specs=[pl.BlockSpec((1, tt, d), lambda i, j: (i, j, 0)),
                  pl.BlockSpec((1, 1, 1, d), lambda i, j: (layer, i, 0, 0)),
                  pl.BlockSpec((1, 1, 1, d), lambda i, j: (layer, i, 0, 1))],
        out_specs=pl.BlockSpec((1, tt, d), lambda i, j: (i, j, 0)),
        out_shape=jax.ShapeDtypeStruct((b, t, d), MXU_DTYPE),
        compiler_params=_params("parallel", "parallel"),
        name="modulate",
    )(x, mod4, mod4)


def _prep_w_in_kernel(w_ref, o_ref, prev_ref, *, gl_blk, n_gl):
    j = pl.program_id(1)
    tn = w_ref.shape[1]

    @pl.when(j < gl_blk)
    def _():
        o_ref[0] = w_ref[0].T.astype(o_ref.dtype)

    @pl.when(j == gl_blk)
    def _():
        row = lax.broadcasted_iota(jnp.int32, w_ref.shape[1:], 0)
        o_ref[0] = jnp.where(row < n_gl, w_ref[0], 0.0).T.astype(o_ref.dtype)

    @pl.when(j > gl_blk)
    def _():
        tile = jnp.concatenate([prev_ref[n_gl:tn, :], w_ref[0, 0:n_gl, :]], axis=0)
        o_ref[0] = tile.T.astype(o_ref.dtype)

    prev_ref[...] = w_ref[0]


def _prep_w_in(w_in_t, c_gl, n_gl, tn):
    depth, n, d = w_in_t.shape
    assert c_gl % tn == 0 and (n - n_gl) % tn == 0 and n_gl % SUBLANES == 0
    n_out = n - n_gl + tn
    return pl.pallas_call(
        functools.partial(_prep_w_in_kernel, gl_blk=c_gl // tn, n_gl=n_gl),
        grid=(depth, n_out // tn),
        in_specs=[pl.BlockSpec((1, tn, d), lambda l, j: (l, j, 0))],
        out_specs=pl.BlockSpec((1, d, tn), lambda l, j: (l, 0, j)),
        out_shape=jax.ShapeDtypeStruct((depth, d, n_out), MXU_DTYPE),
        scratch_shapes=[pltpu.VMEM((tn, d), F32)],
        compiler_params=_params("arbitrary", "arbitrary"),
        name="prep_w_in",
    )(w_in_t)


def _in_proj_kernel(a_ref, w_ref, o_ref):
    o_ref[...] = jnp.dot(a_ref[...], w_ref[0], preferred_element_type=F32)


def _in_proj(u, w_in_p, layer):
    m, d = u.shape
    n = w_in_p.shape[2]
    tm, tn = IN_PROJ_TILE
    return pl.pallas_call(
        _in_proj_kernel,
        grid=(m // tm, n // tn),
        in_specs=[pl.BlockSpec((tm, d), lambda i, j: (i, 0)),
                  pl.BlockSpec((1, d, tn), lambda i, j: (layer, 0, j))],
        out_specs=pl.BlockSpec((tm, tn), lambda i, j: (i, j)),
        out_shape=jax.ShapeDtypeStruct((m, n), F32),
        compiler_params=_params("parallel", "parallel"),
        name="in_proj",
    )(u, w_in_p)


def _rglru_kernel(x_ref, g_ref, cw_ref, cb_ref, wa_ref, ba_ref, wx_ref, bx_ref, lam_ref, o_ref,
                  xp_ref, a_ref, b_ref, h_ref):
    t, cb = x_ref.shape[1], x_ref.shape[2]
    pad = SUBLANES
    xp_ref[0:pad, :] = jnp.zeros((pad, cb), F32)
    xp_ref[pad:, :] = x_ref[0]
    xc = cb_ref[0]
    for j in range(RG_CONV):
        xc = xc + xp_ref[pl.ds(pad - (RG_CONV - 1) + j, t), :] * cw_ref[0, j:j + 1, :]

    row = lax.broadcasted_iota(jnp.int32, (t, HEAD_DIM), 0)
    sub = jnp.bitwise_and(row, SUBLANES - 1)
    for n in range(cb // HEAD_DIM):
        sl = slice(n * HEAD_DIM, (n + 1) * HEAD_DIM)
        xb = xc[:, sl]
        r = _sigmoid(_mm(xb, wa_ref[0, n]) + ba_ref[0, :, sl])
        i = _sigmoid(_mm(xb, wx_ref[0, n]) + bx_ref[0, :, sl])
        neg_lam = -lam_ref[0, :, sl]
        softplus = jnp.maximum(neg_lam, 0.0) + jnp.log1p(jnp.exp(-jnp.abs(neg_lam)))
        log_a = (-RG_C * softplus) * r
        a = jnp.exp(log_a)
        mult = jnp.sqrt(-jnp.tanh(log_a) * (a * a + 1.0))
        mult = jnp.where(row == 0, 1.0, mult)
        bx = mult * (i * xb)
        s = 1
        while s < SUBLANES:
            a_sh = pltpu.roll(a, s, axis=0)
            b_sh = pltpu.roll(bx, s, axis=0)
            inside = sub >= s
            bx = jnp.where(inside, a * b_sh + bx, bx)
            a = jnp.where(inside, a * a_sh, a)
            s *= 2
        a_ref[:, sl] = a
        b_ref[:, sl] = bx

    def carry_rows(v, h):
        r0 = pl.multiple_of(v * SUBLANES, SUBLANES)
        hh = a_ref[pl.ds(r0, SUBLANES), :] * h + b_ref[pl.ds(r0, SUBLANES), :]
        h_ref[pl.ds(r0, SUBLANES), :] = hh
        return jnp.broadcast_to(hh[SUBLANES - 1:SUBLANES, :], hh.shape)

    lax.fori_loop(0, t // SUBLANES, carry_rows, jnp.zeros((SUBLANES, cb), F32))
    o_ref[0] = (h_ref[...] * _silu(g_ref[0])).astype(o_ref.dtype)


def _rglru(proj3, conv_w, conv_b, w_a, b_a, w_x, b_x, lam, layer, d_rg):
    b, t, _ = proj3.shape
    depth = conv_w.shape[0]
    cb = RG_COLS
    nblk = cb // HEAD_DIM
    ncb = d_rg // cb
    vec = lambda v: v.reshape(depth, 1, d_rg)
    vspec = pl.BlockSpec((1, 1, cb), lambda i, j: (layer, 0, j))
    wspec = pl.BlockSpec((1, nblk, HEAD_DIM, HEAD_DIM), lambda i, j: (layer, j, 0, 0))
    return pl.pallas_call(
        _rglru_kernel,
        grid=(b, ncb),
        in_specs=[pl.BlockSpec((1, t, cb), lambda i, j: (i, 0, j)),
                  pl.BlockSpec((1, t, cb), lambda i, j: (i, 0, ncb + j)),
                  pl.BlockSpec((1, RG_CONV, cb), lambda i, j: (layer, 0, j)),
                  vspec, wspec, vspec, wspec, vspec, vspec],
        out_specs=pl.BlockSpec((1, t, cb), lambda i, j: (i, 0, j)),
        out_shape=jax.ShapeDtypeStruct((b, t, d_rg), MXU_DTYPE),
        scratch_shapes=[pltpu.VMEM((t + SUBLANES, cb), F32), pltpu.VMEM((t, cb), F32),
                        pltpu.VMEM((t, cb), F32), pltpu.VMEM((t, cb), F32)],
        compiler_params=_params("parallel", "parallel"),
        name="rglru",
    )(proj3, proj3, conv_w, vec(conv_b), w_a, vec(b_a), w_x, vec(b_x), vec(lam))


def _hgrn2_kernel(q_ref, f_ref, v_ref, g_ref, lbr_ref, ng_ref, o_ref,
                  qs_ref, kk_ref, b_ref, oo_ref, *, layer):
    t = q_ref.shape[1]
    ch, sb = HG_CHUNK, HG_SUB
    nsb = ch // sb
    z = f_ref[0]
    ez = jnp.exp(-jnp.abs(z))
    log_sig = jnp.minimum(z, 0.0) - jnp.log(1.0 + ez)
    inv = 1.0 / (1.0 + ez)
    sig_neg = jnp.where(z >= 0.0, ez * inv, inv)
    if layer == 0:
        log_f = log_sig
        kk = sig_neg
    else:
        raw = lbr_ref[...]
        e = jnp.exp(raw - jnp.max(raw, axis=0, keepdims=True))
        p = e / jnp.sum(e, axis=0, keepdims=True)
        lb = p[1:2]
        for j in range(2, layer + 1):
            lb = lb + p[j:j + 1]
        log_lb = jnp.log(lb)
        other = jnp.log1p(-lb) + log_sig
        log_f = jnp.maximum(log_lb, other) + jnp.log(1.0 + jnp.exp(-jnp.abs(log_lb - other)))
        kk = (1.0 - lb) * sig_neg
    width = q_ref.shape[2]
    heads = range(width // HEAD_DIM)
    row = lax.broadcasted_iota(jnp.int32, (t, width), 0)
    in_chunk = jnp.bitwise_and(row, ch - 1)
    bcum = log_f
    s = 1
    while s < ch:
        bcum = bcum + jnp.where(in_chunk >= s, pltpu.roll(bcum, s, axis=0), 0.0)
        s *= 2
    b_ref[...] = bcum * LOG2E
    kk_ref[...] = kk
    qs_ref[...] = _silu(q_ref[0])

    ones = jnp.ones((HEAD_DIM, ch), MXU_DTYPE)
    lane = lax.broadcasted_iota(jnp.int32, (sb, ch), 1)
    subrow = lax.broadcasted_iota(jnp.int32, (sb, HEAD_DIM), 0)

    def chunk_head(r0, hd, st):
        cols = slice(hd * HEAD_DIM, (hd + 1) * HEAD_DIM)
        bq = b_ref[pl.ds(r0, ch), cols]
        qc = qs_ref[pl.ds(r0, ch), cols]
        kc = kk_ref[pl.ds(r0, ch), cols]
        vc = v_ref[0, pl.ds(r0, ch), cols]
        blast = bq[ch - 1:ch, :]
        o = _mm_nt(qc * jnp.exp2(bq), st)
        a_rows = []
        diag = []
        for blk in range(nsb):
            lo = blk * sb
            b_i, q_i, k_i = bq[lo:lo + sb], qc[lo:lo + sb], kc[lo:lo + sb]
            for s_ in range(sb):
                dec = jnp.where(subrow >= s_, jnp.exp2(b_i - b_i[s_:s_ + 1]), 0.0)
                diag.append(q_i * (k_i[s_:s_ + 1] * dec))
            if blk == 0:
                a_rows.append(jnp.zeros((sb, ch), F32))
            else:
                m_i = bq[lo - 1:lo]
                qd = q_i * jnp.exp2(b_i - m_i)
                kd = kc[0:lo] * jnp.exp2(m_i - bq[0:lo])
                kd = jnp.concatenate([kd, jnp.zeros((ch - lo, HEAD_DIM), F32)], axis=0)
                a_rows.append(_mm_nt(qd, kd))
        dsum = _mm(jnp.concatenate(diag, axis=0), ones)
        for blk in range(nsb):
            acc = a_rows[blk]
            for s_ in range(sb):
                idx = blk * sb + s_
                acc = acc + jnp.where(lane == idx, dsum[idx * sb:(idx + 1) * sb], 0.0)
            a_rows[blk] = acc
        a_mat = jnp.concatenate(a_rows, axis=0)
        oo_ref[pl.ds(r0, ch), cols] = o + _mm(a_mat, vc)
        kdec = kc * jnp.exp2(blast - bq)
        return st * jnp.exp2(blast) + _mm_tn(vc, kdec)

    def chunk(c, states):
        r0 = pl.multiple_of(c * ch, ch)
        return tuple(chunk_head(r0, hd, states[hd]) for hd in heads)

    lax.fori_loop(0, t // ch, chunk, tuple(jnp.zeros((HEAD_DIM, HEAD_DIM), F32) for _ in heads), unroll=HG_UNROLL)
    gate = _silu(g_ref[0])
    for hd in heads:
        cols = slice(hd * HEAD_DIM, (hd + 1) * HEAD_DIM)
        o = oo_ref[:, cols]
        o = o * lax.rsqrt(jnp.mean(o * o, axis=-1, keepdims=True) + RMS_EPS) * ng_ref[0]
        o_ref[0, :, cols] = (o * gate[:, cols]).astype(o_ref.dtype)


def _hgrn2(proj3, col0, lower_bounds, norm_g, layer, d_hg):
    b, t, _ = proj3.shape
    depth = lower_bounds.shape[0]
    width = HG_HEADS_PER_STEP * HEAD_DIM
    steps = d_hg // width
    assert col0 % width == 0 and d_hg % width == 0

    def colspec(k):
        return pl.BlockSpec((1, t, width), lambda i, h: (i, 0, col0 // width + k * steps + h))

    return pl.pallas_call(
        functools.partial(_hgrn2_kernel, layer=layer),
        grid=(b, steps),
        in_specs=[colspec(0), colspec(1), colspec(2), colspec(3),
                  pl.BlockSpec((depth, width), lambda i, h: (0, h)),
                  pl.BlockSpec((1, 1, HEAD_DIM), lambda i, h: (layer, 0, 0))],
        out_specs=pl.BlockSpec((1, t, width), lambda i, h: (i, 0, h)),
        out_shape=jax.ShapeDtypeStruct((b, t, d_hg), MXU_DTYPE),
        scratch_shapes=[pltpu.VMEM((t, width), F32)] * 4,
        compiler_params=_params("parallel", "parallel"),
        name="hgrn2",
    )(proj3, proj3, proj3, proj3, lower_bounds, norm_g.reshape(depth, 1, HEAD_DIM))


def _compress_kernel(*refs):
    x_refs = refs[:2 * NSA_KV]
    pek_ref, pev_ref, w1k_ref, w2k_ref, w1v_ref, w2v_ref, ko_ref, vo_ref = refs[2 * NSA_KV:]
    nrow = x_refs[0].shape[1] // CMP_STRIDE
    row = lax.broadcasted_iota(jnp.int32, (nrow, HEAD_DIM), 0)
    branches = ((pek_ref, w1k_ref, w2k_ref, ko_ref), (pev_ref, w1v_ref, w2v_ref, vo_ref))
    for kv, (pe_ref, w1_ref, w2_ref, out_ref) in enumerate(branches):
        for g in range(NSA_KV):
            x_ref = x_refs[kv * NSA_KV + g]
            lo = jnp.concatenate(
                [x_ref[0, pl.ds(i, nrow, stride=CMP_STRIDE), :] for i in range(CMP_STRIDE)], axis=1)
            hi = pltpu.roll(lo, nrow - 1, axis=0)
            blk = jnp.concatenate([lo, hi], axis=1) + pe_ref[0]
            hid = _silu(_mm(blk, w1_ref[0]))
            out = _mm(hid, w2_ref[0])
            out_ref[0, g] = jnp.where(row < nrow - 1, out, 0.0)


def _compress(proj3, col, pe_k, pe_v, w1_k, w2_k, w1_v, w2_v, layer):
    b, t, _ = proj3.shape
    nrow = t // CMP_STRIDE
    nx = 2 * NSA_KV
    assert col % HEAD_DIM == 0
    xspec = lambda k: pl.BlockSpec((1, t, HEAD_DIM), lambda i: (i, 0, col // HEAD_DIM + k))
    depth = pe_k.shape[0]
    hidden = w1_k.shape[2]
    flat = CMP_LEN * HEAD_DIM
    pespec = pl.BlockSpec((1, 1, flat), lambda i: (layer, 0, 0))
    w1spec = pl.BlockSpec((1, flat, hidden), lambda i: (layer, 0, 0))
    w2spec = pl.BlockSpec((1, hidden, HEAD_DIM), lambda i: (layer, 0, 0))
    ospec = pl.BlockSpec((1, NSA_KV, nrow, HEAD_DIM), lambda i: (i, 0, 0, 0))
    oshape = jax.ShapeDtypeStruct((b, NSA_KV, nrow, HEAD_DIM), F32)
    return pl.pallas_call(
        _compress_kernel,
        grid=(b,),
        in_specs=[xspec(k) for k in range(nx)] + [pespec, pespec, w1spec, w2spec, w1spec, w2spec],
        out_specs=[ospec, ospec],
        out_shape=[oshape, oshape],
        compiler_params=_params("parallel"),
        name="nsa_compress",
    )(*([proj3] * nx), pe_k.reshape(depth, 1, flat), pe_v.reshape(depth, 1, flat), w1_k, w2_k, w1_v, w2_v)


def _nsa_kernel(slt_ref, kaux_ref, kauxc_ref, q_ref, gl_ref, sg_ref, ks_ref, vs_ref, kw_ref, vw_ref,
                kc_ref, vc_ref, ov_ref, ge_ref, o_ref,
                ksx_ref, kwx_ref, vsb_ref, vwb_ref, kcx_ref, vcb_ref):
    qb = pl.program_id(1)
    q0 = qb * Q_BLOCK
    q0a = pl.multiple_of(q0, Q_BLOCK)
    nq = NSA_GROUP * Q_BLOCK
    gw = NSA_GROUP * HEAD_DIM
    t_all = ks_ref.shape[1]
    n_sel = t_all // SEL_LEN
    kchunk = NSA_KEY_CHUNK
    groups = range(NSA_KV)

    @pl.when(qb == 0)
    def _():
        for g in groups:
            cols = slice(g * HEAD_DIM, (g + 1) * HEAD_DIM)
            ksx_ref[g, :, 0:HEAD_DIM] = ks_ref[0, :, cols].astype(MXU_DTYPE)
            ksx_ref[g, :, HEAD_DIM:] = kaux_ref[...]
            kwx_ref[g, :, 0:HEAD_DIM] = kw_ref[0, :, cols].astype(MXU_DTYPE)
            kwx_ref[g, :, HEAD_DIM:] = kaux_ref[...]
            kcx_ref[g, :, 0:HEAD_DIM] = kc_ref[0, g].astype(MXU_DTYPE)
            kcx_ref[g, :, HEAD_DIM:] = kauxc_ref[...]
            vsb_ref[g] = vs_ref[0, :, cols].astype(MXU_DTYPE)
            vwb_ref[g] = vw_ref[0, :, cols].astype(MXU_DTYPE)
            vcb_ref[g] = vc_ref[0, g].astype(MXU_DTYPE)

    def stack(v):
        return jnp.concatenate([v] * NSA_GROUP, axis=0)

    def online(carry, s, v):
        m, l, acc = carry
        m_new = jnp.maximum(m, jnp.max(s, axis=1, keepdims=True))
        alpha = jnp.exp2(m - m_new)
        p = jnp.exp2(s - m_new)
        l = alpha * l + jnp.sum(p, axis=1, keepdims=True)
        acc = alpha * acc + jnp.dot(p.astype(MXU_DTYPE), v, preferred_element_type=F32)
        return m_new, l, acc

    def finish(carry):
        _, l, acc = carry
        return acc * jnp.where(l > 0.0, 1.0 / l, 0.0)

    init = (jnp.full((nq, 1), NEG_INF, F32), jnp.zeros((nq, 1), F32), jnp.zeros((nq, HEAD_DIM), F32))
    t_loc = lax.broadcasted_iota(jnp.int32, (Q_BLOCK, LANES), 0)
    lane = lax.broadcasted_iota(jnp.int32, (Q_BLOCK, LANES), 1)

    ncmp = kcx_ref.shape[1]
    pos_c = (CMP_LEN - 1) + CMP_STRIDE * lax.broadcasted_iota(jnp.int32, (Q_BLOCK, ncmp), 1)
    ok_c = stack(jnp.where(q0 + lax.broadcasted_iota(jnp.int32, (Q_BLOCK, ncmp), 0) >= pos_c, 1.0, 0.0)) > 0.5
    tri_low = stack(jnp.where(lax.broadcasted_iota(jnp.int32, (Q_BLOCK, Q_BLOCK), 1)
                              <= lax.broadcasted_iota(jnp.int32, (Q_BLOCK, Q_BLOCK), 0), 1.0, 0.0)) > 0.5
    n_full = WINDOW // SEL_LEN - 1
    win_bias = jnp.where(lane < n_sel,
                         jnp.where(lane >= qb - n_full, jnp.where(lane < qb, 0.0, MASK_VAL), MASK_VAL), 0.0)
    has_far = jnp.where(q0 >= WINDOW, 1.0, 0.0)
    ok_edge = stack(jnp.where(lane < Q_BLOCK,
                              jnp.where(lane > t_loc, has_far, 0.0),
                              jnp.where(lane - Q_BLOCK <= t_loc, 1.0, 0.0))
                    ) > 0.5
    w0 = pl.multiple_of(jnp.maximum(q0 - WINDOW, 0), Q_BLOCK)
    nblk = lax.broadcasted_iota(jnp.int32, (n_sel, Q_BLOCK), 0)
    forced = (nblk == 0) | (nblk == qb) | (nblk == qb - 1)

    q_plain, q_sel, o_cmp, o_win = [], [], [], []
    for g in groups:
        q = q_ref[0, :, g * gw:(g + 1) * gw] * (HEAD_DIM ** -0.5 * LOG2E)
        q_main = jnp.concatenate([q[:, r * HEAD_DIM:(r + 1) * HEAD_DIM] for r in range(NSA_GROUP)],
                                 axis=0).astype(MXU_DTYPE)
        slope_lanes = slt_ref[g]

        def with_aux(block_bias, q_main=q_main, slope_lanes=slope_lanes):
            return jnp.concatenate([q_main, (stack(block_bias) + slope_lanes).astype(MXU_DTYPE)], axis=1)

        qp = with_aux(jnp.zeros((Q_BLOCK, LANES), F32))
        q_plain.append(qp)

        carry = online(init, _mm_nt(with_aux(win_bias), kwx_ref[g, pl.ds(w0, WINDOW), :]),
                       vwb_ref[g, pl.ds(w0, WINDOW), :])
        k_edge = jnp.concatenate([kwx_ref[g, pl.ds(w0, Q_BLOCK), :], kwx_ref[g, pl.ds(q0a, Q_BLOCK), :]], axis=0)
        v_edge = jnp.concatenate([vwb_ref[g, pl.ds(w0, Q_BLOCK), :], vwb_ref[g, pl.ds(q0a, Q_BLOCK), :]], axis=0)
        s_edge = jnp.where(ok_edge, _mm_nt(qp, k_edge), MASK_VAL)
        o_win.append(finish(online(carry, s_edge, v_edge)))

        s_c = jnp.where(ok_c, _mm_nt(qp, kcx_ref[g]), MASK_VAL)
        p_c = jnp.exp2(s_c - jnp.maximum(jnp.max(s_c, axis=1, keepdims=True), NEG_INF))
        l_c = jnp.sum(p_c, axis=1, keepdims=True)
        p_c = p_c * jnp.where(l_c > 0.0, 1.0 / l_c, 0.0)
        o_cmp.append(jnp.dot(p_c.astype(MXU_DTYPE), vcb_ref[g], preferred_element_type=F32))

        p_sum = p_c[0:Q_BLOCK]
        for r in range(1, NSA_GROUP):
            p_sum = p_sum + p_c[r * Q_BLOCK:(r + 1) * Q_BLOCK]
        imp = lax.dot_general(ov_ref[...], p_sum, (((1,), (1,)), ((), ())),
                              precision=lax.Precision.HIGHEST, preferred_element_type=F32)[0:n_sel]
        imp = jnp.where(nblk > qb, NEG_INF, jnp.where(forced, FORCE_SCORE, imp))
        rank = jnp.zeros(imp.shape, F32)
        for m_ in range(n_sel):
            other = imp[m_:m_ + 1, :]
            rank = rank + jnp.where(nblk > m_, jnp.where(other >= imp, 1.0, 0.0), jnp.where(other > imp, 1.0, 0.0))
        sel_bias = jnp.where(rank < float(min(SEL_TOPN, n_sel)), jnp.where(nblk < qb, 0.0, MASK_VAL), MASK_VAL)
        sel_bias = jnp.concatenate([sel_bias, jnp.zeros((LANES - n_sel, Q_BLOCK), F32)], axis=0).T
        q_sel.append(with_aux(sel_bias))

    def sel_branch(n_chunks):
        def run():
            outs = []
            for g in groups:
                carry = init
                if n_chunks:
                    rows = slice(0, n_chunks * kchunk)
                    carry = online(carry, _mm_nt(q_sel[g], ksx_ref[g, rows, :]), vsb_ref[g, rows, :])
                s_own = jnp.where(tri_low, _mm_nt(q_plain[g], ksx_ref[g, pl.ds(q0a, Q_BLOCK), :]), MASK_VAL)
                outs.append(finish(online(carry, s_own, vsb_ref[g, pl.ds(q0a, Q_BLOCK), :])))
            return tuple(outs)
        return run

    max_chunks = (t_all - Q_BLOCK + kchunk - 1) // kchunk
    o_sel = lax.switch((q0 + kchunk - 1) // kchunk, [sel_branch(n) for n in range(max_chunks + 1)])

    def unstack(o):
        return jnp.concatenate([o[r * Q_BLOCK:(r + 1) * Q_BLOCK] for r in range(NSA_GROUP)], axis=1)

    sig = _sigmoid(gl_ref[0])
    sig_hi = sig.astype(MXU_DTYPE)
    sig_lo = (sig - sig_hi.astype(F32)).astype(MXU_DTYPE)
    for g in groups:
        out = None
        for i, branch in enumerate((o_cmp[g], o_sel[g], o_win[g])):
            gate = (jnp.dot(sig_hi, ge_ref[g, i], preferred_element_type=F32)
                    + jnp.dot(sig_lo, ge_ref[g, i], preferred_element_type=F32))
            term = gate * unstack(branch)
            out = term if out is None else out + term
        cols = slice(g * gw, (g + 1) * gw)
        o_ref[0, :, cols] = (out * _silu(sg_ref[0, :, cols])).astype(o_ref.dtype)


def _split3(v):
    bf = jnp.bfloat16
    hi = v.astype(bf).astype(np.float64)
    mid = (v - hi).astype(bf).astype(np.float64)
    lo = (v - hi - mid).astype(bf).astype(np.float64)
    return hi, mid, lo


def _slope_lanes():
    heads = NSA_KV * NSA_GROUP
    h = np.arange(1, heads + 1, dtype=np.float32)
    s = np.power(np.float32(2.0), -8.0 * h / heads).astype(np.float32).astype(np.float64) * LOG2E
    parts = _split3(s)
    out = np.zeros((heads, LANES), np.float32)
    for rep in range(2):
        for i, part in enumerate(parts):
            out[:, POS_LANE + 3 * rep + i] = part
    out = np.repeat(out.reshape(NSA_KV, NSA_GROUP, 1, LANES), Q_BLOCK, axis=2)
    return jnp.asarray(out.reshape(NSA_KV, NSA_GROUP * Q_BLOCK, LANES))


def _key_aux(pos, blocks):
    out = np.zeros((pos.shape[0], LANES), np.float32)
    if blocks:
        out[np.arange(pos.shape[0]), pos // SEL_LEN] = 1.0
    out[:, POS_LANE:POS_LANE + 3] = (SEL_LEN * (pos // SEL_LEN))[:, None]
    out[:, POS_LANE + 3:POS_LANE + 6] = (pos % SEL_LEN)[:, None]
    return jnp.asarray(out, dtype=MXU_DTYPE)


def _overlap_t(t, ncmp_rows):
    n_cmp = (t - CMP_LEN) // CMP_STRIDE + 1
    n_sel = t // SEL_LEN
    s_c = np.arange(n_cmp) * CMP_STRIDE
    s_s = np.arange(n_sel) * SEL_LEN
    ov = np.clip(np.minimum(s_c[:, None] + CMP_LEN, s_s[None, :] + SEL_LEN)
                 - np.maximum(s_c[:, None], s_s[None, :]), 0, None).astype(np.float32) / CMP_LEN
    out = np.zeros((LANES, ncmp_rows), np.float32)
    out[:n_sel, :n_cmp] = ov.T
    return jnp.asarray(out)


def _gate_expand():
    width = NSA_GROUP * HEAD_DIM
    e = np.zeros((NSA_KV, 3, LANES, width), np.float32)
    for g in range(NSA_KV):
        for r in range(NSA_GROUP):
            for i in range(3):
                e[g, i, (g * NSA_GROUP + r) * 3 + i, r * HEAD_DIM:(r + 1) * HEAD_DIM] = 1.0
    return jnp.asarray(e, dtype=MXU_DTYPE)


def _nsa(proj3, cols, k_cmp, v_cmp):
    b, t, _ = proj3.shape
    assert t // SEL_LEN <= POS_LANE and t % NSA_KEY_CHUNK == 0 and Q_BLOCK == SEL_LEN
    gw = NSA_GROUP * HEAD_DIM
    width = NSA_KV * gw
    kvw = NSA_KV * HEAD_DIM
    ncmp_rows = k_cmp.shape[2]
    q_c, ks_c, kw_c, gl_c, sg_c = cols
    assert q_c % width == 0 and sg_c % width == 0 and ks_c % kvw == 0 and kw_c % kvw == 0

    def kvspec(c, is_v):
        return pl.BlockSpec((1, t, kvw), lambda i, j: (i, 0, c // kvw + (1 if is_v else 0)))

    def whole(*shape):
        return pl.BlockSpec(shape, lambda i, j: (0,) * len(shape))

    cmpspec = pl.BlockSpec((1, NSA_KV, ncmp_rows, HEAD_DIM), lambda i, j: (i, 0, 0, 0))
    cmp_pos = (CMP_LEN - 1) + CMP_STRIDE * np.arange(ncmp_rows)
    return pl.pallas_call(
        _nsa_kernel,
        grid=(b, t // Q_BLOCK),
        in_specs=[whole(NSA_KV, NSA_GROUP * Q_BLOCK, LANES), whole(t, LANES), whole(ncmp_rows, LANES),
                  pl.BlockSpec((1, Q_BLOCK, width), lambda i, j: (i, j, q_c // width)),
                  pl.BlockSpec((1, Q_BLOCK, LANES), lambda i, j: (i, j, gl_c // LANES)),
                  pl.BlockSpec((1, Q_BLOCK, width), lambda i, j: (i, j, sg_c // width)),
                  kvspec(ks_c, False), kvspec(ks_c, True), kvspec(kw_c, False), kvspec(kw_c, True),
                  cmpspec, cmpspec, whole(LANES, ncmp_rows), whole(NSA_KV, 3, LANES, gw)],
        out_specs=pl.BlockSpec((1, Q_BLOCK, width), lambda i, j: (i, j, 0)),
        out_shape=jax.ShapeDtypeStruct((b, t, width), MXU_DTYPE),
        scratch_shapes=[pltpu.VMEM((NSA_KV, t, 2 * HEAD_DIM), MXU_DTYPE),
                        pltpu.VMEM((NSA_KV, t, 2 * HEAD_DIM), MXU_DTYPE),
                        pltpu.VMEM((NSA_KV, t, HEAD_DIM), MXU_DTYPE), pltpu.VMEM((NSA_KV, t, HEAD_DIM), MXU_DTYPE),
                        pltpu.VMEM((NSA_KV, ncmp_rows, 2 * HEAD_DIM), MXU_DTYPE),
                        pltpu.VMEM((NSA_KV, ncmp_rows, HEAD_DIM), MXU_DTYPE)],
        compiler_params=_params("parallel", "arbitrary"),
        name="nsa_attn",
    )(_slope_lanes(), _key_aux(np.arange(t), True), _key_aux(cmp_pos, False),
      proj3, proj3, proj3, proj3, proj3, proj3, proj3, k_cmp, v_cmp, _overlap_t(t, ncmp_rows), _gate_expand())


def _out_ln_kernel(a0_ref, a1_ref, a2_ref, a3_ref, w0_ref, w1_ref, w2_ref, w3_ref,
                   x_ref, gate_ref, lng_ref, lnb_ref, shift_ref, scale_ref, *rest, alpha, emit_u):
    if emit_u:
        o_ref, u_ref, pre_ref, c_ref, s1_ref, s2_ref = rest
    else:
        (o_ref, pre_ref, c_ref, s1_ref, s2_ref), u_ref = rest, None
    i, j = pl.program_id(0), pl.program_id(1)
    n_row, nj = pl.num_programs(0) - 1, pl.num_programs(1)
    tn = o_ref.shape[1]
    slot = lax.rem(i, 2)

    @pl.when((i == 0) & (j == 0))
    def _():
        c_ref[...] = jnp.zeros(c_ref.shape, F32)
        s1_ref[...] = jnp.zeros(s1_ref.shape, F32)
        s2_ref[...] = jnp.zeros(s2_ref.shape, F32)

    def build():
        y = (jnp.dot(a0_ref[...], w0_ref[0], preferred_element_type=F32)
             + jnp.dot(a1_ref[...], w1_ref[0], preferred_element_type=F32)
             + jnp.dot(a2_ref[...], w2_ref[0], preferred_element_type=F32)
             + jnp.dot(a3_ref[...], w3_ref[0], preferred_element_type=F32))
        pre_ref[slot * nj + j] = alpha * x_ref[...] + (1.0 + gate_ref[0, 0]) * y

    def add_stats(sidx, jj):
        v = pre_ref[sidx * nj + jj]
        keep = jnp.where(jj == 0, 0.0, 1.0)
        c = keep * c_ref[sidx] + (1.0 - keep) * (v.sum(axis=-1, keepdims=True) / tn)
        c_ref[sidx] = c
        dev = v - c
        s1_ref[sidx] = keep * s1_ref[sidx] + dev.sum(axis=-1, keepdims=True)
        s2_ref[sidx] = keep * s2_ref[sidx] + (dev * dev).sum(axis=-1, keepdims=True)

    def normalise():
        d = nj * tn
        prev = 1 - slot
        off = s1_ref[prev] / d
        mu = c_ref[prev] + off
        inv = lax.rsqrt(s2_ref[prev] / d - off * off + LN_EPS)
        out = (pre_ref[prev * nj + j] - mu) * inv * lng_ref[0] + lnb_ref[0]
        o_ref[...] = out
        if emit_u:
            u_ref[...] = (out * (1.0 + scale_ref[0, 0]) + shift_ref[0, 0]).astype(u_ref.dtype)

    @pl.when(i == 0)
    def _():
        pl.when(j > 0)(lambda: add_stats(slot, j - 1))
        build()

    @pl.when((i > 0) & (i < n_row))
    def _():
        add_stats(jnp.where(j > 0, slot, 1 - slot), jnp.where(j > 0, j - 1, nj - 1))
        normalise()
        build()

    @pl.when(i == n_row)
    def _():
        pl.when(j == 0)(lambda: add_stats(1 - slot, nj - 1))
        normalise()


def _out_ln(y_rg, y_nsa, y_hg, w_out_b, x2, mod4, ln_g, ln_b, layer, alpha, t, emit_u):
    m, d = x2.shape
    depth = w_out_b.shape[0]
    kb = y_rg.shape[1]
    assert y_nsa.shape[1] == 2 * kb and y_hg.shape[1] == kb and d == 4 * kb
    tm, tn = OUT_TILE
    nj, n_row = d // tn, m // tm
    assert t % tm == 0
    mrow = lambda i: jnp.minimum(i, n_row - 1)
    mcol = lambda i, j: jnp.where(i < n_row, j, nj - 1)
    orow = lambda i: jnp.maximum(i - 1, 0)
    ocol = lambda i, j: jnp.where(i > 0, j, 0)
    aspec = lambda c: pl.BlockSpec((tm, kb), lambda i, j: (mrow(i), c))
    wspec = lambda r: pl.BlockSpec((1, kb, tn), lambda i, j: (layer, r, mcol(i, j)))
    vspec = pl.BlockSpec((1, 1, tn), lambda i, j: (layer, 0, ocol(i, j)))
    nxt = min(layer + 1, depth - 1)
    modspec = lambda part: pl.BlockSpec(
        (1, 1, 1, tn), lambda i, j: (nxt, (orow(i) * tm) // t, 0, part * nj + ocol(i, j)))
    ospec = pl.BlockSpec((tm, tn), lambda i, j: (orow(i), ocol(i, j)))
    oshape = jax.ShapeDtypeStruct((m, d), F32)
    return pl.pallas_call(
        functools.partial(_out_ln_kernel, alpha=alpha, emit_u=emit_u),
        grid=(n_row + 1, nj),
        in_specs=[aspec(0), aspec(0), aspec(1), aspec(0), wspec(0), wspec(1), wspec(2), wspec(3),
                  pl.BlockSpec((tm, tn), lambda i, j: (mrow(i), mcol(i, j))),
                  pl.BlockSpec((1, 1, 1, tn), lambda i, j: (layer, (mrow(i) * tm) // t, 0, 2 * nj + mcol(i, j))),
                  vspec, vspec, modspec(0), modspec(1)],
        out_specs=[ospec, ospec] if emit_u else ospec,
        out_shape=[oshape, jax.ShapeDtypeStruct((m, d), MXU_DTYPE)] if emit_u else oshape,
        scratch_shapes=[pltpu.VMEM((2 * nj, tm, tn), F32)] + [pltpu.VMEM((2, tm, 1), F32)] * 3,
        compiler_params=_params("arbitrary", "arbitrary"),
        name="out_ln",
    )(y_rg, y_nsa, y_nsa, y_hg, w_out_b, w_out_b, w_out_b, w_out_b, x2, mod4,
      ln_g.reshape(depth, 1, d), ln_b.reshape(depth, 1, d), mod4, mod4)


def kernel(x, c, w_ada, b_ada, w_in, rg_conv_w, rg_conv_b, rg_w_a, rg_b_a, rg_w_x, rg_b_x, rg_lambda,
           nsa_pe_k, nsa_pe_v, nsa_cmp_w1_k, nsa_cmp_w2_k, nsa_cmp_w1_v, nsa_cmp_w2_v,
           hg_lower_bounds, hg_norm_g, w_out, ln_g, ln_b):
    b, t, d = x.shape
    depth = w_ada.shape[0]
    m = b * t
    d_rg, d_nsa, d_hg = d // 4, d // 2, d // 4
    kv_cols = 2 * NSA_KV * HEAD_DIM
    n_gl = 3 * NSA_KV * NSA_GROUP
    alpha = (2.0 * depth) ** 0.25
    assert d_nsa == NSA_KV * NSA_GROUP * HEAD_DIM and b <= SUBLANES

    gl_pad = 512
    c_q = 2 * d_rg
    c_kvc = c_q + d_nsa
    c_kvs = c_kvc + kv_cols
    c_kvw = c_kvs + kv_cols
    c_gl = c_kvw + kv_cols
    c_sg = c_gl + gl_pad
    c_hg = c_sg + d_nsa
    w_in_p = _prep_w_in(jnp.swapaxes(w_in, 1, 2), c_gl, n_gl, gl_pad)
    w_out_b = w_out.astype(MXU_DTYPE)
    w1_k, w1_v = nsa_cmp_w1_k.astype(MXU_DTYPE), nsa_cmp_w1_v.astype(MXU_DTYPE)

    c_pad = jnp.pad(c, ((0, SUBLANES - b), (0, 0)))
    mod4 = _ada(c_pad, w_ada, b_ada).reshape(depth, SUBLANES, 1, 3 * d)

    u = _modulate(x, mod4, 0).reshape(m, d)
    x = x.reshape(m, d)
    for layer in range(depth):
        proj3 = _in_proj(u, w_in_p, layer).reshape(b, t, -1)
        y_rg = _rglru(proj3, rg_conv_w, rg_conv_b, rg_w_a, rg_b_a, rg_w_x, rg_b_x, rg_lambda, layer, d_rg)
        k_cmp, v_cmp = _compress(proj3, c_kvc, nsa_pe_k, nsa_pe_v, w1_k, nsa_cmp_w2_k, w1_v, nsa_cmp_w2_v, layer)
        y_nsa = _nsa(proj3, (c_q, c_kvs, c_kvw, c_gl, c_sg), k_cmp, v_cmp)
        y_hg = _hgrn2(proj3, c_hg, hg_lower_bounds, hg_norm_g, layer, d_hg)
        emit_u = layer + 1 < depth
        res = _out_ln(y_rg.reshape(m, d_rg), y_nsa.reshape(m, d_nsa), y_hg.reshape(m, d_hg), w_out_b,
                      x, mod4, ln_g, ln_b, layer, alpha, t, emit_u)
        x, u = res if emit_u else (res, None)
    return x.reshape(b, t, d)
```

```python
import functools

import numpy as np
import jax
import jax.numpy as jnp
from jax import lax
from jax.experimental import pallas as pl
from jax.experimental.pallas import tpu as pltpu

F32 = jnp.float32
MXU_DTYPE = jnp.bfloat16

HEAD_DIM = 128
RG_CONV = 4
RG_C = 8.0
NSA_KV = 2
NSA_GROUP = 8
CMP_LEN = 32
CMP_STRIDE = 16
SEL_LEN = 64
SEL_TOPN = 16
WINDOW = 512
Q_BLOCK = 64
HG_CHUNK = 128
HG_SUB = 8
LN_EPS = 1e-5
RMS_EPS = 1e-6
NEG_INF = -1e30
FORCE_SCORE = 1e9
MASK_VAL = -(2.0 ** 100)
LOG2E = 1.4426950408889634
POS_LANE = 32

LANES = 128
SUBLANES = 8
VMEM_LIMIT = 56 * 1024 * 1024

ADA_TN = 1024
MODULATE_ROWS = 512
IN_PROJ_TILE = (1024, 1024)
OUT_TILE = (512, 1024)
RG_COLS = 256
HG_HEADS_PER_STEP = 2
HG_UNROLL = 16
NSA_KEY_CHUNK = 512


def _mm(a, b):
    return jnp.dot(a.astype(MXU_DTYPE), b.astype(MXU_DTYPE), preferred_element_type=F32)


def _mm_nt(a, b):
    return lax.dot_general(a.astype(MXU_DTYPE), b.astype(MXU_DTYPE),
                           (((1,), (1,)), ((), ())), preferred_element_type=F32)


def _mm_tn(a, b):
    return lax.dot_general(a.astype(MXU_DTYPE), b.astype(MXU_DTYPE),
                           (((0,), (0,)), ((), ())), preferred_element_type=F32)


def _sigmoid(v):
    return jax.nn.sigmoid(v)


def _silu(v):
    return v * jax.nn.sigmoid(v)


def _params(*semantics):
    return pltpu.CompilerParams(dimension_semantics=semantics, vmem_limit_bytes=VMEM_LIMIT)


def _ada_kernel(c_ref, w_ref, b_ref, o_ref):
    o_ref[0] = _mm(c_ref[...], w_ref[0]) + b_ref[0]


def _ada(c_pad, w_ada, b_ada):
    depth, d, n3 = w_ada.shape
    rows = c_pad.shape[0]
    tn = ADA_TN
    return pl.pallas_call(
        _ada_kernel,
        grid=(depth, n3 // tn),
        in_specs=[pl.BlockSpec((rows, d), lambda l, j: (0, 0)),
                  pl.BlockSpec((1, d, tn), lambda l, j: (l, 0, j)),
                  pl.BlockSpec((1, 1, tn), lambda l, j: (l, 0, j))],
        out_specs=pl.BlockSpec((1, rows, tn), lambda l, j: (l, 0, j)),
        out_shape=jax.ShapeDtypeStruct((depth, rows, n3), F32),
        compiler_params=_params("parallel", "parallel"),
        name="ada",
    )(c_pad, w_ada, b_ada.reshape(depth, 1, n3))


def _modulate_kernel(x_ref, shift_ref, scale_ref, o_ref):
    o_ref[0] = (x_ref[0] * (1.0 + scale_ref[0, 0]) + shift_ref[0, 0]).astype(o_ref.dtype)


def _modulate(x, mod4, layer):
    b, t, d = x.shape
    tt = MODULATE_ROWS
    return pl.pallas_call(
        _modulate_kernel,
        grid=(b, t // tt),
        in_specs=[pl.BlockSpec((1, tt, d), lambda i, j: (i, j, 0)),
                  pl.BlockSpec((1, 1, 1, d), lambda i, j: (layer, i, 0, 0)),
                  pl.BlockSpec((1, 1, 1, d), lambda i, j: (layer, i, 0, 1))],
        out_specs=pl.BlockSpec((1, tt, d), lambda i, j: (i, j, 0)),
        out_shape=jax.ShapeDtypeStruct((b, t, d), MXU_DTYPE),
        compiler_params=_params("parallel", "parallel"),
        name="modulate",
    )(x, mod4, mod4)


def _prep_w_in_kernel(w_ref, o_ref, prev_ref, *, gl_blk, n_gl):
    j = pl.program_id(1)
    tn = w_ref.shape[1]

    @pl.when(j < gl_blk)
    def _():
        o_ref[0] = w_ref[0].T.astype(o_ref.dtype)

    @pl.when(j == gl_blk)
    def _():
        row = lax.broadcasted_iota(jnp.int32, w_ref.shape[1:], 0)
        o_ref[0] = jnp.where(row < n_gl, w_ref[0], 0.0).T.astype(o_ref.dtype)

    @pl.when(j > gl_blk)
    def _():
        tile = jnp.concatenate([prev_ref[n_gl:tn, :], w_ref[0, 0:n_gl, :]], axis=0)
        o_ref[0] = tile.T.astype(o_ref.dtype)

    prev_ref[...] = w_ref[0]


def _prep_w_in(w_in_t, c_gl, n_gl, tn):
    depth, n, d = w_in_t.shape
    assert c_gl % tn == 0 and (n - n_gl) % tn == 0 and n_gl % SUBLANES == 0
    n_out = n - n_gl + tn
    return pl.pallas_call(
        functools.partial(_prep_w_in_kernel, gl_blk=c_gl // tn, n_gl=n_gl),
        grid=(depth, n_out // tn),
        in_specs=[pl.BlockSpec((1, tn, d), lambda l, j: (l, j, 0))],
        out_specs=pl.BlockSpec((1, d, tn), lambda l, j: (l, 0, j)),
        out_shape=jax.ShapeDtypeStruct((depth, d, n_out), MXU_DTYPE),
        scratch_shapes=[pltpu.VMEM((tn, d), F32)],
        compiler_params=_params("arbitrary", "arbitrary"),
        name="prep_w_in",
    )(w_in_t)


def _in_proj_kernel(a_ref, w_ref, o_ref):
    o_ref[...] = jnp.dot(a_ref[...], w_ref[0], preferred_element_type=F32)


def _in_proj(u, w_in_p, layer):
    m, d = u.shape
    n = w_in_p.shape[2]
    tm, tn = IN_PROJ_TILE
    return pl.pallas_call(
        _in_proj_kernel,
        grid=(m // tm, n // tn),
        in_specs=[pl.BlockSpec((tm, d), lambda i, j: (i, 0)),
                  pl.BlockSpec((1, d, tn), lambda i, j: (layer, 0, j))],
        out_specs=pl.BlockSpec((tm, tn), lambda i, j: (i, j)),
        out_shape=jax.ShapeDtypeStruct((m, n), F32),
        compiler_params=_params("parallel", "parallel"),
        name="in_proj",
    )(u, w_in_p)


def _rglru_kernel(x_ref, g_ref, cw_ref, cb_ref, wa_ref, ba_ref, wx_ref, bx_ref, lam_ref, o_ref,
                  xp_ref, a_ref, b_ref, h_ref):
    t, cb = x_ref.shape[1], x_ref.shape[2]
    pad = SUBLANES
    xp_ref[0:pad, :] = jnp.zeros((pad, cb), F32)
    xp_ref[pad:, :] = x_ref[0]
    xc = cb_ref[0]
    for j in range(RG_CONV):
        xc = xc + xp_ref[pl.ds(pad - (RG_CONV - 1) + j, t), :] * cw_ref[0, j:j + 1, :]

    row = lax.broadcasted_iota(jnp.int32, (t, HEAD_DIM), 0)
    sub = jnp.bitwise_and(row, SUBLANES - 1)
    for n in range(cb // HEAD_DIM):
        sl = slice(n * HEAD_DIM, (n + 1) * HEAD_DIM)
        xb = xc[:, sl]
        r = _sigmoid(_mm(xb, wa_ref[0, n]) + ba_ref[0, :, sl])
        i = _sigmoid(_mm(xb, wx_ref[0, n]) + bx_ref[0, :, sl])
        neg_lam = -lam_ref[0, :, sl]
        softplus = jnp.maximum(neg_lam, 0.0) + jnp.log1p(jnp.exp(-jnp.abs(neg_lam)))
        log_a = (-RG_C * softplus) * r
        a = jnp.exp(log_a)
        mult = jnp.sqrt(-jnp.tanh(log_a) * (a * a + 1.0))
        mult = jnp.where(row == 0, 1.0, mult)
        bx = mult * (i * xb)
        s = 1
        while s < SUBLANES:
            a_sh = pltpu.roll(a, s, axis=0)
            b_sh = pltpu.roll(bx, s, axis=0)
            inside = sub >= s
            bx = jnp.where(inside, a * b_sh + bx, bx)
            a = jnp.where(inside, a * a_sh, a)
            s *= 2
        a_ref[:, sl] = a
        b_ref[:, sl] = bx

    def carry_rows(v, h):
        r0 = pl.multiple_of(v * SUBLANES, SUBLANES)
        hh = a_ref[pl.ds(r0, SUBLANES), :] * h + b_ref[pl.ds(r0, SUBLANES), :]
        h_ref[pl.ds(r0, SUBLANES), :] = hh
        return jnp.broadcast_to(hh[SUBLANES - 1:SUBLANES, :], hh.shape)

    lax.fori_loop(0, t // SUBLANES, carry_rows, jnp.zeros((SUBLANES, cb), F32))
    o_ref[0] = (h_ref[...] * _silu(g_ref[0])).astype(o_ref.dtype)


def _rglru(proj3, conv_w, conv_b, w_a, b_a, w_x, b_x, lam, layer, d_rg):
    b, t, _ = proj3.shape
    depth = conv_w.shape[0]
    cb = RG_COLS
    nblk = cb // HEAD_DIM
    ncb = d_rg // cb
    vec = lambda v: v.reshape(depth, 1, d_rg)
    vspec = pl.BlockSpec((1, 1, cb), lambda i, j: (layer, 0, j))
    wspec = pl.BlockSpec((1, nblk, HEAD_DIM, HEAD_DIM), lambda i, j: (layer, j, 0, 0))
    return pl.pallas_call(
        _rglru_kernel,
        grid=(b, ncb),
        in_specs=[pl.BlockSpec((1, t, cb), lambda i, j: (i, 0, j)),
                  pl.BlockSpec((1, t, cb), lambda i, j: (i, 0, ncb + j)),
                  pl.BlockSpec((1, RG_CONV, cb), lambda i, j: (layer, 0, j)),
                  vspec, wspec, vspec, wspec, vspec, vspec],
        out_specs=pl.BlockSpec((1, t, cb), lambda i, j: (i, 0, j)),
        out_shape=jax.ShapeDtypeStruct((b, t, d_rg), MXU_DTYPE),
        scratch_shapes=[pltpu.VMEM((t + SUBLANES, cb), F32), pltpu.VMEM((t, cb), F32),
                        pltpu.VMEM((t, cb), F32), pltpu.VMEM((t, cb), F32)],
        compiler_params=_params("parallel", "parallel"),
        name="rglru",
    )(proj3, proj3, conv_w, vec(conv_b), w_a, vec(b_a), w_x, vec(b_x), vec(lam))


def _hgrn2_kernel(q_ref, f_ref, v_ref, g_ref, lbr_ref, ng_ref, o_ref,
                  qs_ref, kk_ref, b_ref, oo_ref, *, layer):
    t = q_ref.shape[1]
    ch, sb = HG_CHUNK, HG_SUB
    nsb = ch // sb
    z = f_ref[0]
    ez = jnp.exp(-jnp.abs(z))
    log_sig = jnp.minimum(z, 0.0) - jnp.log(1.0 + ez)
    inv = 1.0 / (1.0 + ez)
    sig_neg = jnp.where(z >= 0.0, ez * inv, inv)
    if layer == 0:
        log_f = log_sig
        kk = sig_neg
    else:
        raw = lbr_ref[...]
        e = jnp.exp(raw - jnp.max(raw, axis=0, keepdims=True))
        p = e / jnp.sum(e, axis=0, keepdims=True)
        lb = p[1:2]
        for j in range(2, layer + 1):
            lb = lb + p[j:j + 1]
        log_lb = jnp.log(lb)
        other = jnp.log1p(-lb) + log_sig
        log_f = jnp.maximum(log_lb, other) + jnp.log(1.0 + jnp.exp(-jnp.abs(log_lb - other)))
        kk = (1.0 - lb) * sig_neg
    width = q_ref.shape[2]
    heads = range(width // HEAD_DIM)
    row = lax.broadcasted_iota(jnp.int32, (t, width), 0)
    in_chunk = jnp.bitwise_and(row, ch - 1)
    bcum = log_f
    s = 1
    while s < ch:
        bcum = bcum + jnp.where(in_chunk >= s, pltpu.roll(bcum, s, axis=0), 0.0)
        s *= 2
    b_ref[...] = bcum * LOG2E
    kk_ref[...] = kk
    qs_ref[...] = _silu(q_ref[0])

    ones = jnp.ones((HEAD_DIM, ch), MXU_DTYPE)
    lane = lax.broadcasted_iota(jnp.int32, (sb, ch), 1)
    subrow = lax.broadcasted_iota(jnp.int32, (sb, HEAD_DIM), 0)

    def chunk_head(r0, hd, st):
        cols = slice(hd * HEAD_DIM, (hd + 1) * HEAD_DIM)
        bq = b_ref[pl.ds(r0, ch), cols]
        qc = qs_ref[pl.ds(r0, ch), cols]
        kc = kk_ref[pl.ds(r0, ch), cols]
        vc = v_ref[0, pl.ds(r0, ch), cols]
        blast = bq[ch - 1:ch, :]
        o = _mm_nt(qc * jnp.exp2(bq), st)
        a_rows = []
        diag = []
        for blk in range(nsb):
            lo = blk * sb
            b_i, q_i, k_i = bq[lo:lo + sb], qc[lo:lo + sb], kc[lo:lo + sb]
            for s_ in range(sb):
                dec = jnp.where(subrow >= s_, jnp.exp2(b_i - b_i[s_:s_ + 1]), 0.0)
                diag.append(q_i * (k_i[s_:s_ + 1] * dec))
            if blk == 0:
                a_rows.append(jnp.zeros((sb, ch), F32))
            else:
                m_i = bq[lo - 1:lo]
                qd = q_i * jnp.exp2(b_i - m_i)
                kd = kc[0:lo] * jnp.exp2(m_i - bq[0:lo])
                kd = jnp.concatenate([kd, jnp.zeros((ch - lo, HEAD_DIM), F32)], axis=0)
                a_rows.append(_mm_nt(qd, kd))
        dsum = _mm(jnp.concatenate(diag, axis=0), ones)
        for blk in range(nsb):
            acc = a_rows[blk]
            for s_ in range(sb):
                idx = blk * sb + s_
                acc = acc + jnp.where(lane == idx, dsum[idx * sb:(idx + 1) * sb], 0.0)
            a_rows[blk] = acc
        a_mat = jnp.concatenate(a_rows, axis=0)
        oo_ref[pl.ds(r0, ch), cols] = o + _mm(a_mat, vc)
        kdec = kc * jnp.exp2(blast - bq)
        return st * jnp.exp2(blast) + _mm_tn(vc, kdec)

    def chunk(c, states):
        r0 = pl.multiple_of(c * ch, ch)
        return tuple(chunk_head(r0, hd, states[hd]) for hd in heads)

    lax.fori_loop(0, t // ch, chunk, tuple(jnp.zeros((HEAD_DIM, HEAD_DIM), F32) for _ in heads), unroll=HG_UNROLL)
    gate = _silu(g_ref[0])
    for hd in heads:
        cols = slice(hd * HEAD_DIM, (hd + 1) * HEAD_DIM)
        o = oo_ref[:, cols]
        o = o * lax.rsqrt(jnp.mean(o * o, axis=-1, keepdims=True) + RMS_EPS) * ng_ref[0]
        o_ref[0, :, cols] = (o * gate[:, cols]).astype(o_ref.dtype)


def _hgrn2(proj3, col0, lower_bounds, norm_g, layer, d_hg):
    b, t, _ = proj3.shape
    depth = lower_bounds.shape[0]
    width = HG_HEADS_PER_STEP * HEAD_DIM
    steps = d_hg // width
    assert col0 % width == 0 and d_hg % width == 0

    def colspec(k):
        return pl.BlockSpec((1, t, width), lambda i, h: (i, 0, col0 // width + k * steps + h))

    return pl.pallas_call(
        functools.partial(_hgrn2_kernel, layer=layer),
        grid=(b, steps),
        in_specs=[colspec(0), colspec(1), colspec(2), colspec(3),
                  pl.BlockSpec((depth, width), lambda i, h: (0, h)),
                  pl.BlockSpec((1, 1, HEAD_DIM), lambda i, h: (layer, 0, 0))],
        out_specs=pl.BlockSpec((1, t, width), lambda i, h: (i, 0, h)),
        out_shape=jax.ShapeDtypeStruct((b, t, d_hg), MXU_DTYPE),
        scratch_shapes=[pltpu.VMEM((t, width), F32)] * 4,
        compiler_params=_params("parallel", "parallel"),
        name="hgrn2",
    )(proj3, proj3, proj3, proj3, lower_bounds, norm_g.reshape(depth, 1, HEAD_DIM))


def _compress_kernel(*refs):
    x_refs = refs[:2 * NSA_KV]
    pek_ref, pev_ref, w1k_ref, w2k_ref, w1v_ref, w2v_ref, ko_ref, vo_ref = refs[2 * NSA_KV:]
    nrow = x_refs[0].shape[1] // CMP_STRIDE
    row = lax.broadcasted_iota(jnp.int32, (nrow, HEAD_DIM), 0)
    branches = ((pek_ref, w1k_ref, w2k_ref, ko_ref), (pev_ref, w1v_ref, w2v_ref, vo_ref))
    for kv, (pe_ref, w1_ref, w2_ref, out_ref) in enumerate(branches):
        for g in range(NSA_KV):
            x_ref = x_refs[kv * NSA_KV + g]
            lo = jnp.concatenate(
                [x_ref[0, pl.ds(i, nrow, stride=CMP_STRIDE), :] for i in range(CMP_STRIDE)], axis=1)
            hi = pltpu.roll(lo, nrow - 1, axis=0)
            blk = jnp.concatenate([lo, hi], axis=1) + pe_ref[0]
            hid = _silu(_mm(blk, w1_ref[0]))
            out = _mm(hid, w2_ref[0])
            out_ref[0, g] = jnp.where(row < nrow - 1, out, 0.0)


def _compress(proj3, col, pe_k, pe_v, w1_k, w2_k, w1_v, w2_v, layer):
    b, t, _ = proj3.shape
    nrow = t // CMP_STRIDE
    nx = 2 * NSA_KV
    assert col % HEAD_DIM == 0
    xspec = lambda k: pl.BlockSpec((1, t, HEAD_DIM), lambda i: (i, 0, col // HEAD_DIM + k))
    depth = pe_k.shape[0]
    hidden = w1_k.shape[2]
    flat = CMP_LEN * HEAD_DIM
    pespec = pl.BlockSpec((1, 1, flat), lambda i: (layer, 0, 0))
    w1spec = pl.BlockSpec((1, flat, hidden), lambda i: (layer, 0, 0))
    w2spec = pl.BlockSpec((1, hidden, HEAD_DIM), lambda i: (layer, 0, 0))
    ospec = pl.BlockSpec((1, NSA_KV, nrow, HEAD_DIM), lambda i: (i, 0, 0, 0))
    oshape = jax.ShapeDtypeStruct((b, NSA_KV, nrow, HEAD_DIM), F32)
    return pl.pallas_call(
        _compress_kernel,
        grid=(b,),
        in_specs=[xspec(k) for k in range(nx)] + [pespec, pespec, w1spec, w2spec, w1spec, w2spec],
        out_specs=[ospec, ospec],
        out_shape=[oshape, oshape],
        compiler_params=_params("parallel"),
        name="nsa_compress",
    )(*([proj3] * nx), pe_k.reshape(depth, 1, flat), pe_v.reshape(depth, 1, flat), w1_k, w2_k, w1_v, w2_v)


def _nsa_kernel(slt_ref, kaux_ref, kauxc_ref, q_ref, gl_ref, sg_ref, ks_ref, vs_ref, kw_ref, vw_ref,
                kc_ref, vc_ref, ov_ref, ge_ref, o_ref,
                ksx_ref, kwx_ref, vsb_ref, vwb_ref, kcx_ref, vcb_ref):
    qb = pl.program_id(1)
    q0 = qb * Q_BLOCK
    q0a = pl.multiple_of(q0, Q_BLOCK)
    nq = NSA_GROUP * Q_BLOCK
    gw = NSA_GROUP * HEAD_DIM
    t_all = ks_ref.shape[1]
    n_sel = t_all // SEL_LEN
    kchunk = NSA_KEY_CHUNK
    groups = range(NSA_KV)

    @pl.when(qb == 0)
    def _():
        for g in groups:
            cols = slice(g * HEAD_DIM, (g + 1) * HEAD_DIM)
            ksx_ref[g, :, 0:HEAD_DIM] = ks_ref[0, :, cols].astype(MXU_DTYPE)
            ksx_ref[g, :, HEAD_DIM:] = kaux_ref[...]
            kwx_ref[g, :, 0:HEAD_DIM] = kw_ref[0, :, cols].astype(MXU_DTYPE)
            kwx_ref[g, :, HEAD_DIM:] = kaux_ref[...]
            kcx_ref[g, :, 0:HEAD_DIM] = kc_ref[0, g].astype(MXU_DTYPE)
            kcx_ref[g, :, HEAD_DIM:] = kauxc_ref[...]
            vsb_ref[g] = vs_ref[0, :, cols].astype(MXU_DTYPE)
            vwb_ref[g] = vw_ref[0, :, cols].astype(MXU_DTYPE)
            vcb_ref[g] = vc_ref[0, g].astype(MXU_DTYPE)

    def stack(v):
        return jnp.concatenate([v] * NSA_GROUP, axis=0)

    def online(carry, s, v):
        m, l, acc = carry
        m_new = jnp.maximum(m, jnp.max(s, axis=1, keepdims=True))
        alpha = jnp.exp2(m - m_new)
        p = jnp.exp2(s - m_new)
        l = alpha * l + jnp.sum(p, axis=1, keepdims=True)
        acc = alpha * acc + jnp.dot(p.astype(MXU_DTYPE), v, preferred_element_type=F32)
        return m_new, l, acc

    def finish(carry):
        _, l, acc = carry
        return acc * jnp.where(l > 0.0, 1.0 / l, 0.0)

    init = (jnp.full((nq, 1), NEG_INF, F32), jnp.zeros((nq, 1), F32), jnp.zeros((nq, HEAD_DIM), F32))
    t_loc = lax.broadcasted_iota(jnp.int32, (Q_BLOCK, LANES), 0)
    lane = lax.broadcasted_iota(jnp.int32, (Q_BLOCK, LANES), 1)

    ncmp = kcx_ref.shape[1]
    pos_c = (CMP_LEN - 1) + CMP_STRIDE * lax.broadcasted_iota(jnp.int32, (Q_BLOCK, ncmp), 1)
    ok_c = stack(jnp.where(q0 + lax.broadcasted_iota(jnp.int32, (Q_BLOCK, ncmp), 0) >= pos_c, 1.0, 0.0)) > 0.5
    tri_low = stack(jnp.where(lax.broadcasted_iota(jnp.int32, (Q_BLOCK, Q_BLOCK), 1)
                              <= lax.broadcasted_iota(jnp.int32, (Q_BLOCK, Q_BLOCK), 0), 1.0, 0.0)) > 0.5
    n_full = WINDOW // SEL_LEN - 1
    win_bias = jnp.where(lane < n_sel,
                         jnp.where(lane >= qb - n_full, jnp.where(lane < qb, 0.0, MASK_VAL), MASK_VAL), 0.0)
    has_far = jnp.where(q0 >= WINDOW, 1.0, 0.0)
    ok_edge = stack(jnp.where(lane < Q_BLOCK,
                              jnp.where(lane > t_loc, has_far, 0.0),
                              jnp.where(lane - Q_BLOCK <= t_loc, 1.0, 0.0))
                    ) > 0.5
    w0 = pl.multiple_of(jnp.maximum(q0 - WINDOW, 0), Q_BLOCK)
    nblk = lax.broadcasted_iota(jnp.int32, (n_sel, Q_BLOCK), 0)
    forced = (nblk == 0) | (nblk == qb) | (nblk == qb - 1)

    q_plain, q_sel, o_cmp, o_win = [], [], [], []
    for g in groups:
        q = q_ref[0, :, g * gw:(g + 1) * gw] * (HEAD_DIM ** -0.5 * LOG2E)
        q_main = jnp.concatenate([q[:, r * HEAD_DIM:(r + 1) * HEAD_DIM] for r in range(NSA_GROUP)],
                                 axis=0).astype(MXU_DTYPE)
        slope_lanes = slt_ref[g]

        def with_aux(block_bias, q_main=q_main, slope_lanes=slope_lanes):
            return jnp.concatenate([q_main, (stack(block_bias) + slope_lanes).astype(MXU_DTYPE)], axis=1)

        qp = with_aux(jnp.zeros((Q_BLOCK, LANES), F32))
        q_plain.append(qp)

        carry = online(init, _mm_nt(with_aux(win_bias), kwx_ref[g, pl.ds(w0, WINDOW), :]),
                       vwb_ref[g, pl.ds(w0, WINDOW), :])
        k_edge = jnp.concatenate([kwx_ref[g, pl.ds(w0, Q_BLOCK), :], kwx_ref[g, pl.ds(q0a, Q_BLOCK), :]], axis=0)
        v_edge = jnp.concatenate([vwb_ref[g, pl.ds(w0, Q_BLOCK), :], vwb_ref[g, pl.ds(q0a, Q_BLOCK), :]], axis=0)
        s_edge = jnp.where(ok_edge, _mm_nt(qp, k_edge), MASK_VAL)
        o_win.append(finish(online(carry, s_edge, v_edge)))

        s_c = jnp.where(ok_c, _mm_nt(qp, kcx_ref[g]), MASK_VAL)
        p_c = jnp.exp2(s_c - jnp.maximum(jnp.max(s_c, axis=1, keepdims=True), NEG_INF))
        l_c = jnp.sum(p_c, axis=1, keepdims=True)
        p_c = p_c * jnp.where(l_c > 0.0, 1.0 / l_c, 0.0)
        o_cmp.append(jnp.dot(p_c.astype(MXU_DTYPE), vcb_ref[g], preferred_element_type=F32))

        p_sum = p_c[0:Q_BLOCK]
        for r in range(1, NSA_GROUP):
            p_sum = p_sum + p_c[r * Q_BLOCK:(r + 1) * Q_BLOCK]
        imp = lax.dot_general(ov_ref[...], p_sum, (((1,), (1,)), ((), ())),
                              precision=lax.Precision.HIGHEST, preferred_element_type=F32)[0:n_sel]
        imp = jnp.where(nblk > qb, NEG_INF, jnp.where(forced, FORCE_SCORE, imp))
        rank = jnp.zeros(imp.shape, F32)
        for m_ in range(n_sel):
            other = imp[m_:m_ + 1, :]
            rank = rank + jnp.where(nblk > m_, jnp.where(other >= imp, 1.0, 0.0), jnp.where(other > imp, 1.0, 0.0))
        sel_bias = jnp.where(rank < float(min(SEL_TOPN, n_sel)), jnp.where(nblk < qb, 0.0, MASK_VAL), MASK_VAL)
        sel_bias = jnp.concatenate([sel_bias, jnp.zeros((LANES - n_sel, Q_BLOCK), F32)], axis=0).T
        q_sel.append(with_aux(sel_bias))

    def sel_branch(n_chunks):
        def run():
            outs = []
            for g in groups:
                carry = init
                if n_chunks:
                    rows = slice(0, n_chunks * kchunk)
                    carry = online(carry, _mm_nt(q_sel[g], ksx_ref[g, rows, :]), vsb_ref[g, rows, :])
                s_own = jnp.where(tri_low, _mm_nt(q_plain[g], ksx_ref[g, pl.ds(q0a, Q_BLOCK), :]), MASK_VAL)
                outs.append(finish(online(carry, s_own, vsb_ref[g, pl.ds(q0a, Q_BLOCK), :])))
            return tuple(outs)
        return run

    max_chunks = (t_all - Q_BLOCK + kchunk - 1) // kchunk
    o_sel = lax.switch((q0 + kchunk - 1) // kchunk, [sel_branch(n) for n in range(max_chunks + 1)])

    def unstack(o):
        return jnp.concatenate([o[r * Q_BLOCK:(r + 1) * Q_BLOCK] for r in range(NSA_GROUP)], axis=1)

    sig = _sigmoid(gl_ref[0])
    sig_hi = sig.astype(MXU_DTYPE)
    sig_lo = (sig - sig_hi.astype(F32)).astype(MXU_DTYPE)
    for g in groups:
        out = None
        for i, branch in enumerate((o_cmp[g], o_sel[g], o_win[g])):
            gate = (jnp.dot(sig_hi, ge_ref[g, i], preferred_element_type=F32)
                    + jnp.dot(sig_lo, ge_ref[g, i], preferred_element_type=F32))
            term = gate * unstack(branch)
            out = term if out is None else out + term
        cols = slice(g * gw, (g + 1) * gw)
        o_ref[0, :, cols] = (out * _silu(sg_ref[0, :, cols])).astype(o_ref.dtype)


def _split3(v):
    bf = jnp.bfloat16
    hi = v.astype(bf).astype(np.float64)
    mid = (v - hi).astype(bf).astype(np.float64)
    lo = (v - hi - mid).astype(bf).astype(np.float64)
    return hi, mid, lo


def _slope_lanes():
    heads = NSA_KV * NSA_GROUP
    h = np.arange(1, heads + 1, dtype=np.float32)
    s = np.power(np.float32(2.0), -8.0 * h / heads).astype(np.float32).astype(np.float64) * LOG2E
    parts = _split3(s)
    out = np.zeros((heads, LANES), np.float32)
    for rep in range(2):
        for i, part in enumerate(parts):
            out[:, POS_LANE + 3 * rep + i] = part
    out = np.repeat(out.reshape(NSA_KV, NSA_GROUP, 1, LANES), Q_BLOCK, axis=2)
    return jnp.asarray(out.reshape(NSA_KV, NSA_GROUP * Q_BLOCK, LANES))


def _key_aux(pos, blocks):
    out = np.zeros((pos.shape[0], LANES), np.float32)
    if blocks:
        out[np.arange(pos.shape[0]), pos // SEL_LEN] = 1.0
    out[:, POS_LANE:POS_LANE + 3] = (SEL_LEN * (pos // SEL_LEN))[:, None]
    out[:, POS_LANE + 3:POS_LANE + 6] = (pos % SEL_LEN)[:, None]
    return jnp.asarray(out, dtype=MXU_DTYPE)


def _overlap_t(t, ncmp_rows):
    n_cmp = (t - CMP_LEN) // CMP_STRIDE + 1
    n_sel = t // SEL_LEN
    s_c = np.arange(n_cmp) * CMP_STRIDE
    s_s = np.arange(n_sel) * SEL_LEN
    ov = np.clip(np.minimum(s_c[:, None] + CMP_LEN, s_s[None, :] + SEL_LEN)
                 - np.maximum(s_c[:, None], s_s[None, :]), 0, None).astype(np.float32) / CMP_LEN
    out = np.zeros((LANES, ncmp_rows), np.float32)
    out[:n_sel, :n_cmp] = ov.T
    return jnp.asarray(out)


def _gate_expand():
    width = NSA_GROUP * HEAD_DIM
    e = np.zeros((NSA_KV, 3, LANES, width), np.float32)
    for g in range(NSA_KV):
        for r in range(NSA_GROUP):
            for i in range(3):
                e[g, i, (g * NSA_GROUP + r) * 3 + i, r * HEAD_DIM:(r + 1) * HEAD_DIM] = 1.0
    return jnp.asarray(e, dtype=MXU_DTYPE)


def _nsa(proj3, cols, k_cmp, v_cmp):
    b, t, _ = proj3.shape
    assert t // SEL_LEN <= POS_LANE and t % NSA_KEY_CHUNK == 0 and Q_BLOCK == SEL_LEN
    gw = NSA_GROUP * HEAD_DIM
    width = NSA_KV * gw
    kvw = NSA_KV * HEAD_DIM
    ncmp_rows = k_cmp.shape[2]
    q_c, ks_c, kw_c, gl_c, sg_c = cols
    assert q_c % width == 0 and sg_c % width == 0 and ks_c % kvw == 0 and kw_c % kvw == 0

    def kvspec(c, is_v):
        return pl.BlockSpec((1, t, kvw), lambda i, j: (i, 0, c // kvw + (1 if is_v else 0)))

    def whole(*shape):
        return pl.BlockSpec(shape, lambda i, j: (0,) * len(shape))

    cmpspec = pl.BlockSpec((1, NSA_KV, ncmp_rows, HEAD_DIM), lambda i, j: (i, 0, 0, 0))
    cmp_pos = (CMP_LEN - 1) + CMP_STRIDE * np.arange(ncmp_rows)
    return pl.pallas_call(
        _nsa_kernel,
        grid=(b, t // Q_BLOCK),
        in_specs=[whole(NSA_KV, NSA_GROUP * Q_BLOCK, LANES), whole(t, LANES), whole(ncmp_rows, LANES),
                  pl.BlockSpec((1, Q_BLOCK, width), lambda i, j: (i, j, q_c // width)),
                  pl.BlockSpec((1, Q_BLOCK, LANES), lambda i, j: (i, j, gl_c // LANES)),
                  pl.BlockSpec((1, Q_BLOCK, width), lambda i, j: (i, j, sg_c // width)),
                  kvspec(ks_c, False), kvspec(ks_c, True), kvspec(kw_c, False), kvspec(kw_c, True),
                  cmpspec, cmpspec, whole(LANES, ncmp_rows), whole(NSA_KV, 3, LANES, gw)],
        out_specs=pl.BlockSpec((1, Q_BLOCK, width), lambda i, j: (i, j, 0)),
        out_shape=jax.ShapeDtypeStruct((b, t, width), MXU_DTYPE),
        scratch_shapes=[pltpu.VMEM((NSA_KV, t, 2 * HEAD_DIM), MXU_DTYPE),
                        pltpu.VMEM((NSA_KV, t, 2 * HEAD_DIM), MXU_DTYPE),
                        pltpu.VMEM((NSA_KV, t, HEAD_DIM), MXU_DTYPE), pltpu.VMEM((NSA_KV, t, HEAD_DIM), MXU_DTYPE),
                        pltpu.VMEM((NSA_KV, ncmp_rows, 2 * HEAD_DIM), MXU_DTYPE),
                        pltpu.VMEM((NSA_KV, ncmp_rows, HEAD_DIM), MXU_DTYPE)],
        compiler_params=_params("parallel", "arbitrary"),
        name="nsa_attn",
    )(_slope_lanes(), _key_aux(np.arange(t), True), _key_aux(cmp_pos, False),
      proj3, proj3, proj3, proj3, proj3, proj3, proj3, k_cmp, v_cmp, _overlap_t(t, ncmp_rows), _gate_expand())


def _out_ln_kernel(a0_ref, a1_ref, a2_ref, a3_ref, w0_ref, w1_ref, w2_ref, w3_ref,
                   x_ref, gate_ref, lng_ref, lnb_ref, shift_ref, scale_ref, *rest, alpha, emit_u):
    if emit_u:
        o_ref, u_ref, pre_ref, c_ref, s1_ref, s2_ref = rest
    else:
        (o_ref, pre_ref, c_ref, s1_ref, s2_ref), u_ref = rest, None
    i, j = pl.program_id(0), pl.program_id(1)
    n_row, nj = pl.num_programs(0) - 1, pl.num_programs(1)
    tn = o_ref.shape[1]
    slot = lax.rem(i, 2)

    @pl.when((i == 0) & (j == 0))
    def _():
        c_ref[...] = jnp.zeros(c_ref.shape, F32)
        s1_ref[...] = jnp.zeros(s1_ref.shape, F32)
        s2_ref[...] = jnp.zeros(s2_ref.shape, F32)

    def build():
        y = (jnp.dot(a0_ref[...], w0_ref[0], preferred_element_type=F32)
             + jnp.dot(a1_ref[...], w1_ref[0], preferred_element_type=F32)
             + jnp.dot(a2_ref[...], w2_ref[0], preferred_element_type=F32)
             + jnp.dot(a3_ref[...], w3_ref[0], preferred_element_type=F32))
        pre_ref[slot * nj + j] = alpha * x_ref[...] + (1.0 + gate_ref[0, 0]) * y

    def add_stats(sidx, jj):
        v = pre_ref[sidx * nj + jj]
        keep = jnp.where(jj == 0, 0.0, 1.0)
        c = keep * c_ref[sidx] + (1.0 - keep) * (v.sum(axis=-1, keepdims=True) / tn)
        c_ref[sidx] = c
        dev = v - c
        s1_ref[sidx] = keep * s1_ref[sidx] + dev.sum(axis=-1, keepdims=True)
        s2_ref[sidx] = keep * s2_ref[sidx] + (dev * dev).sum(axis=-1, keepdims=True)

    def normalise():
        d = nj * tn
        prev = 1 - slot
        off = s1_ref[prev] / d
        mu = c_ref[prev] + off
        inv = lax.rsqrt(s2_ref[prev] / d - off * off + LN_EPS)
        out = (pre_ref[prev * nj + j] - mu) * inv * lng_ref[0] + lnb_ref[0]
        o_ref[...] = out
        if emit_u:
            u_ref[...] = (out * (1.0 + scale_ref[0, 0]) + shift_ref[0, 0]).astype(u_ref.dtype)

    @pl.when(i == 0)
    def _():
        pl.when(j > 0)(lambda: add_stats(slot, j - 1))
        build()

    @pl.when((i > 0) & (i < n_row))
    def _():
        add_stats(jnp.where(j > 0, slot, 1 - slot), jnp.where(j > 0, j - 1, nj - 1))
        normalise()
        build()

    @pl.when(i == n_row)
    def _():
        pl.when(j == 0)(lambda: add_stats(1 - slot, nj - 1))
        normalise()


def _out_ln(y_rg, y_nsa, y_hg, w_out_b, x2, mod4, ln_g, ln_b, layer, alpha, t, emit_u):
    m, d = x2.shape
    depth = w_out_b.shape[0]
    kb = y_rg.shape[1]
    assert y_nsa.shape[1] == 2 * kb and y_hg.shape[1] == kb and d == 4 * kb
    tm, tn = OUT_TILE
    nj, n_row = d // tn, m // tm
    assert t % tm == 0
    mrow = lambda i: jnp.minimum(i, n_row - 1)
    mcol = lambda i, j: jnp.where(i < n_row, j, nj - 1)
    orow = lambda i: jnp.maximum(i - 1, 0)
    ocol = lambda i, j: jnp.where(i > 0, j, 0)
    aspec = lambda c: pl.BlockSpec((tm, kb), lambda i, j: (mrow(i), c))
    wspec = lambda r: pl.BlockSpec((1, kb, tn), lambda i, j: (layer, r, mcol(i, j)))
    vspec = pl.BlockSpec((1, 1, tn), lambda i, j: (layer, 0, ocol(i, j)))
    nxt = min(layer + 1, depth - 1)
    modspec = lambda part: pl.BlockSpec(
        (1, 1, 1, tn), lambda i, j: (nxt, (orow(i) * tm) // t, 0, part * nj + ocol(i, j)))
    ospec = pl.BlockSpec((tm, tn), lambda i, j: (orow(i), ocol(i, j)))
    oshape = jax.ShapeDtypeStruct((m, d), F32)
    return pl.pallas_call(
        functools.partial(_out_ln_kernel, alpha=alpha, emit_u=emit_u),
        grid=(n_row + 1, nj),
        in_specs=[aspec(0), aspec(0), aspec(1), aspec(0), wspec(0), wspec(1), wspec(2), wspec(3),
                  pl.BlockSpec((tm, tn), lambda i, j: (mrow(i), mcol(i, j))),
                  pl.BlockSpec((1, 1, 1, tn), lambda i, j: (layer, (mrow(i) * tm) // t, 0, 2 * nj + mcol(i, j))),
                  vspec, vspec, modspec(0), modspec(1)],
        out_specs=[ospec, ospec] if emit_u else ospec,
        out_shape=[oshape, jax.ShapeDtypeStruct((m, d), MXU_DTYPE)] if emit_u else oshape,
        scratch_shapes=[pltpu.VMEM((2 * nj, tm, tn), F32)] + [pltpu.VMEM((2, tm, 1), F32)] * 3,
        compiler_params=_params("arbitrary", "arbitrary"),
        name="out_ln",
    )(y_rg, y_nsa, y_nsa, y_hg, w_out_b, w_out_b, w_out_b, w_out_b, x2, mod4,
      ln_g.reshape(depth, 1, d), ln_b.reshape(depth, 1, d), mod4, mod4)


def kernel(x, c, w_ada, b_ada, w_in, rg_conv_w, rg_conv_b, rg_w_a, rg_b_a, rg_w_x, rg_b_x, rg_lambda,
           nsa_pe_k, nsa_pe_v, nsa_cmp_w1_k, nsa_cmp_w2_k, nsa_cmp_w1_v, nsa_cmp_w2_v,
           hg_lower_bounds, hg_norm_g, w_out, ln_g, ln_b):
    b, t, d = x.shape
    depth = w_ada.shape[0]
    m = b * t
    d_rg, d_nsa, d_hg = d // 4, d // 2, d // 4
    kv_cols = 2 * NSA_KV * HEAD_DIM
    n_gl = 3 * NSA_KV * NSA_GROUP
    alpha = (2.0 * depth) ** 0.25
    assert d_nsa == NSA_KV * NSA_GROUP * HEAD_DIM and b <= SUBLANES

    gl_pad = 512
    c_q = 2 * d_rg
    c_kvc = c_q + d_nsa
    c_kvs = c_kvc + kv_cols
    c_kvw = c_kvs + kv_cols
    c_gl = c_kvw + kv_cols
    c_sg = c_gl + gl_pad
    c_hg = c_sg + d_nsa
    w_in_p = _prep_w_in(jnp.swapaxes(w_in, 1, 2), c_gl, n_gl, gl_pad)
    w_out_b = w_out.astype(MXU_DTYPE)
    w1_k, w1_v = nsa_cmp_w1_k.astype(MXU_DTYPE), nsa_cmp_w1_v.astype(MXU_DTYPE)

    c_pad = jnp.pad(c, ((0, SUBLANES - b), (0, 0)))
    mod4 = _ada(c_pad, w_ada, b_ada).reshape(depth, SUBLANES, 1, 3 * d)

    u = _modulate(x, mod4, 0).reshape(m, d)
    x = x.reshape(m, d)
    for layer in range(depth):
        proj3 = _in_proj(u, w_in_p, layer).reshape(b, t, -1)
        y_rg = _rglru(proj3, rg_conv_w, rg_conv_b, rg_w_a, rg_b_a, rg_w_x, rg_b_x, rg_lambda, layer, d_rg)
        k_cmp, v_cmp = _compress(proj3, c_kvc, nsa_pe_k, nsa_pe_v, w1_k, nsa_cmp_w2_k, w1_v, nsa_cmp_w2_v, layer)
        y_nsa = _nsa(proj3, (c_q, c_kvs, c_kvw, c_gl, c_sg), k_cmp, v_cmp)
        y_hg = _hgrn2(proj3, c_hg, hg_lower_bounds, hg_norm_g, layer, d_hg)
        emit_u = layer + 1 < depth
        res = _out_ln(y_rg.reshape(m, d_rg), y_nsa.reshape(m, d_nsa), y_hg.reshape(m, d_hg), w_out_b,
                      x, mod4, ln_g, ln_b, layer, alpha, t, emit_u)
        x, u = res if emit_u else (res, None)
    return x.reshape(b, t, d)
```

```python
import functools

import numpy as np
import jax
import jax.numpy as jnp
from jax import lax
from jax.experimental import pallas as pl
from jax.experimental.pallas import tpu as pltpu

F32 = jnp.float32
MXU_DTYPE = jnp.bfloat16

HEAD_DIM = 128
RG_CONV = 4
RG_C = 8.0
NSA_KV = 2
NSA_GROUP = 8
CMP_LEN = 32
CMP_STRIDE = 16
SEL_LEN = 64
SEL_TOPN = 16
WINDOW = 512
Q_BLOCK = 64
HG_CHUNK = 128
HG_SUB = 8
LN_EPS = 1e-5
RMS_EPS = 1e-6
NEG_INF = -1e30
FORCE_SCORE = 1e9
MASK_VAL = -(2.0 ** 100)
LOG2E = 1.4426950408889634
POS_LANE = 32

LANES = 128
SUBLANES = 8
VMEM_LIMIT = 56 * 1024 * 1024

ADA_TN = 1024
MODULATE_ROWS = 512
IN_PROJ_TILE = (1024, 1024)
OUT_TILE = (512, 1024)
RG_COLS = 256
HG_HEADS_PER_STEP = 2
HG_UNROLL = 16
NSA_KEY_CHUNK = 512


def _mm(a, b):
    return jnp.dot(a.astype(MXU_DTYPE), b.astype(MXU_DTYPE), preferred_element_type=F32)


def _mm_nt(a, b):
    return lax.dot_general(a.astype(MXU_DTYPE), b.astype(MXU_DTYPE),
                           (((1,), (1,)), ((), ())), preferred_element_type=F32)


def _mm_tn(a, b):
    return lax.dot_general(a.astype(MXU_DTYPE), b.astype(MXU_DTYPE),
                           (((0,), (0,)), ((), ())), preferred_element_type=F32)


def _sigmoid(v):
    return jax.nn.sigmoid(v)


def _silu(v):
    return v * jax.nn.sigmoid(v)


def _params(*semantics):
    return pltpu.CompilerParams(dimension_semantics=semantics, vmem_limit_bytes=VMEM_LIMIT)


def _ada_kernel(c_ref, w_ref, b_ref, o_ref):
    o_ref[0] = _mm(c_ref[...], w_ref[0]) + b_ref[0]


def _ada(c_pad, w_ada, b_ada):
    depth, d, n3 = w_ada.shape
    rows = c_pad.shape[0]
    tn = ADA_TN
    return pl.pallas_call(
        _ada_kernel,
        grid=(depth, n3 // tn),
        in_specs=[pl.BlockSpec((rows, d), lambda l, j: (0, 0)),
                  pl.BlockSpec((1, d, tn), lambda l, j: (l, 0, j)),
                  pl.BlockSpec((1, 1, tn), lambda l, j: (l, 0, j))],
        out_specs=pl.BlockSpec((1, rows, tn), lambda l, j: (l, 0, j)),
        out_shape=jax.ShapeDtypeStruct((depth, rows, n3), F32),
        compiler_params=_params("parallel", "parallel"),
        name="ada",
    )(c_pad, w_ada, b_ada.reshape(depth, 1, n3))


def _modulate_kernel(x_ref, shift_ref, scale_ref, o_ref):
    o_ref[0] = (x_ref[0] * (1.0 + scale_ref[0, 0]) + shift_ref[0, 0]).astype(o_ref.dtype)


def _modulate(x, mod4, layer):
    b, t, d = x.shape
    tt = MODULATE_ROWS
    return pl.pallas_call(
        _modulate_kernel,
        grid=(b, t // tt),
        in_specs=[pl.BlockSpec((1, tt, d), lambda i, j: (i, j, 0)),
                  pl.BlockSpec((1, 1, 1, d), lambda i, j: (layer, i, 0, 0)),
                  pl.BlockSpec((1, 1, 1, d), lambda i, j: (layer, i, 0, 1))],
        out_specs=pl.BlockSpec((1, tt, d), lambda i, j: (i, j, 0)),
        out_shape=jax.ShapeDtypeStruct((b, t, d), MXU_DTYPE),
        compiler_params=_params("parallel", "parallel"),
        name="modulate",
    )(x, mod4, mod4)


def _prep_w_in_kernel(w_ref, o_ref, prev_ref, *, gl_blk, n_gl):
    j = pl.program_id(1)
    tn = w_ref.shape[1]

    @pl.when(j < gl_blk)
    def _():
        o_ref[0] = w_ref[0].T.astype(o_ref.dtype)

    @pl.when(j == gl_blk)
    def _():
        row = lax.broadcasted_iota(jnp.int32, w_ref.shape[1:], 0)
        o_ref[0] = jnp.where(row < n_gl, w_ref[0], 0.0).T.astype(o_ref.dtype)

    @pl.when(j > gl_blk)
    def _():
        tile = jnp.concatenate([prev_ref[n_gl:tn, :], w_ref[0, 0:n_gl, :]], axis=0)
        o_ref[0] = tile.T.astype(o_ref.dtype)

    prev_ref[...] = w_ref[0]


def _prep_w_in(w_in_t, c_gl, n_gl, tn):
    depth, n, d = w_in_t.shape
    assert c_gl % tn == 0 and (n - n_gl) % tn == 0 and n_gl % SUBLANES == 0
    n_out = n - n_gl + tn
    return pl.pallas_call(
        functools.partial(_prep_w_in_kernel, gl_blk=c_gl // tn, n_gl=n_gl),
        grid=(depth, n_out // tn),
        in_specs=[pl.BlockSpec((1, tn, d), lambda l, j: (l, j, 0))],
        out_specs=pl.BlockSpec((1, d, tn), lambda l, j: (l, 0, j)),
        out_shape=jax.ShapeDtypeStruct((depth, d, n_out), MXU_DTYPE),
        scratch_shapes=[pltpu.VMEM((tn, d), F32)],
        compiler_params=_params("arbitrary", "arbitrary"),
        name="prep_w_in",
    )(w_in_t)


def _in_proj_kernel(a_ref, w_ref, o_ref):
    o_ref[...] = jnp.dot(a_ref[...], w_ref[0], preferred_element_type=F32)


def _in_proj(u, w_in_p, layer):
    m, d = u.shape
    n = w_in_p.shape[2]
    tm, tn = IN_PROJ_TILE
    return pl.pallas_call(
        _in_proj_kernel,
        grid=(m // tm, n // tn),
        in_specs=[pl.BlockSpec((tm, d), lambda i, j: (i, 0)),
                  pl.BlockSpec((1, d, tn), lambda i, j: (layer, 0, j))],
        out_specs=pl.BlockSpec((tm, tn), lambda i, j: (i, j)),
        out_shape=jax.ShapeDtypeStruct((m, n), F32),
        compiler_params=_params("parallel", "parallel"),
        name="in_proj",
    )(u, w_in_p)


def _rglru_kernel(x_ref, g_ref, cw_ref, cb_ref, wa_ref, ba_ref, wx_ref, bx_ref, lam_ref, o_ref,
                  xp_ref, a_ref, b_ref, h_ref):
    t, cb = x_ref.shape[1], x_ref.shape[2]
    pad = SUBLANES
    xp_ref[0:pad, :] = jnp.zeros((pad, cb), F32)
    xp_ref[pad:, :] = x_ref[0]
    xc = cb_ref[0]
    for j in range(RG_CONV):
        xc = xc + xp_ref[pl.ds(pad - (RG_CONV - 1) + j, t), :] * cw_ref[0, j:j + 1, :]

    row = lax.broadcasted_iota(jnp.int32, (t, HEAD_DIM), 0)
    sub = jnp.bitwise_and(row, SUBLANES - 1)
    for n in range(cb // HEAD_DIM):
        sl = slice(n * HEAD_DIM, (n + 1) * HEAD_DIM)
        xb = xc[:, sl]
        r = _sigmoid(_mm(xb, wa_ref[0, n]) + ba_ref[0, :, sl])
        i = _sigmoid(_mm(xb, wx_ref[0, n]) + bx_ref[0, :, sl])
        neg_lam = -lam_ref[0, :, sl]
        softplus = jnp.maximum(neg_lam, 0.0) + jnp.log1p(jnp.exp(-jnp.abs(neg_lam)))
        log_a = (-RG_C * softplus) * r
        a = jnp.exp(log_a)
        mult = jnp.sqrt(-jnp.tanh(log_a) * (a * a + 1.0))
        mult = jnp.where(row == 0, 1.0, mult)
        bx = mult * (i * xb)
        s = 1
        while s < SUBLANES:
            a_sh = pltpu.roll(a, s, axis=0)
            b_sh = pltpu.roll(bx, s, axis=0)
            inside = sub >= s
            bx = jnp.where(inside, a * b_sh + bx, bx)
            a = jnp.where(inside, a * a_sh, a)
            s *= 2
        a_ref[:, sl] = a
        b_ref[:, sl] = bx

    def carry_rows(v, h):
        r0 = pl.multiple_of(v * SUBLANES, SUBLANES)
        hh = a_ref[pl.ds(r0, SUBLANES), :] * h + b_ref[pl.ds(r0, SUBLANES), :]
        h_ref[pl.ds(r0, SUBLANES), :] = hh
        return jnp.broadcast_to(hh[SUBLANES - 1:SUBLANES, :], hh.shape)

    lax.fori_loop(0, t // SUBLANES, carry_rows, jnp.zeros((SUBLANES, cb), F32), unroll=8)
    o_ref[0] = (h_ref[...] * _silu(g_ref[0])).astype(o_ref.dtype)


def _rglru(proj3, conv_w, conv_b, w_a, b_a, w_x, b_x, lam, layer, d_rg):
    b, t, _ = proj3.shape
    depth = conv_w.shape[0]
    cb = RG_COLS
    nblk = cb // HEAD_DIM
    ncb = d_rg // cb
    vec = lambda v: v.reshape(depth, 1, d_rg)
    vspec = pl.BlockSpec((1, 1, cb), lambda i, j: (layer, 0, j))
    wspec = pl.BlockSpec((1, nblk, HEAD_DIM, HEAD_DIM), lambda i, j: (layer, j, 0, 0))
    return pl.pallas_call(
        _rglru_kernel,
        grid=(b, ncb),
        in_specs=[pl.BlockSpec((1, t, cb), lambda i, j: (i, 0, j)),
                  pl.BlockSpec((1, t, cb), lambda i, j: (i, 0, ncb + j)),
                  pl.BlockSpec((1, RG_CONV, cb), lambda i, j: (layer, 0, j)),
                  vspec, wspec, vspec, wspec, vspec, vspec],
        out_specs=pl.BlockSpec((1, t, cb), lambda i, j: (i, 0, j)),
        out_shape=jax.ShapeDtypeStruct((b, t, d_rg), MXU_DTYPE),
        scratch_shapes=[pltpu.VMEM((t + SUBLANES, cb), F32), pltpu.VMEM((t, cb), F32),
                        pltpu.VMEM((t, cb), F32), pltpu.VMEM((t, cb), F32)],
        compiler_params=_params("parallel", "parallel"),
        name="rglru",
    )(proj3, proj3, conv_w, vec(conv_b), w_a, vec(b_a), w_x, vec(b_x), vec(lam))


def _hgrn2_kernel(q_ref, f_ref, v_ref, g_ref, lbr_ref, ng_ref, o_ref,
                  qs_ref, kk_ref, b_ref, oo_ref, *, layer):
    t = q_ref.shape[1]
    ch, sb = HG_CHUNK, HG_SUB
    nsb = ch // sb
    z = f_ref[0]
    ez = jnp.exp(-jnp.abs(z))
    log_sig = jnp.minimum(z, 0.0) - jnp.log(1.0 + ez)
    inv = 1.0 / (1.0 + ez)
    sig_neg = jnp.where(z >= 0.0, ez * inv, inv)
    if layer == 0:
        log_f = log_sig
        kk = sig_neg
    else:
        raw = lbr_ref[...]
        e = jnp.exp(raw - jnp.max(raw, axis=0, keepdims=True))
        p = e / jnp.sum(e, axis=0, keepdims=True)
        lb = p[1:2]
        for j in range(2, layer + 1):
            lb = lb + p[j:j + 1]
        log_lb = jnp.log(lb)
        other = jnp.log1p(-lb) + log_sig
        log_f = jnp.maximum(log_lb, other) + jnp.log(1.0 + jnp.exp(-jnp.abs(log_lb - other)))
        kk = (1.0 - lb) * sig_neg
    width = q_ref.shape[2]
    heads = range(width // HEAD_DIM)
    row = lax.broadcasted_iota(jnp.int32, (t, width), 0)
    in_chunk = jnp.bitwise_and(row, ch - 1)
    bcum = log_f
    s = 1
    while s < ch:
        bcum = bcum + jnp.where(in_chunk >= s, pltpu.roll(bcum, s, axis=0), 0.0)
        s *= 2
    b_ref[...] = bcum * LOG2E
    kk_ref[...] = kk
    qs_ref[...] = _silu(q_ref[0])

    ones = jnp.ones((HEAD_DIM, ch), MXU_DTYPE)
    lane = lax.broadcasted_iota(jnp.int32, (sb, ch), 1)
    subrow = lax.broadcasted_iota(jnp.int32, (sb, HEAD_DIM), 0)

    def chunk_head(r0, hd, st):
        cols = slice(hd * HEAD_DIM, (hd + 1) * HEAD_DIM)
        bq = b_ref[pl.ds(r0, ch), cols]
        qc = qs_ref[pl.ds(r0, ch), cols]
        kc = kk_ref[pl.ds(r0, ch), cols]
        vc = v_ref[0, pl.ds(r0, ch), cols]
        blast = bq[ch - 1:ch, :]
        o = _mm_nt(qc * jnp.exp2(bq), st)
        a_rows = []
        diag = []
        for blk in range(nsb):
            lo = blk * sb
            b_i, q_i, k_i = bq[lo:lo + sb], qc[lo:lo + sb], kc[lo:lo + sb]
            for s_ in range(sb):
                dec = jnp.where(subrow >= s_, jnp.exp2(b_i - b_i[s_:s_ + 1]), 0.0)
                diag.append(q_i * (k_i[s_:s_ + 1] * dec))
            if blk == 0:
                a_rows.append(jnp.zeros((sb, ch), F32))
            else:
                m_i = bq[lo - 1:lo]
                qd = q_i * jnp.exp2(b_i - m_i)
                kd = kc[0:lo] * jnp.exp2(m_i - bq[0:lo])
                kd = jnp.concatenate([kd, jnp.zeros((ch - lo, HEAD_DIM), F32)], axis=0)
                a_rows.append(_mm_nt(qd, kd))
        dsum = _mm(jnp.concatenate(diag, axis=0), ones)
        for blk in range(nsb):
            acc = a_rows[blk]
            for s_ in range(sb):
                idx = blk * sb + s_
                acc = acc + jnp.where(lane == idx, dsum[idx * sb:(idx + 1) * sb], 0.0)
            a_rows[blk] = acc
        a_mat = jnp.concatenate(a_rows, axis=0)
        oo_ref[pl.ds(r0, ch), cols] = o + _mm(a_mat, vc)
        kdec = kc * jnp.exp2(blast - bq)
        return st * jnp.exp2(blast) + _mm_tn(vc, kdec)

    def chunk(c, states):
        r0 = pl.multiple_of(c * ch, ch)
        return tuple(chunk_head(r0, hd, states[hd]) for hd in heads)

    lax.fori_loop(0, t // ch, chunk, tuple(jnp.zeros((HEAD_DIM, HEAD_DIM), F32) for _ in heads), unroll=HG_UNROLL)
    gate = _silu(g_ref[0])
    for hd in heads:
        cols = slice(hd * HEAD_DIM, (hd + 1) * HEAD_DIM)
        o = oo_ref[:, cols]
        o = o * lax.rsqrt(jnp.mean(o * o, axis=-1, keepdims=True) + RMS_EPS) * ng_ref[0]
        o_ref[0, :, cols] = (o * gate[:, cols]).astype(o_ref.dtype)


def _hgrn2(proj3, col0, lower_bounds, norm_g, layer, d_hg):
    b, t, _ = proj3.shape
    depth = lower_bounds.shape[0]
    width = HG_HEADS_PER_STEP * HEAD_DIM
    steps = d_hg // width
    assert col0 % width == 0 and d_hg % width == 0

    def colspec(k):
        return pl.BlockSpec((1, t, width), lambda i, h: (i, 0, col0 // width + k * steps + h))

    return pl.pallas_call(
        functools.partial(_hgrn2_kernel, layer=layer),
        grid=(b, steps),
        in_specs=[colspec(0), colspec(1), colspec(2), colspec(3),
                  pl.BlockSpec((depth, width), lambda i, h: (0, h)),
                  pl.BlockSpec((1, 1, HEAD_DIM), lambda i, h: (layer, 0, 0))],
        out_specs=pl.BlockSpec((1, t, width), lambda i, h: (i, 0, h)),
        out_shape=jax.ShapeDtypeStruct((b, t, d_hg), MXU_DTYPE),
        scratch_shapes=[pltpu.VMEM((t, width), F32)] * 4,
        compiler_params=_params("parallel", "parallel"),
        name="hgrn2",
    )(proj3, proj3, proj3, proj3, lower_bounds, norm_g.reshape(depth, 1, HEAD_DIM))


def _compress_kernel(*refs):
    x_refs = refs[:2 * NSA_KV]
    pek_ref, pev_ref, w1k_ref, w2k_ref, w1v_ref, w2v_ref, ko_ref, vo_ref = refs[2 * NSA_KV:]
    nrow = x_refs[0].shape[1] // CMP_STRIDE
    row = lax.broadcasted_iota(jnp.int32, (nrow, HEAD_DIM), 0)
    branches = ((pek_ref, w1k_ref, w2k_ref, ko_ref), (pev_ref, w1v_ref, w2v_ref, vo_ref))
    for kv, (pe_ref, w1_ref, w2_ref, out_ref) in enumerate(branches):
        for g in range(NSA_KV):
            x_ref = x_refs[kv * NSA_KV + g]
            lo = jnp.concatenate(
                [x_ref[0, pl.ds(i, nrow, stride=CMP_STRIDE), :] for i in range(CMP_STRIDE)], axis=1)
            hi = pltpu.roll(lo, nrow - 1, axis=0)
            blk = jnp.concatenate([lo, hi], axis=1) + pe_ref[0]
            hid = _silu(_mm(blk, w1_ref[0]))
            out = _mm(hid, w2_ref[0])
            out_ref[0, g] = jnp.where(row < nrow - 1, out, 0.0)


def _compress(proj3, col, pe_k, pe_v, w1_k, w2_k, w1_v, w2_v, layer):
    b, t, _ = proj3.shape
    nrow = t // CMP_STRIDE
    nx = 2 * NSA_KV
    assert col % HEAD_DIM == 0
    xspec = lambda k: pl.BlockSpec((1, t, HEAD_DIM), lambda i: (i, 0, col // HEAD_DIM + k))
    depth = pe_k.shape[0]
    hidden = w1_k.shape[2]
    flat = CMP_LEN * HEAD_DIM
    pespec = pl.BlockSpec((1, 1, flat), lambda i: (layer, 0, 0))
    w1spec = pl.BlockSpec((1, flat, hidden), lambda i: (layer, 0, 0))
    w2spec = pl.BlockSpec((1, hidden, HEAD_DIM), lambda i: (layer, 0, 0))
    ospec = pl.BlockSpec((1, NSA_KV, nrow, HEAD_DIM), lambda i: (i, 0, 0, 0))
    oshape = jax.ShapeDtypeStruct((b, NSA_KV, nrow, HEAD_DIM), F32)
    return pl.pallas_call(
        _compress_kernel,
        grid=(b,),
        in_specs=[xspec(k) for k in range(nx)] + [pespec, pespec, w1spec, w2spec, w1spec, w2spec],
        out_specs=[ospec, ospec],
        out_shape=[oshape, oshape],
        compiler_params=_params("parallel"),
        name="nsa_compress",
    )(*([proj3] * nx), pe_k.reshape(depth, 1, flat), pe_v.reshape(depth, 1, flat), w1_k, w2_k, w1_v, w2_v)


def _nsa_kernel(slt_ref, kaux_ref, kauxc_ref, q_ref, gl_ref, sg_ref, ks_ref, vs_ref, kw_ref, vw_ref,
                kc_ref, vc_ref, ov_ref, ge_ref, o_ref,
                ksx_ref, kwx_ref, vsb_ref, vwb_ref, kcx_ref, vcb_ref):
    qb = pl.program_id(1)
    q0 = qb * Q_BLOCK
    q0a = pl.multiple_of(q0, Q_BLOCK)
    nq = NSA_GROUP * Q_BLOCK
    gw = NSA_GROUP * HEAD_DIM
    t_all = ks_ref.shape[1]
    n_sel = t_all // SEL_LEN
    kchunk = NSA_KEY_CHUNK
    groups = range(NSA_KV)

    @pl.when(qb == 0)
    def _():
        for g in groups:
            cols = slice(g * HEAD_DIM, (g + 1) * HEAD_DIM)
            ksx_ref[g, :, 0:HEAD_DIM] = ks_ref[0, :, cols].astype(MXU_DTYPE)
            ksx_ref[g, :, HEAD_DIM:] = kaux_ref[...]
            kwx_ref[g, :, 0:HEAD_DIM] = kw_ref[0, :, cols].astype(MXU_DTYPE)
            kwx_ref[g, :, HEAD_DIM:] = kaux_ref[...]
            kcx_ref[g, :, 0:HEAD_DIM] = kc_ref[0, g].astype(MXU_DTYPE)
            kcx_ref[g, :, HEAD_DIM:] = kauxc_ref[...]
            vsb_ref[g] = vs_ref[0, :, cols].astype(MXU_DTYPE)
            vwb_ref[g] = vw_ref[0, :, cols].astype(MXU_DTYPE)
            vcb_ref[g] = vc_ref[0, g].astype(MXU_DTYPE)

    def stack(v):
        return jnp.concatenate([v] * NSA_GROUP, axis=0)

    def online(carry, s, v):
        m, l, acc = carry
        m_new = jnp.maximum(m, jnp.max(s, axis=1, keepdims=True))
        alpha = jnp.exp2(m - m_new)
        p = jnp.exp2(s - m_new)
        l = alpha * l + jnp.sum(p, axis=1, keepdims=True)
        acc = alpha * acc + jnp.dot(p.astype(MXU_DTYPE), v, preferred_element_type=F32)
        return m_new, l, acc

    def finish(carry):
        _, l, acc = carry
        return acc * jnp.where(l > 0.0, 1.0 / l, 0.0)

    init = (jnp.full((nq, 1), NEG_INF, F32), jnp.zeros((nq, 1), F32), jnp.zeros((nq, HEAD_DIM), F32))
    t_loc = lax.broadcasted_iota(jnp.int32, (Q_BLOCK, LANES), 0)
    lane = lax.broadcasted_iota(jnp.int32, (Q_BLOCK, LANES), 1)

    ncmp = kcx_ref.shape[1]
    pos_c = (CMP_LEN - 1) + CMP_STRIDE * lax.broadcasted_iota(jnp.int32, (Q_BLOCK, ncmp), 1)
    ok_c = stack(jnp.where(q0 + lax.broadcasted_iota(jnp.int32, (Q_BLOCK, ncmp), 0) >= pos_c, 1.0, 0.0)) > 0.5
    tri_low = stack(jnp.where(lax.broadcasted_iota(jnp.int32, (Q_BLOCK, Q_BLOCK), 1)
                              <= lax.broadcasted_iota(jnp.int32, (Q_BLOCK, Q_BLOCK), 0), 1.0, 0.0)) > 0.5
    n_full = WINDOW // SEL_LEN - 1
    win_bias = jnp.where(lane < n_sel,
                         jnp.where(lane >= qb - n_full, jnp.where(lane < qb, 0.0, MASK_VAL), MASK_VAL), 0.0)
    has_far = jnp.where(q0 >= WINDOW, 1.0, 0.0)
    ok_edge = stack(jnp.where(lane < Q_BLOCK,
                              jnp.where(lane > t_loc, has_far, 0.0),
                              jnp.where(lane - Q_BLOCK <= t_loc, 1.0, 0.0))
                    ) > 0.5
    w0 = pl.multiple_of(jnp.maximum(q0 - WINDOW, 0), Q_BLOCK)
    nblk = lax.broadcasted_iota(jnp.int32, (n_sel, Q_BLOCK), 0)
    forced = (nblk == 0) | (nblk == qb) | (nblk == qb - 1)

    q_plain, q_sel, o_cmp, o_win = [], [], [], []
    for g in groups:
        q = q_ref[0, :, g * gw:(g + 1) * gw] * (HEAD_DIM ** -0.5 * LOG2E)
        q_main = jnp.concatenate([q[:, r * HEAD_DIM:(r + 1) * HEAD_DIM] for r in range(NSA_GROUP)],
                                 axis=0).astype(MXU_DTYPE)
        slope_lanes = slt_ref[g]

        def with_aux(block_bias, q_main=q_main, slope_lanes=slope_lanes):
            return jnp.concatenate([q_main, (stack(block_bias) + slope_lanes).astype(MXU_DTYPE)], axis=1)

        qp = with_aux(jnp.zeros((Q_BLOCK, LANES), F32))
        q_plain.append(qp)

        carry = online(init, _mm_nt(with_aux(win_bias), kwx_ref[g, pl.ds(w0, WINDOW), :]),
                       vwb_ref[g, pl.ds(w0, WINDOW), :])
        k_edge = jnp.concatenate([kwx_ref[g, pl.ds(w0, Q_BLOCK), :], kwx_ref[g, pl.ds(q0a, Q_BLOCK), :]], axis=0)
        v_edge = jnp.concatenate([vwb_ref[g, pl.ds(w0, Q_BLOCK), :], vwb_ref[g, pl.ds(q0a, Q_BLOCK), :]], axis=0)
        s_edge = jnp.where(ok_edge, _mm_nt(qp, k_edge), MASK_VAL)
        o_win.append(finish(online(carry, s_edge, v_edge)))

        s_c = jnp.where(ok_c, _mm_nt(qp, kcx_ref[g]), MASK_VAL)
        p_c = jnp.exp2(s_c - jnp.maximum(jnp.max(s_c, axis=1, keepdims=True), NEG_INF))
        l_c = jnp.sum(p_c, axis=1, keepdims=True)
        p_c = p_c * jnp.where(l_c > 0.0, 1.0 / l_c, 0.0)
        o_cmp.append(jnp.dot(p_c.astype(MXU_DTYPE), vcb_ref[g], preferred_element_type=F32))

        p_sum = p_c[0:Q_BLOCK]
        for r in range(1, NSA_GROUP):
            p_sum = p_sum + p_c[r * Q_BLOCK:(r + 1) * Q_BLOCK]
        imp = lax.dot_general(ov_ref[...], p_sum, (((1,), (1,)), ((), ())),
                              precision=lax.Precision.HIGHEST, preferred_element_type=F32)[0:n_sel]
        imp = jnp.where(nblk > qb, NEG_INF, jnp.where(forced, FORCE_SCORE, imp))
        rank = jnp.zeros(imp.shape, F32)
        for m_ in range(n_sel):
            other = imp[m_:m_ + 1, :]
            rank = rank + jnp.where(nblk > m_, jnp.where(other >= imp, 1.0, 0.0), jnp.where(other > imp, 1.0, 0.0))
        sel_bias = jnp.where(rank < float(min(SEL_TOPN, n_sel)), jnp.where(nblk < qb, 0.0, MASK_VAL), MASK_VAL)
        sel_bias = jnp.concatenate([sel_bias, jnp.zeros((LANES - n_sel, Q_BLOCK), F32)], axis=0).T
        q_sel.append(with_aux(sel_bias))

    def sel_branch(n_chunks):
        def run():
            outs = []
            for g in groups:
                carry = init
                if n_chunks:
                    rows = slice(0, n_chunks * kchunk)
                    carry = online(carry, _mm_nt(q_sel[g], ksx_ref[g, rows, :]), vsb_ref[g, rows, :])
                s_own = jnp.where(tri_low, _mm_nt(q_plain[g], ksx_ref[g, pl.ds(q0a, Q_BLOCK), :]), MASK_VAL)
                outs.append(finish(online(carry, s_own, vsb_ref[g, pl.ds(q0a, Q_BLOCK), :])))
            return tuple(outs)
        return run

    max_chunks = (t_all - Q_BLOCK + kchunk - 1) // kchunk
    o_sel = lax.switch((q0 + kchunk - 1) // kchunk, [sel_branch(n) for n in range(max_chunks + 1)])

    def unstack(o):
        return jnp.concatenate([o[r * Q_BLOCK:(r + 1) * Q_BLOCK] for r in range(NSA_GROUP)], axis=1)

    sig = _sigmoid(gl_ref[0])
    sig_hi = sig.astype(MXU_DTYPE)
    sig_lo = (sig - sig_hi.astype(F32)).astype(MXU_DTYPE)
    for g in groups:
        out = None
        for i, branch in enumerate((o_cmp[g], o_sel[g], o_win[g])):
            gate = (jnp.dot(sig_hi, ge_ref[g, i], preferred_element_type=F32)
                    + jnp.dot(sig_lo, ge_ref[g, i], preferred_element_type=F32))
            term = gate * unstack(branch)
            out = term if out is None else out + term
        cols = slice(g * gw, (g + 1) * gw)
        o_ref[0, :, cols] = (out * _silu(sg_ref[0, :, cols])).astype(o_ref.dtype)


def _split3(v):
    bf = jnp.bfloat16
    hi = v.astype(bf).astype(np.float64)
    mid = (v - hi).astype(bf).astype(np.float64)
    lo = (v - hi - mid).astype(bf).astype(np.float64)
    return hi, mid, lo


def _slope_lanes():
    heads = NSA_KV * NSA_GROUP
    h = np.arange(1, heads + 1, dtype=np.float32)
    s = np.power(np.float32(2.0), -8.0 * h / heads).astype(np.float32).astype(np.float64) * LOG2E
    parts = _split3(s)
    out = np.zeros((heads, LANES), np.float32)
    for rep in range(2):
        for i, part in enumerate(parts):
            out[:, POS_LANE + 3 * rep + i] = part
    out = np.repeat(out.reshape(NSA_KV, NSA_GROUP, 1, LANES), Q_BLOCK, axis=2)
    return jnp.asarray(out.reshape(NSA_KV, NSA_GROUP * Q_BLOCK, LANES))


def _key_aux(pos, blocks):
    out = np.zeros((pos.shape[0], LANES), np.float32)
    if blocks:
        out[np.arange(pos.shape[0]), pos // SEL_LEN] = 1.0
    out[:, POS_LANE:POS_LANE + 3] = (SEL_LEN * (pos // SEL_LEN))[:, None]
    out[:, POS_LANE + 3:POS_LANE + 6] = (pos % SEL_LEN)[:, None]
    return jnp.asarray(out, dtype=MXU_DTYPE)


def _overlap_t(t, ncmp_rows):
    n_cmp = (t - CMP_LEN) // CMP_STRIDE + 1
    n_sel = t // SEL_LEN
    s_c = np.arange(n_cmp) * CMP_STRIDE
    s_s = np.arange(n_sel) * SEL_LEN
    ov = np.clip(np.minimum(s_c[:, None] + CMP_LEN, s_s[None, :] + SEL_LEN)
                 - np.maximum(s_c[:, None], s_s[None, :]), 0, None).astype(np.float32) / CMP_LEN
    out = np.zeros((LANES, ncmp_rows), np.float32)
    out[:n_sel, :n_cmp] = ov.T
    return jnp.asarray(out)


def _gate_expand():
    width = NSA_GROUP * HEAD_DIM
    e = np.zeros((NSA_KV, 3, LANES, width), np.float32)
    for g in range(NSA_KV):
        for r in range(NSA_GROUP):
            for i in range(3):
                e[g, i, (g * NSA_GROUP + r) * 3 + i, r * HEAD_DIM:(r + 1) * HEAD_DIM] = 1.0
    return jnp.asarray(e, dtype=MXU_DTYPE)


def _nsa(proj3, cols, k_cmp, v_cmp):
    b, t, _ = proj3.shape
    assert t // SEL_LEN <= POS_LANE and t % NSA_KEY_CHUNK == 0 and Q_BLOCK == SEL_LEN
    gw = NSA_GROUP * HEAD_DIM
    width = NSA_KV * gw
    kvw = NSA_KV * HEAD_DIM
    ncmp_rows = k_cmp.shape[2]
    q_c, ks_c, kw_c, gl_c, sg_c = cols
    assert q_c % width == 0 and sg_c % width == 0 and ks_c % kvw == 0 and kw_c % kvw == 0

    def kvspec(c, is_v):
        return pl.BlockSpec((1, t, kvw), lambda i, j: (i, 0, c // kvw + (1 if is_v else 0)))

    def whole(*shape):
        return pl.BlockSpec(shape, lambda i, j: (0,) * len(shape))

    cmpspec = pl.BlockSpec((1, NSA_KV, ncmp_rows, HEAD_DIM), lambda i, j: (i, 0, 0, 0))
    cmp_pos = (CMP_LEN - 1) + CMP_STRIDE * np.arange(ncmp_rows)
    return pl.pallas_call(
        _nsa_kernel,
        grid=(b, t // Q_BLOCK),
        in_specs=[whole(NSA_KV, NSA_GROUP * Q_BLOCK, LANES), whole(t, LANES), whole(ncmp_rows, LANES),
                  pl.BlockSpec((1, Q_BLOCK, width), lambda i, j: (i, j, q_c // width)),
                  pl.BlockSpec((1, Q_BLOCK, LANES), lambda i, j: (i, j, gl_c // LANES)),
                  pl.BlockSpec((1, Q_BLOCK, width), lambda i, j: (i, j, sg_c // width)),
                  kvspec(ks_c, False), kvspec(ks_c, True), kvspec(kw_c, False), kvspec(kw_c, True),
                  cmpspec, cmpspec, whole(LANES, ncmp_rows), whole(NSA_KV, 3, LANES, gw)],
        out_specs=pl.BlockSpec((1, Q_BLOCK, width), lambda i, j: (i, j, 0)),
        out_shape=jax.ShapeDtypeStruct((b, t, width), MXU_DTYPE),
        scratch_shapes=[pltpu.VMEM((NSA_KV, t, 2 * HEAD_DIM), MXU_DTYPE),
                        pltpu.VMEM((NSA_KV, t, 2 * HEAD_DIM), MXU_DTYPE),
                        pltpu.VMEM((NSA_KV, t, HEAD_DIM), MXU_DTYPE), pltpu.VMEM((NSA_KV, t, HEAD_DIM), MXU_DTYPE),
                        pltpu.VMEM((NSA_KV, ncmp_rows, 2 * HEAD_DIM), MXU_DTYPE),
                        pltpu.VMEM((NSA_KV, ncmp_rows, HEAD_DIM), MXU_DTYPE)],
        compiler_params=_params("parallel", "arbitrary"),
        name="nsa_attn",
    )(_slope_lanes(), _key_aux(np.arange(t), True), _key_aux(cmp_pos, False),
      proj3, proj3, proj3, proj3, proj3, proj3, proj3, k_cmp, v_cmp, _overlap_t(t, ncmp_rows), _gate_expand())


def _out_ln_kernel(a0_ref, a1_ref, a2_ref, a3_ref, w0_ref, w1_ref, w2_ref, w3_ref,
                   x_ref, gate_ref, lng_ref, lnb_ref, shift_ref, scale_ref, *rest, alpha, emit_u):
    if emit_u:
        o_ref, u_ref, pre_ref, c_ref, s1_ref, s2_ref = rest
    else:
        (o_ref, pre_ref, c_ref, s1_ref, s2_ref), u_ref = rest, None
    i, j = pl.program_id(0), pl.program_id(1)
    n_row, nj = pl.num_programs(0) - 1, pl.num_programs(1)
    tn = o_ref.shape[1]
    slot = lax.rem(i, 2)

    @pl.when((i == 0) & (j == 0))
    def _():
        c_ref[...] = jnp.zeros(c_ref.shape, F32)
        s1_ref[...] = jnp.zeros(s1_ref.shape, F32)
        s2_ref[...] = jnp.zeros(s2_ref.shape, F32)

    def build():
        y = (jnp.dot(a0_ref[...], w0_ref[0], preferred_element_type=F32)
             + jnp.dot(a1_ref[...], w1_ref[0], preferred_element_type=F32)
             + jnp.dot(a2_ref[...], w2_ref[0], preferred_element_type=F32)
             + jnp.dot(a3_ref[...], w3_ref[0], preferred_element_type=F32))
        pre_ref[j] = alpha * x_ref[...] + (1.0 + gate_ref[0, 0]) * y

    def add_stats(sidx, jj):
        v = pre_ref[jj]
        keep = jnp.where(jj == 0, 0.0, 1.0)
        c = keep * c_ref[sidx] + (1.0 - keep) * (v.sum(axis=-1, keepdims=True) / tn)
        c_ref[sidx] = c
        dev = v - c
        s1_ref[sidx] = keep * s1_ref[sidx] + dev.sum(axis=-1, keepdims=True)
        s2_ref[sidx] = keep * s2_ref[sidx] + (dev * dev).sum(axis=-1, keepdims=True)

    def normalise():
        d = nj * tn
        prev = 1 - slot
        off = s1_ref[prev] / d
        mu = c_ref[prev] + off
        inv = lax.rsqrt(s2_ref[prev] / d - off * off + LN_EPS)
        out = (pre_ref[j] - mu) * inv * lng_ref[0] + lnb_ref[0]
        o_ref[...] = out
        if emit_u:
            u_ref[...] = (out * (1.0 + scale_ref[0, 0]) + shift_ref[0, 0]).astype(u_ref.dtype)

    @pl.when(i == 0)
    def _():
        pl.when(j > 0)(lambda: add_stats(slot, j - 1))
        build()

    @pl.when((i > 0) & (i < n_row))
    def _():
        add_stats(jnp.where(j > 0, slot, 1 - slot), jnp.where(j > 0, j - 1, nj - 1))
        normalise()
        build()

    @pl.when(i == n_row)
    def _():
        pl.when(j == 0)(lambda: add_stats(1 - slot, nj - 1))
        normalise()


def _out_ln(y_rg, y_nsa, y_hg, w_out_b, x2, mod4, ln_g, ln_b, layer, alpha, t, emit_u):
    m, d = x2.shape
    depth = w_out_b.shape[0]
    kb = y_rg.shape[1]
    assert y_nsa.shape[1] == 2 * kb and y_hg.shape[1] == kb and d == 4 * kb
    tm, tn = OUT_TILE
    nj, n_row = d // tn, m // tm
    assert t % tm == 0
    mrow = lambda i: jnp.minimum(i, n_row - 1)
    mcol = lambda i, j: jnp.where(i < n_row, j, nj - 1)
    orow = lambda i: jnp.maximum(i - 1, 0)
    ocol = lambda i, j: jnp.where(i > 0, j, 0)
    aspec = lambda c: pl.BlockSpec((tm, kb), lambda i, j: (mrow(i), c))
    wspec = lambda r: pl.BlockSpec((1, kb, tn), lambda i, j: (layer, r, mcol(i, j)))
    vspec = pl.BlockSpec((1, 1, tn), lambda i, j: (layer, 0, ocol(i, j)))
    nxt = min(layer + 1, depth - 1)
    modspec = lambda part: pl.BlockSpec(
        (1, 1, 1, tn), lambda i, j: (nxt, (orow(i) * tm) // t, 0, part * nj + ocol(i, j)))
    ospec = pl.BlockSpec((tm, tn), lambda i, j: (orow(i), ocol(i, j)))
    oshape = jax.ShapeDtypeStruct((m, d), F32)
    return pl.pallas_call(
        functools.partial(_out_ln_kernel, alpha=alpha, emit_u=emit_u),
        grid=(n_row + 1, nj),
        in_specs=[aspec(0), aspec(0), aspec(1), aspec(0), wspec(0), wspec(1), wspec(2), wspec(3),
                  pl.BlockSpec((tm, tn), lambda i, j: (mrow(i), mcol(i, j))),
                  pl.BlockSpec((1, 1, 1, tn), lambda i, j: (layer, (mrow(i) * tm) // t, 0, 2 * nj + mcol(i, j))),
                  vspec, vspec, modspec(0), modspec(1)],
        out_specs=[ospec, ospec] if emit_u else ospec,
        out_shape=[oshape, jax.ShapeDtypeStruct((m, d), MXU_DTYPE)] if emit_u else oshape,
        scratch_shapes=[pltpu.VMEM((nj, tm, tn), F32)] + [pltpu.VMEM((2, tm, 1), F32)] * 3,
        compiler_params=_params("arbitrary", "arbitrary"),
        name="out_ln",
    )(y_rg, y_nsa, y_nsa, y_hg, w_out_b, w_out_b, w_out_b, w_out_b, x2, mod4,
      ln_g.reshape(depth, 1, d), ln_b.reshape(depth, 1, d), mod4, mod4)


def kernel(x, c, w_ada, b_ada, w_in, rg_conv_w, rg_conv_b, rg_w_a, rg_b_a, rg_w_x, rg_b_x, rg_lambda,
           nsa_pe_k, nsa_pe_v, nsa_cmp_w1_k, nsa_cmp_w2_k, nsa_cmp_w1_v, nsa_cmp_w2_v,
           hg_lower_bounds, hg_norm_g, w_out, ln_g, ln_b):
    b, t, d = x.shape
    depth = w_ada.shape[0]
    m = b * t
    d_rg, d_nsa, d_hg = d // 4, d // 2, d // 4
    kv_cols = 2 * NSA_KV * HEAD_DIM
    n_gl = 3 * NSA_KV * NSA_GROUP
    alpha = (2.0 * depth) ** 0.25
    assert d_nsa == NSA_KV * NSA_GROUP * HEAD_DIM and b <= SUBLANES

    gl_pad = 512
    c_q = 2 * d_rg
    c_kvc = c_q + d_nsa
    c_kvs = c_kvc + kv_cols
    c_kvw = c_kvs + kv_cols
    c_gl = c_kvw + kv_cols
    c_sg = c_gl + gl_pad
    c_hg = c_sg + d_nsa
    w_in_p = _prep_w_in(jnp.swapaxes(w_in, 1, 2), c_gl, n_gl, gl_pad)
    w_out_b = w_out.astype(MXU_DTYPE)
    w1_k, w1_v = nsa_cmp_w1_k.astype(MXU_DTYPE), nsa_cmp_w1_v.astype(MXU_DTYPE)

    c_pad = jnp.pad(c, ((0, SUBLANES - b), (0, 0)))
    mod4 = _ada(c_pad, w_ada, b_ada).reshape(depth, SUBLANES, 1, 3 * d)

    u = _modulate(x, mod4, 0).reshape(m, d)
    x = x.reshape(m, d)
    for layer in range(depth):
        proj3 = _in_proj(u, w_in_p, layer).reshape(b, t, -1)
        y_rg = _rglru(proj3, rg_conv_w, rg_conv_b, rg_w_a, rg_b_a, rg_w_x, rg_b_x, rg_lambda, layer, d_rg)
        k_cmp, v_cmp = _compress(proj3, c_kvc, nsa_pe_k, nsa_pe_v, w1_k, nsa_cmp_w2_k, w1_v, nsa_cmp_w2_v, layer)
        y_nsa = _nsa(proj3, (c_q, c_kvs, c_kvw, c_gl, c_sg), k_cmp, v_cmp)
        y_hg = _hgrn2(proj3, c_hg, hg_lower_bounds, hg_norm_g, layer, d_hg)
        emit_u = layer + 1 < depth
        res = _out_ln(y_rg.reshape(m, d_rg), y_nsa.reshape(m, d_nsa), y_hg.reshape(m, d_hg), w_out_b,
                      x, mod4, ln_g, ln_b, layer, alpha, t, emit_u)
        x, u = res if emit_u else (res, None)
    return x.reshape(b, t, d)
```
